```python
import jax, jax.numpy as jnp
from jax import lax
import numpy as np

D_MODEL = 1024
BATCH = 2
SEQ = 8192
DEPTH = 1

GRID_W = 64
CTX_LEN = 256
D_MIX = D_MODEL
NA_WIDTH = D_MIX // 2
NA_HEADS = 8
NA_HEAD_DIM = NA_WIDTH // NA_HEADS
NA_KH_MAX = 8
NA_KW = 16
SGU_WIDTH = D_MIX - NA_WIDTH
SGU_GROUPS = 4
SGU_GROUP_DIM = SGU_WIDTH // SGU_GROUPS
CHUNK = 128
N_EXPERTS = 32
TOP_K = 4
D_FF_EXPERT = D_MODEL
SWIGLU_LIMIT = 7.0
SWIGLU_ALPHA = 1.702
LN_EPS = 1e-5
DEEPNORM_ALPHA = (2.0 * DEPTH) ** 0.25
DEEPNORM_BETA = (8.0 * DEPTH) ** -0.25
Q_OFF = 0
K_OFF = NA_WIDTH
V_OFF = 2 * NA_WIDTH
U_OFF = 3 * NA_WIDTH
D_IN = U_OFF + 2 * SGU_WIDTH

kernel_name = "hybrid_natten_sgu_moe_dit_block"


def layer_norm(x, g=None, b=None):
    xf = x.astype(jnp.float32)
    mu = jnp.mean(xf, axis=-1, keepdims=True)
    var = jnp.mean(jnp.square(xf - mu), axis=-1, keepdims=True)
    y = (xf - mu) * lax.rsqrt(var + LN_EPS)
    if g is not None:
        y = y * g.astype(jnp.float32) + b.astype(jnp.float32)
    return y.astype(x.dtype)


def ada_params(cond, ada_w, ada_b):
    m = jax.nn.silu(cond) @ ada_w + ada_b
    return jnp.split(m, 6, axis=-1)


def modulate(h, shift, scale):
    return layer_norm(h) * (1 + scale) + shift


def split_heads(t):
    b, l, _ = t.shape
    return t.reshape(b, l, NA_HEADS, NA_HEAD_DIM)


def neighbourhood_attention(q, k, v, k_ctx, v_ctx, rpb):
    b, s, h, dh = q.shape
    rows = s // GRID_W
    kh = min(NA_KH_MAX, rows)
    kw = NA_KW
    scale = dh ** -0.5
    qg = q.reshape(b, rows, GRID_W, h, dh)
    kg = k.reshape(b, rows, GRID_W, h, dh)
    vg = v.reshape(b, rows, GRID_W, h, dh)
    col = jnp.arange(GRID_W)
    col_start = jnp.clip(col - kw // 2, 0, GRID_W - kw)
    col_idx = col_start[:, None] + jnp.arange(kw)[None, :]
    dc_idx = col_idx - col[:, None] + (NA_KW - 1)

    def row_block(r):
        rs = jnp.clip(r - kh // 2, 0, rows - kh)
        k_rows = lax.dynamic_slice_in_dim(kg, rs, kh, axis=1)
        v_rows = lax.dynamic_slice_in_dim(vg, rs, kh, axis=1)
        k_nb = k_rows[:, :, col_idx]
        v_nb = v_rows[:, :, col_idx]
        q_r = lax.dynamic_index_in_dim(qg, r, axis=1, keepdims=False)
        dr_idx = rs + jnp.arange(kh) - r + (NA_KH_MAX - 1)
        bias = rpb[:, dr_idx[None, :, None], dc_idx[:, None, :]]
        s_nb = jnp.einsum('bwhd,biwjhd->bhwij', q_r, k_nb).astype(jnp.float32) * scale \
            + bias.astype(jnp.float32)
        s_ctx = jnp.einsum('bwhd,bchd->bhwc', q_r, k_ctx).astype(jnp.float32) * scale
        scores = jnp.concatenate([s_nb.reshape(b, h, GRID_W, kh * kw), s_ctx], axis=-1)
        p = jax.nn.softmax(scores, axis=-1).astype(q.dtype)
        p_nb = p[..., :kh * kw].reshape(b, h, GRID_W, kh, kw)
        p_ctx = p[..., kh * kw:]
        return jnp.einsum('bhwij,biwjhd->bwhd', p_nb, v_nb) + jnp.einsum('bhwc,bchd->bwhd', p_ctx, v_ctx)

    out = lax.map(row_block, jnp.arange(rows))
    return jnp.moveaxis(out, 0, 1).reshape(b, s, h * dh)


def context_attention(q, k, v):
    b, l, h, dh = q.shape
    s = jnp.einsum('bqhd,bkhd->bhqk', q, k).astype(jnp.float32) * (dh ** -0.5)
    p = jax.nn.softmax(s, axis=-1).astype(q.dtype)
    return jnp.einsum('bhqk,bkhd->bqhd', p, v).reshape(b, l, h * dh)


def spatial_gating(z, ln_g, ln_b, w_s, b_s):
    b, l, _ = z.shape
    u, g = jnp.split(z, 2, axis=-1)
    g = layer_norm(g, ln_g, ln_b)
    g = g.reshape(b, l // CHUNK, CHUNK, SGU_GROUPS, SGU_GROUP_DIM)
    g = jnp.einsum('gij,bnjgc->bnigc', w_s, g) + b_s.T[None, None, :, :, None]
    return u * g.reshape(b, l, SGU_WIDTH)


def moe_ffn(h, router_w, router_b, w_gu, b_gu, w_down, b_down):
    shp = h.shape
    t = h.reshape(-1, shp[-1])
    logits = (t @ router_w + router_b).astype(jnp.float32)
    top_vals, top_idx = lax.top_k(logits, TOP_K)
    gates = jax.nn.softmax(top_vals, axis=-1)
    combine = jnp.sum(jax.nn.one_hot(top_idx, N_EXPERTS, dtype=jnp.float32) * gates[..., None],
                      axis=1).astype(t.dtype)

    def expert_step(acc, xs):
        wgu, bgu, wd, bd, wgt = xs
        gu = t @ wgu + bgu
        gate = jnp.minimum(gu[:, :D_FF_EXPERT], SWIGLU_LIMIT)
        lin = jnp.clip(gu[:, D_FF_EXPERT:], -SWIGLU_LIMIT, SWIGLU_LIMIT)
        y = ((lin + 1) * (gate * jax.nn.sigmoid(SWIGLU_ALPHA * gate))) @ wd + bd
        return acc + wgt[:, None] * y, None

    out, _ = lax.scan(expert_step, jnp.zeros_like(t), (w_gu, b_gu, w_down, b_down, combine.T))
    return out.reshape(shp)


def hybrid_layer(x, ctx, c, c_ctx, ada_w, ada_b, w_in, rpb, sgu_ln_g, sgu_ln_b, sgu_w, sgu_b,
                 w_out, ln1_g, ln1_b, ln2_g, ln2_b, router_w, router_b,
                 exp_w_gu, exp_b_gu, exp_w_down, exp_b_down, last):
    sh1, sc1, g1, sh2, sc2, g2 = ada_params(c[:, None, :], ada_w, ada_b)
    csh1, csc1, cg1, csh2, csc2, cg2 = ada_params(c_ctx, ada_w, ada_b)

    hc = modulate(ctx, csh1, csc1)
    if last:
        kv_c = hc @ w_in[:, K_OFF:U_OFF]
        k_c, v_c = split_heads(kv_c[..., :NA_WIDTH]), split_heads(kv_c[..., NA_WIDTH:])
    else:
        proj_c = hc @ w_in
        q_c = split_heads(proj_c[..., Q_OFF:K_OFF])
        k_c = split_heads(proj_c[..., K_OFF:V_OFF])
        v_c = split_heads(proj_c[..., V_OFF:U_OFF])
        att_c = context_attention(q_c, k_c, v_c)
        sgu_c = spatial_gating(jax.nn.gelu(proj_c[..., U_OFF:]), sgu_ln_g, sgu_ln_b, sgu_w, sgu_b)
        mix_c = jnp.concatenate([att_c, sgu_c], axis=-1) @ w_out

    hx = modulate(x, sh1, sc1)
    proj_x = hx @ w_in
    q_x = split_heads(proj_x[..., Q_OFF:K_OFF])
    k_x = split_heads(proj_x[..., K_OFF:V_OFF])
    v_x = split_heads(proj_x[..., V_OFF:U_OFF])
    att_x = neighbourhood_attention(q_x, k_x, v_x, k_c, v_c, rpb)
    sgu_x = spatial_gating(jax.nn.gelu(proj_x[..., U_OFF:]), sgu_ln_g, sgu_ln_b, sgu_w, sgu_b)
    mix_x = jnp.concatenate([att_x, sgu_x], axis=-1) @ w_out
    x = layer_norm(DEEPNORM_ALPHA * x + g1 * mix_x, ln1_g, ln1_b)
    ffn_x = moe_ffn(modulate(x, sh2, sc2), router_w, router_b, exp_w_gu, exp_b_gu, exp_w_down, exp_b_down)
    x = layer_norm(DEEPNORM_ALPHA * x + g2 * ffn_x, ln2_g, ln2_b)

    if not last:
        ctx = layer_norm(DEEPNORM_ALPHA * ctx + cg1 * mix_c, ln1_g, ln1_b)
        ffn_c = moe_ffn(modulate(ctx, csh2, csc2), router_w, router_b, exp_w_gu, exp_b_gu,
                        exp_w_down, exp_b_down)
        ctx = layer_norm(DEEPNORM_ALPHA * ctx + cg2 * ffn_c, ln2_g, ln2_b)
    return x, ctx


def setup_inputs(seed: int = 0) -> dict:
    key = jax.random.key(seed)
    ks = jax.random.split(key, 24)
    f32 = jnp.float32

    def nrm(k, shape, std):
        return jax.random.normal(k, shape, f32) * std

    L = DEPTH
    return {
        "x": nrm(ks[0], (BATCH, SEQ, D_MODEL), 1.0),
        "c": nrm(ks[1], (BATCH, D_MODEL), 1.0),
        "ctx": nrm(ks[2], (BATCH, CTX_LEN, D_MODEL), 1.0),
        "c_ctx": nrm(ks[3], (D_MODEL,), 1.0),
        "ada_w": nrm(ks[4], (L, D_MODEL, 6 * D_MODEL), 0.5 * D_MODEL ** -0.5),
        "ada_b": nrm(ks[5], (L, 6 * D_MODEL), 0.02),
        "w_in": nrm(ks[6], (L, D_MODEL, D_IN), D_MODEL ** -0.5),
        "rpb": nrm(ks[7], (L, NA_HEADS, 2 * NA_KH_MAX - 1, 2 * NA_KW - 1), 0.1),
        "sgu_ln_g": 1.0 + nrm(ks[8], (L, SGU_WIDTH), 0.01),
        "sgu_ln_b": nrm(ks[9], (L, SGU_WIDTH), 0.01),
        "sgu_w": nrm(ks[10], (L, SGU_GROUPS, CHUNK, CHUNK), CHUNK ** -0.5),
        "sgu_b": 1.0 + nrm(ks[11], (L, SGU_GROUPS, CHUNK), 0.1),
        "w_out": nrm(ks[12], (L, D_MIX, D_MODEL), DEEPNORM_BETA * D_MIX ** -0.5),
        "ln1_g": 1.0 + nrm(ks[13], (L, D_MODEL), 0.01),
        "ln1_b": nrm(ks[14], (L, D_MODEL), 0.01),
        "ln2_g": 1.0 + nrm(ks[15], (L, D_MODEL), 0.01),
        "ln2_b": nrm(ks[16], (L, D_MODEL), 0.01),
        "router_w": nrm(ks[17], (L, D_MODEL, N_EXPERTS), D_MODEL ** -0.5),
        "router_b": nrm(ks[18], (L, N_EXPERTS), 0.01),
        "exp_w_gu": nrm(ks[19], (L, N_EXPERTS, D_MODEL, 2 * D_FF_EXPERT), D_MODEL ** -0.5),
        "exp_b_gu": nrm(ks[20], (L, N_EXPERTS, 2 * D_FF_EXPERT), 0.01),
        "exp_w_down": nrm(ks[21], (L, N_EXPERTS, D_FF_EXPERT, D_MODEL), DEEPNORM_BETA * D_FF_EXPERT ** -0.5),
        "exp_b_down": nrm(ks[22], (L, N_EXPERTS, D_MODEL), 0.01),
    }


def reference(x, c, ctx, c_ctx, ada_w, ada_b, w_in, rpb, sgu_ln_g, sgu_ln_b, sgu_w, sgu_b,
              w_out, ln1_g, ln1_b, ln2_g, ln2_b, router_w, router_b,
              exp_w_gu, exp_b_gu, exp_w_down, exp_b_down):
    for i in range(DEPTH):
        x, ctx = hybrid_layer(
            x, ctx, c, c_ctx, ada_w[i], ada_b[i], w_in[i], rpb[i], sgu_ln_g[i], sgu_ln_b[i],
            sgu_w[i], sgu_b[i], w_out[i], ln1_g[i], ln1_b[i], ln2_g[i], ln2_b[i],
            router_w[i], router_b[i], exp_w_gu[i], exp_b_gu[i], exp_w_down[i], exp_b_down[i],
            last=(i == DEPTH - 1))
    return x
```

```python
import functools

import jax
import jax.numpy as jnp
from jax import lax
from jax.experimental import pallas as pl
from jax.experimental.pallas import tpu as pltpu

F32 = jnp.float32
BF16 = jnp.bfloat16

GRID_W = 64
NA_HEADS = 8
NA_HEAD_DIM = 64
NA_WIDTH = NA_HEADS * NA_HEAD_DIM
NA_KH = 8
NA_KW = 16
SGU_GROUPS = 4
SGU_GROUP_DIM = 128
SGU_WIDTH = SGU_GROUPS * SGU_GROUP_DIM
CHUNK = 128
N_EXPERTS = 32
TOP_K = 4
SWIGLU_LIMIT = 7.0
SWIGLU_ALPHA = 1.702
LN_EPS = 1e-5
DEPTH = 1
DEEPNORM_ALPHA = (2.0 * DEPTH) ** 0.25
MASKED = -1e30

VMEM_LIMIT_BYTES = 52 * 1024 * 1024

ROW_TILE = 512
ATT_ROWS = 8
MOE_CHUNK = 1024
MOE_ROWS = 128


def _params(*sem):
    return pltpu.CompilerParams(dimension_semantics=sem, vmem_limit_bytes=VMEM_LIMIT_BYTES)


def _normalize(x):
    mu = jnp.mean(x, axis=-1, keepdims=True)
    xc = x - mu
    var = jnp.mean(xc * xc, axis=-1, keepdims=True)
    return xc * lax.rsqrt(var + LN_EPS)


def _ada_kernel(c_ref, w_ref, b_ref, o_ref):
    s = c_ref[...]
    s = s * jax.nn.sigmoid(s)
    o_ref[...] = jnp.dot(s, w_ref[...], precision=lax.Precision.HIGHEST,
                         preferred_element_type=F32) + b_ref[...]


def _ada(cond_rows, ada_w, ada_b):
    d = cond_rows.shape[1]
    n_out = ada_w.shape[1]
    return pl.pallas_call(
        _ada_kernel,
        grid=(n_out // d,),
        in_specs=[pl.BlockSpec((8, d), lambda j: (0, 0)),
                  pl.BlockSpec((d, d), lambda j: (0, j)),
                  pl.BlockSpec((1, d), lambda j: (0, j))],
        out_specs=pl.BlockSpec((8, d), lambda j: (0, j)),
        out_shape=jax.ShapeDtypeStruct((8, n_out), F32),
        compiler_params=_params("arbitrary"),
        name="ada",
    )(cond_rows, ada_w, ada_b.reshape(1, n_out))


def _ctx_kv_kernel(ctx_ref, sh_ref, sc_ref, w_ref, k_ref, v_ref):
    h = _normalize(ctx_ref[0]) * (1.0 + sc_ref[...]) + sh_ref[...]
    kv = jnp.dot(h.astype(BF16), w_ref[...], preferred_element_type=F32)
    k_ref[0] = kv[:, :NA_WIDTH].astype(BF16)
    v_ref[0] = kv[:, NA_WIDTH:].astype(BF16)


def _ctx_kv(ctx, csh1, csc1, w_kv):
    b, l, d = ctx.shape
    out = jax.ShapeDtypeStruct((b, l, NA_WIDTH), BF16)
    return pl.pallas_call(
        _ctx_kv_kernel,
        grid=(b,),
        in_specs=[pl.BlockSpec((1, l, d), lambda i: (i, 0, 0)),
                  pl.BlockSpec((1, d), lambda i: (0, 0)),
                  pl.BlockSpec((1, d), lambda i: (0, 0)),
                  pl.BlockSpec((d, 2 * NA_WIDTH), lambda i: (0, 0))],
        out_specs=[pl.BlockSpec((1, l, NA_WIDTH), lambda i: (i, 0, 0)),
                   pl.BlockSpec((1, l, NA_WIDTH), lambda i: (i, 0, 0))],
        out_shape=[out, out],
        compiler_params=_params("arbitrary"),
        name="ctx_kv",
    )(ctx, csh1.reshape(1, d), csc1.reshape(1, d), w_kv)


def _proj_in_kernel(x_ref, sh_ref, sc_ref, w_ref, lng_ref, lnb_ref, ws_ref, bs_ref,
                    q_ref, k_ref, v_ref, s_ref):
    h = (_normalize(x_ref[0]) * (1.0 + sc_ref[0]) + sh_ref[0]).astype(BF16)

    def proj(lo, width):
        return jnp.dot(h, w_ref[:, lo:lo + width], preferred_element_type=F32)

    q_ref[0] = (proj(0, NA_WIDTH) * (NA_HEAD_DIM ** -0.5)).astype(BF16)
    k_ref[0] = proj(NA_WIDTH, NA_WIDTH).astype(BF16)
    v_ref[0] = proj(2 * NA_WIDTH, NA_WIDTH).astype(BF16)
    u = jax.nn.gelu(proj(3 * NA_WIDTH, SGU_WIDTH))
    g = jax.nn.gelu(proj(3 * NA_WIDTH + SGU_WIDTH, SGU_WIDTH))
    gn = (_normalize(g) * lng_ref[...] + lnb_ref[...]).astype(BF16)
    rows = h.shape[0]
    for n in range(rows // CHUNK):
        r0 = n * CHUNK
        for grp in range(SGU_GROUPS):
            c0 = grp * SGU_GROUP_DIM
            mixed = jnp.dot(ws_ref[grp], gn[r0:r0 + CHUNK, c0:c0 + SGU_GROUP_DIM],
                            preferred_element_type=F32) + bs_ref[grp]
            s_ref[0, r0:r0 + CHUNK, c0:c0 + SGU_GROUP_DIM] = (
                u[r0:r0 + CHUNK, c0:c0 + SGU_GROUP_DIM] * mixed).astype(BF16)


def _proj_in(x, sh1, sc1, w_in, sgu_ln_g, sgu_ln_b, sgu_w, sgu_b):
    b, s, d = x.shape
    d_in = w_in.shape[1]
    tm = min(ROW_TILE, s)
    out = jax.ShapeDtypeStruct((b, s, NA_WIDTH), BF16)
    row_spec = pl.BlockSpec((1, tm, NA_WIDTH), lambda i, j: (i, j, 0))
    mod_spec = pl.BlockSpec((1, 1, d), lambda i, j: (i, 0, 0))
    bs = jnp.broadcast_to(sgu_b[:, :, None], (SGU_GROUPS, CHUNK, SGU_GROUP_DIM))
    return pl.pallas_call(
        _proj_in_kernel,
        grid=(b, s // tm),
        in_specs=[pl.BlockSpec((1, tm, d), lambda i, j: (i, j, 0)),
                  mod_spec, mod_spec,
                  pl.BlockSpec((d, d_in), lambda i, j: (0, 0)),
                  pl.BlockSpec((1, SGU_WIDTH), lambda i, j: (0, 0)),
                  pl.BlockSpec((1, SGU_WIDTH), lambda i, j: (0, 0)),
                  pl.BlockSpec((SGU_GROUPS, CHUNK, CHUNK), lambda i, j: (0, 0, 0)),
                  pl.BlockSpec((SGU_GROUPS, CHUNK, SGU_GROUP_DIM), lambda i, j: (0, 0, 0))],
        out_specs=[row_spec, row_spec, row_spec, row_spec],
        out_shape=[out, out, out, out],
        compiler_params=_params("parallel", "parallel"),
        name="proj_in",
    )(x, sh1.reshape(b, 1, d), sc1.reshape(b, 1, d), w_in,
      sgu_ln_g.reshape(1, SGU_WIDTH), sgu_ln_b.reshape(1, SGU_WIDTH), sgu_w.astype(BF16), bs)


def _bias_table(rpb):
    c = jnp.arange(GRID_W)
    cs = jnp.clip(c - NA_KW // 2, 0, GRID_W - NA_KW)
    kc = jnp.arange(GRID_W)
    valid = (kc[None, :] >= cs[:, None]) & (kc[None, :] < cs[:, None] + NA_KW)
    dci = jnp.clip(kc[None, :] - c[:, None] + NA_KW - 1, 0, 2 * NA_KW - 2)
    dri = jnp.arange(NA_KH)[None, :] - jnp.arange(NA_KH)[:, None] + NA_KH - 1
    tab = rpb[:, dri[:, :, None, None], dci[None, None, :, :]]
    tab = jnp.where(valid[None, None, None], tab.astype(F32), MASKED)
    tab = tab.transpose(1, 0, 3, 2, 4)
    return tab.reshape(NA_KH, NA_HEADS * GRID_W, NA_KH * GRID_W)


def _natten_kernel(q_ref, k_ref, v_ref, kc_ref, vc_ref, bias_ref, o_ref, *, grid_rows):
    stacked = NA_HEADS * GRID_W
    row_head = lax.broadcasted_iota(jnp.int32, (stacked, NA_WIDTH), 0) // GRID_W
    lane_head = lax.broadcasted_iota(jnp.int32, (stacked, NA_WIDTH), 1) // NA_HEAD_DIM
    own_head = row_head == lane_head
    kc = kc_ref[0]
    vc = vc_ref[0]
    nt = (((1,), (1,)), ((), ()))

    def one_row(i, carry):
        r = pl.program_id(1) * ATT_ROWS + i
        rs = jnp.clip(r - NA_KH // 2, 0, grid_rows - NA_KH)
        k0 = pl.multiple_of(rs * GRID_W, GRID_W)
        q0 = pl.multiple_of(i * GRID_W, GRID_W)
        q = q_ref[0, pl.ds(q0, GRID_W), :]
        qs = jnp.where(own_head, jnp.concatenate([q] * NA_HEADS, axis=0), jnp.zeros((), BF16))
        kr = k_ref[0, pl.ds(k0, NA_KH * GRID_W), :]
        vr = v_ref[0, pl.ds(k0, NA_KH * GRID_W), :]
        s_nb = lax.dot_general(qs, kr, nt, preferred_element_type=F32) + bias_ref[r - rs]
        s_cx = lax.dot_general(qs, kc, nt, preferred_element_type=F32)
        m = jnp.maximum(jnp.max(s_nb, axis=-1, keepdims=True), jnp.max(s_cx, axis=-1, keepdims=True))
        p_nb = jnp.exp(s_nb - m)
        p_cx = jnp.exp(s_cx - m)
        denom = jnp.sum(p_nb, axis=-1, keepdims=True) + jnp.sum(p_cx, axis=-1, keepdims=True)
        o = (jnp.dot(p_nb.astype(BF16), vr, preferred_element_type=F32)
             + jnp.dot(p_cx.astype(BF16), vc, preferred_element_type=F32)) / denom
        o = jnp.where(own_head, o, 0.0)
        acc = o[0:GRID_W]
        for h in range(1, NA_HEADS):
            acc = acc + o[h * GRID_W:(h + 1) * GRID_W]
        o_ref[0, pl.ds(q0, GRID_W), :] = acc.astype(BF16)
        return carry

    lax.fori_loop(0, ATT_ROWS, one_row, 0)


def _natten(q, k, v, k_c, v_c, bias):
    b, s, w = q.shape
    l = k_c.shape[1]
    grid_rows = s // GRID_W
    tq = ATT_ROWS * GRID_W
    full = pl.BlockSpec((1, s, w), lambda i, j: (i, 0, 0), pipeline_mode=pl.Buffered(1))
    ctx = pl.BlockSpec((1, l, w), lambda i, j: (i, 0, 0))
    return pl.pallas_call(
        functools.partial(_natten_kernel, grid_rows=grid_rows),
        grid=(b, grid_rows // ATT_ROWS),
        in_specs=[pl.BlockSpec((1, tq, w), lambda i, j: (i, j, 0)),
                  full, full, ctx, ctx,
                  pl.BlockSpec(bias.shape, lambda i, j: (0, 0, 0), pipeline_mode=pl.Buffered(1))],
        out_specs=pl.BlockSpec((1, tq, w), lambda i, j: (i, j, 0)),
        out_shape=jax.ShapeDtypeStruct((b, s, w), BF16),
        compiler_params=_params("parallel", "arbitrary"),
        name="natten",
    )(q, k, v, k_c, v_c, bias)


def _mix_out_kernel(att_ref, sgu_ref, x_ref, wo_ref, g1_ref, sh_ref, sc_ref, lng_ref, lnb_ref,
                    wr_ref, br_ref, x1_ref, t_ref, lg_ref):
    mix = (jnp.dot(att_ref[0], wo_ref[:NA_WIDTH, :], preferred_element_type=F32)
           + jnp.dot(sgu_ref[0], wo_ref[NA_WIDTH:, :], preferred_element_type=F32))
    x1 = _normalize(DEEPNORM_ALPHA * x_ref[0] + g1_ref[0] * mix) * lng_ref[...] + lnb_ref[...]
    x1_ref[0] = x1
    t = _normalize(x1) * (1.0 + sc_ref[0]) + sh_ref[0]
    t_ref[0] = t.astype(BF16)
    lg_ref[...] = lax.dot_general(wr_ref[...], t, (((1,), (1,)), ((), ())),
                                  precision=lax.Precision.HIGHEST,
                                  preferred_element_type=F32) + br_ref[...]


def _mix_out(att, sgu, x, w_out, g1, sh2, sc2, ln_g, ln_b, router_w, router_b):
    b, s, d = x.shape
    tm = min(ROW_TILE, s)
    nj = s // tm
    row = lambda width: pl.BlockSpec((1, tm, width), lambda i, j: (i, j, 0))
    mod_spec = pl.BlockSpec((1, 1, d), lambda i, j: (i, 0, 0))
    vec_spec = pl.BlockSpec((1, d), lambda i, j: (0, 0))
    return pl.pallas_call(
        _mix_out_kernel,
        grid=(b, nj),
        in_specs=[row(NA_WIDTH), row(SGU_WIDTH), row(d),
                  pl.BlockSpec((d, d), lambda i, j: (0, 0)),
                  mod_spec, mod_spec, mod_spec, vec_spec, vec_spec,
                  pl.BlockSpec((N_EXPERTS, d), lambda i, j: (0, 0)),
                  pl.BlockSpec((N_EXPERTS, 1), lambda i, j: (0, 0))],
        out_specs=[row(d), row(d),
                   pl.BlockSpec((N_EXPERTS, tm), lambda i, j: (0, i * nj + j))],
        out_shape=[jax.ShapeDtypeStruct((b, s, d), F32),
                   jax.ShapeDtypeStruct((b, s, d), BF16),
                   jax.ShapeDtypeStruct((N_EXPERTS, b * s), F32)],
        compiler_params=_params("parallel", "parallel"),
        name="mix_out",
    )(att, sgu, x, w_out, g1.reshape(b, 1, d), sh2.reshape(b, 1, d), sc2.reshape(b, 1, d),
      ln_g.reshape(1, d), ln_b.reshape(1, d), router_w.T, router_b.reshape(N_EXPERTS, 1))


SCAN_BLOCK = 256


def _route_kernel(lg_ref, gate_ref, slot_ref, cnt_ref):
    logits = lg_ref[...]
    n_tok = logits.shape[1]
    expert = lax.broadcasted_iota(jnp.int32, logits.shape, 0)
    work = logits
    chosen = jnp.zeros(logits.shape, jnp.bool_)
    top = None
    for kk in range(TOP_K):
        m = jnp.max(work, axis=0, keepdims=True)
        first = jnp.min(jnp.where(work == m, expert, N_EXPERTS), axis=0, keepdims=True)
        pick = expert == first
        chosen = jnp.logical_or(chosen, pick)
        work = jnp.where(pick, -jnp.inf, work)
        if kk == 0:
            top = m
    e = jnp.where(chosen, jnp.exp(logits - top), 0.0)
    gate_ref[...] = e / jnp.sum(e, axis=0, keepdims=True)
    tri = (lax.broadcasted_iota(jnp.int32, (SCAN_BLOCK, SCAN_BLOCK), 0)
           <= lax.broadcasted_iota(jnp.int32, (SCAN_BLOCK, SCAN_BLOCK), 1)).astype(BF16)
    sel = jnp.where(chosen, 1.0, 0.0).astype(BF16)
    carry = jnp.zeros((N_EXPERTS, 1), F32)
    for blk in range(n_tok // SCAN_BLOCK):
        lo = blk * SCAN_BLOCK
        run = jnp.dot(sel[:, lo:lo + SCAN_BLOCK], tri, preferred_element_type=F32) + carry
        slot_ref[:, lo:lo + SCAN_BLOCK] = jnp.where(
            chosen[:, lo:lo + SCAN_BLOCK], run.astype(jnp.int32) - 1, -1)
        carry = run[:, SCAN_BLOCK - 1:SCAN_BLOCK]
    cnt_ref[0] = jnp.broadcast_to(carry, (N_EXPERTS, 128)).astype(jnp.int32)


def _route(logits_t, chunk):
    n = logits_t.shape[1]
    nc = n // chunk
    blk = pl.BlockSpec((N_EXPERTS, chunk), lambda c: (0, c))
    return pl.pallas_call(
        _route_kernel,
        grid=(nc,),
        in_specs=[blk],
        out_specs=[blk, blk, pl.BlockSpec((1, N_EXPERTS, 128), lambda c: (c, 0, 0))],
        out_shape=[jax.ShapeDtypeStruct((N_EXPERTS, n), F32),
                   jax.ShapeDtypeStruct((N_EXPERTS, n), jnp.int32),
                   jax.ShapeDtypeStruct((nc, N_EXPERTS, 128), jnp.int32)],
        compiler_params=_params("parallel"),
        name="route",
    )(logits_t)


def _moe_kernel(cnt_ref, t_ref, gate_ref, slot_ref, wgu_ref, bgu_ref, wd_ref, bd_ref,
                x1_ref, g2_ref, lng_ref, lnb_ref, o_ref):
    c = pl.program_id(0)
    e = pl.program_id(1)
    d_ff = wd_ref.shape[1]
    chunk = t_ref.shape[0]

    @pl.when(e == 0)
    def _():
        o_ref[...] = jnp.zeros_like(o_ref)

    slot_row = slot_ref[pl.ds(e, 1), :]
    gate_row = gate_ref[pl.ds(e, 1), :]
    n_pass = (cnt_ref[c, e] + MOE_ROWS - 1) // MOE_ROWS

    def one_pass(i, carry):
        rows = i * MOE_ROWS + lax.broadcasted_iota(jnp.int32, (MOE_ROWS, chunk), 0)
        hit = slot_row == rows
        gather = jnp.where(hit, 1.0, 0.0).astype(BF16)
        scatter = jnp.where(hit, gate_row, 0.0).astype(BF16)
        xg = jnp.dot(gather, t_ref[...], preferred_element_type=F32).astype(BF16)
        gu = jnp.dot(xg, wgu_ref[0], preferred_element_type=F32) + bgu_ref[0]
        gate = jnp.minimum(gu[:, :d_ff], SWIGLU_LIMIT)
        lin = jnp.clip(gu[:, d_ff:], -SWIGLU_LIMIT, SWIGLU_LIMIT)
        act = ((lin + 1.0) * (gate * jax.nn.sigmoid(SWIGLU_ALPHA * gate))).astype(BF16)
        y = jnp.dot(act, wd_ref[0], preferred_element_type=F32) + bd_ref[0]
        o_ref[...] += lax.dot_general(scatter, y.astype(BF16), (((0,), (0,)), ((), ())),
                                      preferred_element_type=F32)
        return carry

    lax.fori_loop(0, n_pass, one_pass, 0)

    @pl.when(e == N_EXPERTS - 1)
    def _():
        z = DEEPNORM_ALPHA * x1_ref[...] + g2_ref[0] * o_ref[...]
        o_ref[...] = _normalize(z) * lng_ref[...] + lnb_ref[...]


def _moe(counts, t, gates, slots, w_gu, b_gu, w_down, b_down, x1, g2, ln_g, ln_b, chunks_per_sample):
    n, d = t.shape
    n_exp, _, two_ff = w_gu.shape
    d_ff = two_ff // 2
    chunk = n // counts.shape[0]
    tok = lambda c, e, cnt: (c, 0)
    per_chunk = pl.BlockSpec((N_EXPERTS, chunk), lambda c, e, cnt: (0, c))
    vec = pl.BlockSpec((1, d), lambda c, e, cnt: (0, 0))
    grid_spec = pltpu.PrefetchScalarGridSpec(
        num_scalar_prefetch=1,
        grid=(counts.shape[0], n_exp),
        in_specs=[pl.BlockSpec((chunk, d), tok),
                  per_chunk, per_chunk,
                  pl.BlockSpec((1, d, two_ff), lambda c, e, cnt: (e, 0, 0)),
                  pl.BlockSpec((1, 1, two_ff), lambda c, e, cnt: (e, 0, 0)),
                  pl.BlockSpec((1, d_ff, d), lambda c, e, cnt: (e, 0, 0)),
                  pl.BlockSpec((1, 1, d), lambda c, e, cnt: (e, 0, 0)),
                  pl.BlockSpec((chunk, d), tok),
                  pl.BlockSpec((1, 1, d), lambda c, e, cnt: (c // chunks_per_sample, 0, 0)),
                  vec, vec],
        out_specs=pl.BlockSpec((chunk, d), tok),
    )
    return pl.pallas_call(
        _moe_kernel,
        grid_spec=grid_spec,
        out_shape=jax.ShapeDtypeStruct((n, d), F32),
        compiler_params=_params("parallel", "arbitrary"),
        name="moe",
    )(counts, t, gates, slots, w_gu, b_gu.reshape(n_exp, 1, two_ff), w_down,
      b_down.reshape(n_exp, 1, d), x1, g2, ln_g.reshape(1, d), ln_b.reshape(1, d))


def _layer(x, c, ctx, c_ctx, ada_w, ada_b, w_in, rpb, sgu_ln_g, sgu_ln_b, sgu_w, sgu_b, w_out,
           ln1_g, ln1_b, ln2_g, ln2_b, router_w, router_b, w_gu, b_gu, w_down, b_down):
    b, s, d = x.shape
    assert s % (GRID_W * ATT_ROWS) == 0 and s // GRID_W >= NA_KH
    chunk = min(MOE_CHUNK, s)
    assert s % chunk == 0 and chunk % SCAN_BLOCK == 0

    cond_rows = jnp.zeros((8, d), F32).at[:b].set(c).at[b].set(c_ctx)
    mod = _ada(cond_rows, ada_w, ada_b)
    sh1, sc1, g1, sh2, sc2, g2 = jnp.split(mod[:b], 6, axis=-1)
    csh1, csc1 = mod[b, :d], mod[b, d:2 * d]

    w_in_bf = w_in.astype(BF16)
    k_c, v_c = _ctx_kv(ctx, csh1, csc1, w_in_bf[:, NA_WIDTH:3 * NA_WIDTH])
    q, k, v, sgu = _proj_in(x, sh1, sc1, w_in_bf, sgu_ln_g, sgu_ln_b, sgu_w, sgu_b)
    att = _natten(q, k, v, k_c, v_c, _bias_table(rpb))
    x1, t, logits_t = _mix_out(att, sgu, x, w_out.astype(BF16), g1, sh2, sc2, ln1_g, ln1_b,
                               router_w, router_b)
    gates, slots, counts = _route(logits_t, chunk)
    out = _moe(counts[:, :, 0], t.reshape(b * s, d), gates, slots,
               w_gu.astype(BF16), b_gu, w_down.astype(BF16), b_down,
               x1.reshape(b * s, d), g2.reshape(b, 1, d), ln2_g, ln2_b, s // chunk)
    return out.reshape(b, s, d)


def kernel(x, c, ctx, c_ctx, ada_w, ada_b, w_in, rpb, sgu_ln_g, sgu_ln_b, sgu_w, sgu_b, w_out,
           ln1_g, ln1_b, ln2_g, ln2_b, router_w, router_b, exp_w_gu, exp_b_gu, exp_w_down,
           exp_b_down):
    assert ada_w.shape[0] == DEPTH
    return _layer(x, c, ctx, c_ctx, ada_w[0], ada_b[0], w_in[0], rpb[0], sgu_ln_g[0], sgu_ln_b[0],
                  sgu_w[0], sgu_b[0], w_out[0], ln1_g[0], ln1_b[0], ln2_g[0], ln2_b[0],
                  router_w[0], router_b[0], exp_w_gu[0], exp_b_gu[0], exp_w_down[0], exp_b_down[0])
```

```python
import functools

import jax
import jax.numpy as jnp
from jax import lax
from jax.experimental import pallas as pl
from jax.experimental.pallas import tpu as pltpu

F32 = jnp.float32
BF16 = jnp.bfloat16

GRID_W = 64
NA_HEADS = 8
NA_HEAD_DIM = 64
NA_WIDTH = NA_HEADS * NA_HEAD_DIM
NA_KH = 8
NA_KW = 16
SGU_GROUPS = 4
SGU_GROUP_DIM = 128
SGU_WIDTH = SGU_GROUPS * SGU_GROUP_DIM
CHUNK = 128
N_EXPERTS = 32
TOP_K = 4
SWIGLU_LIMIT = 7.0
SWIGLU_ALPHA = 1.702
LN_EPS = 1e-5
DEPTH = 1
DEEPNORM_ALPHA = (2.0 * DEPTH) ** 0.25
MASKED = -1e30

VMEM_LIMIT_BYTES = 52 * 1024 * 1024

ROW_TILE = 512
ATT_ROWS = 8
MOE_CHUNK = 1024
MOE_ROWS = 128


def _params(*sem):
    return pltpu.CompilerParams(dimension_semantics=sem, vmem_limit_bytes=VMEM_LIMIT_BYTES)


def _normalize(x):
    mu = jnp.mean(x, axis=-1, keepdims=True)
    xc = x - mu
    var = jnp.mean(xc * xc, axis=-1, keepdims=True)
    return xc * lax.rsqrt(var + LN_EPS)


def _ada_kernel(c_ref, w_ref, b_ref, o_ref):
    s = c_ref[...]
    s = s * jax.nn.sigmoid(s)
    o_ref[...] = jnp.dot(s, w_ref[...], precision=lax.Precision.HIGHEST,
                         preferred_element_type=F32) + b_ref[...]


def _ada(cond_rows, ada_w, ada_b):
    d = cond_rows.shape[1]
    n_out = ada_w.shape[1]
    return pl.pallas_call(
        _ada_kernel,
        grid=(n_out // d,),
        in_specs=[pl.BlockSpec((8, d), lambda j: (0, 0)),
                  pl.BlockSpec((d, d), lambda j: (0, j)),
                  pl.BlockSpec((1, d), lambda j: (0, j))],
        out_specs=pl.BlockSpec((8, d), lambda j: (0, j)),
        out_shape=jax.ShapeDtypeStruct((8, n_out), F32),
        compiler_params=_params("arbitrary"),
        name="ada",
    )(cond_rows, ada_w, ada_b.reshape(1, n_out))


def _ctx_kv_kernel(ctx_ref, sh_ref, sc_ref, w_ref, k_ref, v_ref):
    h = _normalize(ctx_ref[0]) * (1.0 + sc_ref[...]) + sh_ref[...]
    kv = jnp.dot(h.astype(BF16), w_ref[...], preferred_element_type=F32)
    k_ref[0] = kv[:, :NA_WIDTH].astype(BF16)
    v_ref[0] = kv[:, NA_WIDTH:].astype(BF16)


def _ctx_kv(ctx, csh1, csc1, w_kv):
    b, l, d = ctx.shape
    out = jax.ShapeDtypeStruct((b, l, NA_WIDTH), BF16)
    return pl.pallas_call(
        _ctx_kv_kernel,
        grid=(b,),
        in_specs=[pl.BlockSpec((1, l, d), lambda i: (i, 0, 0)),
                  pl.BlockSpec((1, d), lambda i: (0, 0)),
                  pl.BlockSpec((1, d), lambda i: (0, 0)),
                  pl.BlockSpec((d, 2 * NA_WIDTH), lambda i: (0, 0))],
        out_specs=[pl.BlockSpec((1, l, NA_WIDTH), lambda i: (i, 0, 0)),
                   pl.BlockSpec((1, l, NA_WIDTH), lambda i: (i, 0, 0))],
        out_shape=[out, out],
        compiler_params=_params("arbitrary"),
        name="ctx_kv",
    )(ctx, csh1.reshape(1, d), csc1.reshape(1, d), w_kv)


def _proj_in_kernel(x_ref, sh_ref, sc_ref, w_ref, lng_ref, lnb_ref, ws_ref, bs_ref,
                    q_ref, k_ref, v_ref, s_ref):
    h = (_normalize(x_ref[0]) * (1.0 + sc_ref[0]) + sh_ref[0]).astype(BF16)

    def proj(lo, width):
        return jnp.dot(h, w_ref[:, lo:lo + width], preferred_element_type=F32)

    q_ref[0] = (proj(0, NA_WIDTH) * (NA_HEAD_DIM ** -0.5)).astype(BF16)
    k_ref[0] = proj(NA_WIDTH, NA_WIDTH).astype(BF16)
    v_ref[0] = proj(2 * NA_WIDTH, NA_WIDTH).astype(BF16)
    u = jax.nn.gelu(proj(3 * NA_WIDTH, SGU_WIDTH))
    g = jax.nn.gelu(proj(3 * NA_WIDTH + SGU_WIDTH, SGU_WIDTH))
    gn = (_normalize(g) * lng_ref[...] + lnb_ref[...]).astype(BF16)
    rows = h.shape[0]
    for n in range(rows // CHUNK):
        r0 = n * CHUNK
        for grp in range(SGU_GROUPS):
            c0 = grp * SGU_GROUP_DIM
            mixed = jnp.dot(ws_ref[grp], gn[r0:r0 + CHUNK, c0:c0 + SGU_GROUP_DIM],
                            preferred_element_type=F32) + bs_ref[grp]
            s_ref[0, r0:r0 + CHUNK, c0:c0 + SGU_GROUP_DIM] = (
                u[r0:r0 + CHUNK, c0:c0 + SGU_GROUP_DIM] * mixed).astype(BF16)


def _proj_in(x, sh1, sc1, w_in, sgu_ln_g, sgu_ln_b, sgu_w, sgu_b):
    b, s, d = x.shape
    d_in = w_in.shape[1]
    tm = min(ROW_TILE, s)
    out = jax.ShapeDtypeStruct((b, s, NA_WIDTH), BF16)
    row_spec = pl.BlockSpec((1, tm, NA_WIDTH), lambda i, j: (i, j, 0))
    mod_spec = pl.BlockSpec((1, 1, d), lambda i, j: (i, 0, 0))
    bs = jnp.broadcast_to(sgu_b[:, :, None], (SGU_GROUPS, CHUNK, SGU_GROUP_DIM))
    return pl.pallas_call(
        _proj_in_kernel,
        grid=(b, s // tm),
        in_specs=[pl.BlockSpec((1, tm, d), lambda i, j: (i, j, 0)),
                  mod_spec, mod_spec,
                  pl.BlockSpec((d, d_in), lambda i, j: (0, 0)),
                  pl.BlockSpec((1, SGU_WIDTH), lambda i, j: (0, 0)),
                  pl.BlockSpec((1, SGU_WIDTH), lambda i, j: (0, 0)),
                  pl.BlockSpec((SGU_GROUPS, CHUNK, CHUNK), lambda i, j: (0, 0, 0)),
                  pl.BlockSpec((SGU_GROUPS, CHUNK, SGU_GROUP_DIM), lambda i, j: (0, 0, 0))],
        out_specs=[row_spec, row_spec, row_spec, row_spec],
        out_shape=[out, out, out, out],
        compiler_params=_params("parallel", "parallel"),
        name="proj_in",
    )(x, sh1.reshape(b, 1, d), sc1.reshape(b, 1, d), w_in,
      sgu_ln_g.reshape(1, SGU_WIDTH), sgu_ln_b.reshape(1, SGU_WIDTH), sgu_w.astype(BF16), bs)


def _bias_table(rpb):
    heads, n_dr, n_dc = rpb.shape
    c = jnp.arange(GRID_W)
    cs = jnp.clip(c - NA_KW // 2, 0, GRID_W - NA_KW)
    kc = jnp.arange(GRID_W)
    valid = (kc[None, :] >= cs[:, None]) & (kc[None, :] < cs[:, None] + NA_KW)
    lead = GRID_W - NA_KW
    padded = jnp.pad(rpb.astype(F32), ((0, 0), (0, 0), (lead, 2 * GRID_W - lead - n_dc)))
    flat = jnp.tile(padded, (1, 1, GRID_W))[:, :, :GRID_W * (2 * GRID_W - 1)]
    toe = flat.reshape(heads, n_dr, GRID_W, 2 * GRID_W - 1)[..., GRID_W - 1:]
    toe = jnp.where(valid, toe, MASKED)
    tab = jnp.stack([toe[:, NA_KH - 1 - o:2 * NA_KH - 1 - o] for o in range(NA_KH)])
    tab = tab.transpose(0, 1, 3, 2, 4)
    return tab.reshape(NA_KH, NA_HEADS * GRID_W, NA_KH * GRID_W)


def _natten_kernel(q_ref, k_ref, v_ref, kc_ref, vc_ref, bias_ref, o_ref, *, grid_rows):
    stacked = NA_HEADS * GRID_W
    row_head = lax.broadcasted_iota(jnp.int32, (stacked, NA_WIDTH), 0) // GRID_W
    lane_head = lax.broadcasted_iota(jnp.int32, (stacked, NA_WIDTH), 1) // NA_HEAD_DIM
    own_head = row_head == lane_head
    kc = kc_ref[0]
    vc = vc_ref[0]
    nt = (((1,), (1,)), ((), ()))

    def one_row(i, carry):
        r = pl.program_id(1) * ATT_ROWS + i
        rs = jnp.clip(r - NA_KH // 2, 0, grid_rows - NA_KH)
        k0 = pl.multiple_of(rs * GRID_W, GRID_W)
        q0 = pl.multiple_of(i * GRID_W, GRID_W)
        q = q_ref[0, pl.ds(q0, GRID_W), :]
        qs = jnp.where(own_head, jnp.concatenate([q] * NA_HEADS, axis=0), jnp.zeros((), BF16))
        kr = k_ref[0, pl.ds(k0, NA_KH * GRID_W), :]
        vr = v_ref[0, pl.ds(k0, NA_KH * GRID_W), :]
        s_nb = lax.dot_general(qs, kr, nt, preferred_element_type=F32) + bias_ref[r - rs]
        s_cx = lax.dot_general(qs, kc, nt, preferred_element_type=F32)
        m = jnp.maximum(jnp.max(s_nb, axis=-1, keepdims=True), jnp.max(s_cx, axis=-1, keepdims=True))
        p_nb = jnp.exp(s_nb - m)
        p_cx = jnp.exp(s_cx - m)
        denom = jnp.sum(p_nb, axis=-1, keepdims=True) + jnp.sum(p_cx, axis=-1, keepdims=True)
        o = (jnp.dot(p_nb.astype(BF16), vr, preferred_element_type=F32)
             + jnp.dot(p_cx.astype(BF16), vc, preferred_element_type=F32)) / denom
        o = jnp.where(own_head, o, 0.0)
        acc = o[0:GRID_W]
        for h in range(1, NA_HEADS):
            acc = acc + o[h * GRID_W:(h + 1) * GRID_W]
        o_ref[0, pl.ds(q0, GRID_W), :] = acc.astype(BF16)
        return carry

    lax.fori_loop(0, ATT_ROWS, one_row, 0, unroll=True)


def _natten(q, k, v, k_c, v_c, bias):
    b, s, w = q.shape
    l = k_c.shape[1]
    grid_rows = s // GRID_W
    tq = ATT_ROWS * GRID_W
    full = pl.BlockSpec((1, s, w), lambda i, j: (i, 0, 0), pipeline_mode=pl.Buffered(1))
    ctx = pl.BlockSpec((1, l, w), lambda i, j: (i, 0, 0))
    return pl.pallas_call(
        functools.partial(_natten_kernel, grid_rows=grid_rows),
        grid=(b, grid_rows // ATT_ROWS),
        in_specs=[pl.BlockSpec((1, tq, w), lambda i, j: (i, j, 0)),
                  full, full, ctx, ctx,
                  pl.BlockSpec(bias.shape, lambda i, j: (0, 0, 0), pipeline_mode=pl.Buffered(1))],
        out_specs=pl.BlockSpec((1, tq, w), lambda i, j: (i, j, 0)),
        out_shape=jax.ShapeDtypeStruct((b, s, w), BF16),
        compiler_params=_params("parallel", "arbitrary"),
        name="natten",
    )(q, k, v, k_c, v_c, bias)


def _mix_out_kernel(att_ref, sgu_ref, x_ref, wo_ref, g1_ref, sh_ref, sc_ref, lng_ref, lnb_ref,
                    wr_ref, br_ref, x1_ref, t_ref, lg_ref):
    mix = (jnp.dot(att_ref[0], wo_ref[:NA_WIDTH, :], preferred_element_type=F32)
           + jnp.dot(sgu_ref[0], wo_ref[NA_WIDTH:, :], preferred_element_type=F32))
    x1 = _normalize(DEEPNORM_ALPHA * x_ref[0] + g1_ref[0] * mix) * lng_ref[...] + lnb_ref[...]
    x1_ref[0] = x1
    t = _normalize(x1) * (1.0 + sc_ref[0]) + sh_ref[0]
    t_ref[0] = t.astype(BF16)
    lg_ref[...] = lax.dot_general(wr_ref[...], t, (((1,), (1,)), ((), ())),
                                  precision=lax.Precision.HIGHEST,
                                  preferred_element_type=F32) + br_ref[...]


def _mix_out(att, sgu, x, w_out, g1, sh2, sc2, ln_g, ln_b, router_w, router_b):
    b, s, d = x.shape
    tm = min(ROW_TILE, s)
    nj = s // tm
    row = lambda width: pl.BlockSpec((1, tm, width), lambda i, j: (i, j, 0))
    mod_spec = pl.BlockSpec((1, 1, d), lambda i, j: (i, 0, 0))
    vec_spec = pl.BlockSpec((1, d), lambda i, j: (0, 0))
    return pl.pallas_call(
        _mix_out_kernel,
        grid=(b, nj),
        in_specs=[row(NA_WIDTH), row(SGU_WIDTH), row(d),
                  pl.BlockSpec((d, d), lambda i, j: (0, 0)),
                  mod_spec, mod_spec, mod_spec, vec_spec, vec_spec,
                  pl.BlockSpec((N_EXPERTS, d), lambda i, j: (0, 0)),
                  pl.BlockSpec((N_EXPERTS, 1), lambda i, j: (0, 0))],
        out_specs=[row(d), row(d),
                   pl.BlockSpec((N_EXPERTS, tm), lambda i, j: (0, i * nj + j))],
        out_shape=[jax.ShapeDtypeStruct((b, s, d), F32),
                   jax.ShapeDtypeStruct((b, s, d), BF16),
                   jax.ShapeDtypeStruct((N_EXPERTS, b * s), F32)],
        compiler_params=_params("parallel", "parallel"),
        name="mix_out",
    )(att, sgu, x, w_out, g1.reshape(b, 1, d), sh2.reshape(b, 1, d), sc2.reshape(b, 1, d),
      ln_g.reshape(1, d), ln_b.reshape(1, d), router_w.T, router_b.reshape(N_EXPERTS, 1))


SCAN_BLOCK = 256


def _route_kernel(lg_ref, gate_ref, slot_ref, cnt_ref):
    logits = lg_ref[...]
    n_tok = logits.shape[1]
    expert = lax.broadcasted_iota(jnp.int32, logits.shape, 0)
    work = logits
    chosen = jnp.zeros(logits.shape, jnp.bool_)
    top = None
    for kk in range(TOP_K):
        m = jnp.max(work, axis=0, keepdims=True)
        first = jnp.min(jnp.where(work == m, expert, N_EXPERTS), axis=0, keepdims=True)
        pick = expert == first
        chosen = jnp.logical_or(chosen, pick)
        work = jnp.where(pick, -jnp.inf, work)
        if kk == 0:
            top = m
    e = jnp.where(chosen, jnp.exp(logits - top), 0.0)
    gate_ref[...] = e / jnp.sum(e, axis=0, keepdims=True)
    tri = (lax.broadcasted_iota(jnp.int32, (SCAN_BLOCK, SCAN_BLOCK), 0)
           <= lax.broadcasted_iota(jnp.int32, (SCAN_BLOCK, SCAN_BLOCK), 1)).astype(BF16)
    sel = jnp.where(chosen, 1.0, 0.0).astype(BF16)
    carry = jnp.zeros((N_EXPERTS, 1), F32)
    for blk in range(n_tok // SCAN_BLOCK):
        lo = blk * SCAN_BLOCK
        run = jnp.dot(sel[:, lo:lo + SCAN_BLOCK], tri, preferred_element_type=F32) + carry
        slot_ref[:, lo:lo + SCAN_BLOCK] = jnp.where(
            chosen[:, lo:lo + SCAN_BLOCK], run.astype(jnp.int32) - 1, -1)
        carry = run[:, SCAN_BLOCK - 1:SCAN_BLOCK]
    cnt_ref[0] = jnp.broadcast_to(carry, (N_EXPERTS, 128)).astype(jnp.int32)


def _route(logits_t, chunk):
    n = logits_t.shape[1]
    nc = n // chunk
    blk = pl.BlockSpec((N_EXPERTS, chunk), lambda c: (0, c))
    return pl.pallas_call(
        _route_kernel,
        grid=(nc,),
        in_specs=[blk],
        out_specs=[blk, blk, pl.BlockSpec((1, N_EXPERTS, 128), lambda c: (c, 0, 0))],
        out_shape=[jax.ShapeDtypeStruct((N_EXPERTS, n), F32),
                   jax.ShapeDtypeStruct((N_EXPERTS, n), jnp.int32),
                   jax.ShapeDtypeStruct((nc, N_EXPERTS, 128), jnp.int32)],
        compiler_params=_params("parallel"),
        name="route",
    )(logits_t)


def _moe_kernel(cnt_ref, t_ref, gate_ref, slot_ref, wgu_ref, bgu_ref, wd_ref, bd_ref,
                x1_ref, g2_ref, lng_ref, lnb_ref, o_ref):
    c = pl.program_id(0)
    e = pl.program_id(1)
    d_ff = wd_ref.shape[1]
    chunk = t_ref.shape[0]

    @pl.when(e == 0)
    def _():
        o_ref[...] = jnp.zeros_like(o_ref)

    slot_row = slot_ref[pl.ds(e, 1), :]
    gate_row = gate_ref[pl.ds(e, 1), :]
    n_pass = (cnt_ref[c, e] + MOE_ROWS - 1) // MOE_ROWS

    def one_pass(i, carry):
        rows = i * MOE_ROWS + lax.broadcasted_iota(jnp.int32, (MOE_ROWS, chunk), 0)
        hit = slot_row == rows
        gather = jnp.where(hit, 1.0, 0.0).astype(BF16)
        scatter = jnp.where(hit, gate_row, 0.0).astype(BF16)
        xg = jnp.dot(gather, t_ref[...], preferred_element_type=F32).astype(BF16)
        gu = jnp.dot(xg, wgu_ref[0], preferred_element_type=F32) + bgu_ref[0]
        gate = jnp.minimum(gu[:, :d_ff], SWIGLU_LIMIT)
        lin = jnp.clip(gu[:, d_ff:], -SWIGLU_LIMIT, SWIGLU_LIMIT)
        act = ((lin + 1.0) * (gate * jax.nn.sigmoid(SWIGLU_ALPHA * gate))).astype(BF16)
        y = jnp.dot(act, wd_ref[0], preferred_element_type=F32) + bd_ref[0]
        o_ref[...] += lax.dot_general(scatter, y.astype(BF16), (((0,), (0,)), ((), ())),
                                      preferred_element_type=F32)
        return carry

    lax.fori_loop(0, n_pass, one_pass, 0)

    @pl.when(e == N_EXPERTS - 1)
    def _():
        z = DEEPNORM_ALPHA * x1_ref[...] + g2_ref[0] * o_ref[...]
        o_ref[...] = _normalize(z) * lng_ref[...] + lnb_ref[...]


def _moe(counts, t, gates, slots, w_gu, b_gu, w_down, b_down, x1, g2, ln_g, ln_b, chunks_per_sample):
    n, d = t.shape
    n_exp, _, two_ff = w_gu.shape
    d_ff = two_ff // 2
    chunk = n // counts.shape[0]
    tok = lambda c, e, cnt: (c, 0)
    per_chunk = pl.BlockSpec((N_EXPERTS, chunk), lambda c, e, cnt: (0, c))
    vec = pl.BlockSpec((1, d), lambda c, e, cnt: (0, 0))
    grid_spec = pltpu.PrefetchScalarGridSpec(
        num_scalar_prefetch=1,
        grid=(counts.shape[0], n_exp),
        in_specs=[pl.BlockSpec((chunk, d), tok),
                  per_chunk, per_chunk,
                  pl.BlockSpec((1, d, two_ff), lambda c, e, cnt: (e, 0, 0)),
                  pl.BlockSpec((1, 1, two_ff), lambda c, e, cnt: (e, 0, 0)),
                  pl.BlockSpec((1, d_ff, d), lambda c, e, cnt: (e, 0, 0)),
                  pl.BlockSpec((1, 1, d), lambda c, e, cnt: (e, 0, 0)),
                  pl.BlockSpec((chunk, d), tok),
                  pl.BlockSpec((1, 1, d), lambda c, e, cnt: (c // chunks_per_sample, 0, 0)),
                  vec, vec],
        out_specs=pl.BlockSpec((chunk, d), tok),
    )
    return pl.pallas_call(
        _moe_kernel,
        grid_spec=grid_spec,
        out_shape=jax.ShapeDtypeStruct((n, d), F32),
        compiler_params=_params("parallel", "arbitrary"),
        name="moe",
    )(counts, t, gates, slots, w_gu, b_gu.reshape(n_exp, 1, two_ff), w_down,
      b_down.reshape(n_exp, 1, d), x1, g2, ln_g.reshape(1, d), ln_b.reshape(1, d))


def _layer(x, c, ctx, c_ctx, ada_w, ada_b, w_in, rpb, sgu_ln_g, sgu_ln_b, sgu_w, sgu_b, w_out,
           ln1_g, ln1_b, ln2_g, ln2_b, router_w, router_b, w_gu, b_gu, w_down, b_down):
    b, s, d = x.shape
    assert s % (GRID_W * ATT_ROWS) == 0 and s // GRID_W >= NA_KH
    chunk = min(MOE_CHUNK, s)
    assert s % chunk == 0 and chunk % SCAN_BLOCK == 0

    cond_rows = jnp.zeros((8, d), F32).at[:b].set(c).at[b].set(c_ctx)
    mod = _ada(cond_rows, ada_w, ada_b)
    sh1, sc1, g1, sh2, sc2, g2 = jnp.split(mod[:b], 6, axis=-1)
    csh1, csc1 = mod[b, :d], mod[b, d:2 * d]

    w_in_bf = w_in.astype(BF16)
    k_c, v_c = _ctx_kv(ctx, csh1, csc1, w_in_bf[:, NA_WIDTH:3 * NA_WIDTH])
    q, k, v, sgu = _proj_in(x, sh1, sc1, w_in_bf, sgu_ln_g, sgu_ln_b, sgu_w, sgu_b)
    att = _natten(q, k, v, k_c, v_c, _bias_table(rpb))
    x1, t, logits_t = _mix_out(att, sgu, x, w_out.astype(BF16), g1, sh2, sc2, ln1_g, ln1_b,
                               router_w, router_b)
    gates, slots, counts = _route(logits_t, chunk)
    out = _moe(counts[:, :, 0], t.reshape(b * s, d), gates, slots,
               w_gu.astype(BF16), b_gu, w_down.astype(BF16), b_down,
               x1.reshape(b * s, d), g2.reshape(b, 1, d), ln2_g, ln2_b, s // chunk)
    return out.reshape(b, s, d)


def kernel(x, c, ctx, c_ctx, ada_w, ada_b, w_in, rpb, sgu_ln_g, sgu_ln_b, sgu_w, sgu_b, w_out,
           ln1_g, ln1_b, ln2_g, ln2_b, router_w, router_b, exp_w_gu, exp_b_gu, exp_w_down,
           exp_b_down):
    assert ada_w.shape[0] == DEPTH
    return _layer(x, c, ctx, c_ctx, ada_w[0], ada_b[0], w_in[0], rpb[0], sgu_ln_g[0], sgu_ln_b[0],
                  sgu_w[0], sgu_b[0], w_out[0], ln1_g[0], ln1_b[0], ln2_g[0], ln2_b[0],
                  router_w[0], router_b[0], exp_w_gu[0], exp_b_gu[0], exp_w_down[0], exp_b_down[0])
```

```python
import functools

import jax
import jax.numpy as jnp
from jax import lax
from jax.experimental import pallas as pl
from jax.experimental.pallas import tpu as pltpu

F32 = jnp.float32
BF16 = jnp.bfloat16

GRID_W = 64
NA_HEADS = 8
NA_HEAD_DIM = 64
NA_WIDTH = NA_HEADS * NA_HEAD_DIM
NA_KH = 8
NA_KW = 16
SGU_GROUPS = 4
SGU_GROUP_DIM = 128
SGU_WIDTH = SGU_GROUPS * SGU_GROUP_DIM
CHUNK = 128
N_EXPERTS = 32
TOP_K = 4
SWIGLU_LIMIT = 7.0
SWIGLU_ALPHA = 1.702
LN_EPS = 1e-5
DEPTH = 1
DEEPNORM_ALPHA = (2.0 * DEPTH) ** 0.25
MASKED = -1e30

SUBLANES = 8
LANES = 128
VMEM_LIMIT_BYTES = 52 * 1024 * 1024

ROW_TILE = 512
ATT_ROWS = 8
ROUTE_CHUNK = 1024
SCAN_BLOCK = 256
EXPERT_ROWS = 256
DISPATCH_TOKENS = 1024
COMBINE_TOKENS = 256


def _params(*sem):
    return pltpu.CompilerParams(dimension_semantics=sem, vmem_limit_bytes=VMEM_LIMIT_BYTES)


def _normalize(x):
    mu = jnp.mean(x, axis=-1, keepdims=True)
    xc = x - mu
    var = jnp.mean(xc * xc, axis=-1, keepdims=True)
    return xc * lax.rsqrt(var + LN_EPS)


def _ada_kernel(c_ref, w_ref, b_ref, o_ref):
    s = c_ref[...]
    s = s * jax.nn.sigmoid(s)
    o_ref[...] = jnp.dot(s, w_ref[...], precision=lax.Precision.HIGHEST,
                         preferred_element_type=F32) + b_ref[...]


def _ada(cond_rows, ada_w, ada_b):
    d = cond_rows.shape[1]
    n_out = ada_w.shape[1]
    return pl.pallas_call(
        _ada_kernel,
        grid=(n_out // d,),
        in_specs=[pl.BlockSpec((8, d), lambda j: (0, 0)),
                  pl.BlockSpec((d, d), lambda j: (0, j)),
                  pl.BlockSpec((1, d), lambda j: (0, j))],
        out_specs=pl.BlockSpec((8, d), lambda j: (0, j)),
        out_shape=jax.ShapeDtypeStruct((8, n_out), F32),
        compiler_params=_params("arbitrary"),
        name="ada",
    )(cond_rows, ada_w, ada_b.reshape(1, n_out))


def _ctx_kv_kernel(ctx_ref, sh_ref, sc_ref, w_ref, k_ref, v_ref):
    h = _normalize(ctx_ref[0]) * (1.0 + sc_ref[...]) + sh_ref[...]
    kv = jnp.dot(h.astype(BF16), w_ref[...], preferred_element_type=F32)
    k_ref[0] = kv[:, :NA_WIDTH].astype(BF16)
    v_ref[0] = kv[:, NA_WIDTH:].astype(BF16)


def _ctx_kv(ctx, csh1, csc1, w_kv):
    b, l, d = ctx.shape
    out = jax.ShapeDtypeStruct((b, l, NA_WIDTH), BF16)
    return pl.pallas_call(
        _ctx_kv_kernel,
        grid=(b,),
        in_specs=[pl.BlockSpec((1, l, d), lambda i: (i, 0, 0)),
                  pl.BlockSpec((1, d), lambda i: (0, 0)),
                  pl.BlockSpec((1, d), lambda i: (0, 0)),
                  pl.BlockSpec((d, 2 * NA_WIDTH), lambda i: (0, 0))],
        out_specs=[pl.BlockSpec((1, l, NA_WIDTH), lambda i: (i, 0, 0)),
                   pl.BlockSpec((1, l, NA_WIDTH), lambda i: (i, 0, 0))],
        out_shape=[out, out],
        compiler_params=_params("arbitrary"),
        name="ctx_kv",
    )(ctx, csh1.reshape(1, d), csc1.reshape(1, d), w_kv)


def _proj_in_kernel(x_ref, sh_ref, sc_ref, w_ref, lng_ref, lnb_ref, ws_ref, bs_ref,
                    q_ref, k_ref, v_ref, s_ref):
    h = (_normalize(x_ref[0]) * (1.0 + sc_ref[0]) + sh_ref[0]).astype(BF16)

    def proj(lo, width):
        return jnp.dot(h, w_ref[:, lo:lo + width], preferred_element_type=F32)

    q_ref[0] = (proj(0, NA_WIDTH) * (NA_HEAD_DIM ** -0.5)).astype(BF16)
    k_ref[0] = proj(NA_WIDTH, NA_WIDTH).astype(BF16)
    v_ref[0] = proj(2 * NA_WIDTH, NA_WIDTH).astype(BF16)
    u = jax.nn.gelu(proj(3 * NA_WIDTH, SGU_WIDTH))
    g = jax.nn.gelu(proj(3 * NA_WIDTH + SGU_WIDTH, SGU_WIDTH))
    gn = (_normalize(g) * lng_ref[...] + lnb_ref[...]).astype(BF16)
    rows = h.shape[0]
    for n in range(rows // CHUNK):
        r0 = n * CHUNK
        for grp in range(SGU_GROUPS):
            c0 = grp * SGU_GROUP_DIM
            mixed = jnp.dot(ws_ref[grp], gn[r0:r0 + CHUNK, c0:c0 + SGU_GROUP_DIM],
                            preferred_element_type=F32) + bs_ref[grp]
            s_ref[0, r0:r0 + CHUNK, c0:c0 + SGU_GROUP_DIM] = (
                u[r0:r0 + CHUNK, c0:c0 + SGU_GROUP_DIM] * mixed).astype(BF16)


def _proj_in(x, sh1, sc1, w_in, sgu_ln_g, sgu_ln_b, sgu_w, sgu_b):
    b, s, d = x.shape
    d_in = w_in.shape[1]
    tm = min(ROW_TILE, s)
    out = jax.ShapeDtypeStruct((b, s, NA_WIDTH), BF16)
    row_spec = pl.BlockSpec((1, tm, NA_WIDTH), lambda i, j: (i, j, 0))
    mod_spec = pl.BlockSpec((1, 1, d), lambda i, j: (i, 0, 0))
    bs = jnp.broadcast_to(sgu_b[:, :, None], (SGU_GROUPS, CHUNK, SGU_GROUP_DIM))
    return pl.pallas_call(
        _proj_in_kernel,
        grid=(b, s // tm),
        in_specs=[pl.BlockSpec((1, tm, d), lambda i, j: (i, j, 0)),
                  mod_spec, mod_spec,
                  pl.BlockSpec((d, d_in), lambda i, j: (0, 0)),
                  pl.BlockSpec((1, SGU_WIDTH), lambda i, j: (0, 0)),
                  pl.BlockSpec((1, SGU_WIDTH), lambda i, j: (0, 0)),
                  pl.BlockSpec((SGU_GROUPS, CHUNK, CHUNK), lambda i, j: (0, 0, 0)),
                  pl.BlockSpec((SGU_GROUPS, CHUNK, SGU_GROUP_DIM), lambda i, j: (0, 0, 0))],
        out_specs=[row_spec, row_spec, row_spec, row_spec],
        out_shape=[out, out, out, out],
        compiler_params=_params("parallel", "parallel"),
        name="proj_in",
    )(x, sh1.reshape(b, 1, d), sc1.reshape(b, 1, d), w_in,
      sgu_ln_g.reshape(1, SGU_WIDTH), sgu_ln_b.reshape(1, SGU_WIDTH), sgu_w.astype(BF16), bs)


def _bias_table(rpb):
    heads, n_dr, n_dc = rpb.shape
    c = jnp.arange(GRID_W)
    cs = jnp.clip(c - NA_KW // 2, 0, GRID_W - NA_KW)
    kc = jnp.arange(GRID_W)
    valid = (kc[None, :] >= cs[:, None]) & (kc[None, :] < cs[:, None] + NA_KW)
    lead = GRID_W - NA_KW
    padded = jnp.pad(rpb.astype(F32), ((0, 0), (0, 0), (lead, 2 * GRID_W - lead - n_dc)))
    flat = jnp.tile(padded, (1, 1, GRID_W))[:, :, :GRID_W * (2 * GRID_W - 1)]
    toe = flat.reshape(heads, n_dr, GRID_W, 2 * GRID_W - 1)[..., GRID_W - 1:]
    toe = jnp.where(valid, toe, MASKED)
    tab = jnp.stack([toe[:, NA_KH - 1 - o:2 * NA_KH - 1 - o] for o in range(NA_KH)])
    tab = tab.transpose(0, 1, 3, 2, 4)
    return tab.reshape(NA_KH, NA_HEADS * GRID_W, NA_KH * GRID_W)


def _natten_kernel(q_ref, k_ref, v_ref, kc_ref, vc_ref, bias_ref, o_ref, *, grid_rows):
    stacked = NA_HEADS * GRID_W
    row_head = lax.broadcasted_iota(jnp.int32, (stacked, NA_WIDTH), 0) // GRID_W
    lane_head = lax.broadcasted_iota(jnp.int32, (stacked, NA_WIDTH), 1) // NA_HEAD_DIM
    own_head = row_head == lane_head
    kc = kc_ref[0]
    vc = vc_ref[0]
    nt = (((1,), (1,)), ((), ()))

    def one_row(i, carry):
        r = pl.program_id(1) * ATT_ROWS + i
        rs = jnp.clip(r - NA_KH // 2, 0, grid_rows - NA_KH)
        k0 = pl.multiple_of(rs * GRID_W, GRID_W)
        q0 = pl.multiple_of(i * GRID_W, GRID_W)
        q = q_ref[0, pl.ds(q0, GRID_W), :]
        qs = jnp.where(own_head, jnp.concatenate([q] * NA_HEADS, axis=0), jnp.zeros((), BF16))
        kr = k_ref[0, pl.ds(k0, NA_KH * GRID_W), :]
        vr = v_ref[0, pl.ds(k0, NA_KH * GRID_W), :]
        s_nb = lax.dot_general(qs, kr, nt, preferred_element_type=F32) + bias_ref[r - rs]
        s_cx = lax.dot_general(qs, kc, nt, preferred_element_type=F32)
        m = jnp.maximum(jnp.max(s_nb, axis=-1, keepdims=True), jnp.max(s_cx, axis=-1, keepdims=True))
        p_nb = jnp.exp(s_nb - m)
        p_cx = jnp.exp(s_cx - m)
        denom = jnp.sum(p_nb, axis=-1, keepdims=True) + jnp.sum(p_cx, axis=-1, keepdims=True)
        o = (jnp.dot(p_nb.astype(BF16), vr, preferred_element_type=F32)
             + jnp.dot(p_cx.astype(BF16), vc, preferred_element_type=F32)) / denom
        o = jnp.where(own_head, o, 0.0)
        acc = o[0:GRID_W]
        for h in range(1, NA_HEADS):
            acc = acc + o[h * GRID_W:(h + 1) * GRID_W]
        o_ref[0, pl.ds(q0, GRID_W), :] = acc.astype(BF16)
        return carry

    lax.fori_loop(0, ATT_ROWS, one_row, 0, unroll=True)


def _natten(q, k, v, k_c, v_c, bias):
    b, s, w = q.shape
    l = k_c.shape[1]
    grid_rows = s // GRID_W
    tq = ATT_ROWS * GRID_W
    full = pl.BlockSpec((1, s, w), lambda i, j: (i, 0, 0), pipeline_mode=pl.Buffered(1))
    ctx = pl.BlockSpec((1, l, w), lambda i, j: (i, 0, 0))
    return pl.pallas_call(
        functools.partial(_natten_kernel, grid_rows=grid_rows),
        grid=(b, grid_rows // ATT_ROWS),
        in_specs=[pl.BlockSpec((1, tq, w), lambda i, j: (i, j, 0)),
                  full, full, ctx, ctx,
                  pl.BlockSpec(bias.shape, lambda i, j: (0, 0, 0), pipeline_mode=pl.Buffered(1))],
        out_specs=pl.BlockSpec((1, tq, w), lambda i, j: (i, j, 0)),
        out_shape=jax.ShapeDtypeStruct((b, s, w), BF16),
        compiler_params=_params("parallel", "arbitrary"),
        name="natten",
    )(q, k, v, k_c, v_c, bias)


def _mix_out_kernel(att_ref, sgu_ref, x_ref, wo_ref, g1_ref, sh_ref, sc_ref, lng_ref, lnb_ref,
                    wr_ref, br_ref, x1_ref, t_ref, lg_ref):
    mix = (jnp.dot(att_ref[0], wo_ref[:NA_WIDTH, :], preferred_element_type=F32)
           + jnp.dot(sgu_ref[0], wo_ref[NA_WIDTH:, :], preferred_element_type=F32))
    x1 = _normalize(DEEPNORM_ALPHA * x_ref[0] + g1_ref[0] * mix) * lng_ref[...] + lnb_ref[...]
    x1_ref[0] = x1
    t = _normalize(x1) * (1.0 + sc_ref[0]) + sh_ref[0]
    rows = t.shape[0]
    for s in range(t.shape[1] // LANES):
        t_ref[pl.ds(s, rows, stride=SUBLANES), :] = t[:, s * LANES:(s + 1) * LANES]
    lg_ref[...] = lax.dot_general(wr_ref[...], t, (((1,), (1,)), ((), ())),
                                  precision=lax.Precision.HIGHEST,
                                  preferred_element_type=F32) + br_ref[...]


def _mix_out(att, sgu, x, w_out, g1, sh2, sc2, ln_g, ln_b, router_w, router_b):
    b, s, d = x.shape
    assert d == SUBLANES * LANES
    tm = min(ROW_TILE, s)
    nj = s // tm
    row = lambda width: pl.BlockSpec((1, tm, width), lambda i, j: (i, j, 0))
    mod_spec = pl.BlockSpec((1, 1, d), lambda i, j: (i, 0, 0))
    vec_spec = pl.BlockSpec((1, d), lambda i, j: (0, 0))
    return pl.pallas_call(
        _mix_out_kernel,
        grid=(b, nj),
        in_specs=[row(NA_WIDTH), row(SGU_WIDTH), row(d),
                  pl.BlockSpec((d, d), lambda i, j: (0, 0)),
                  mod_spec, mod_spec, mod_spec, vec_spec, vec_spec,
                  pl.BlockSpec((N_EXPERTS, d), lambda i, j: (0, 0)),
                  pl.BlockSpec((N_EXPERTS, 1), lambda i, j: (0, 0))],
        out_specs=[row(d),
                   pl.BlockSpec((tm * SUBLANES, LANES), lambda i, j: (i * nj + j, 0)),
                   pl.BlockSpec((N_EXPERTS, tm), lambda i, j: (0, i * nj + j))],
        out_shape=[jax.ShapeDtypeStruct((b, s, d), F32),
                   jax.ShapeDtypeStruct((b * s * SUBLANES, LANES), F32),
                   jax.ShapeDtypeStruct((N_EXPERTS, b * s), F32)],
        compiler_params=_params("parallel", "parallel"),
        name="mix_out",
    )(att, sgu, x, w_out, g1.reshape(b, 1, d), sh2.reshape(b, 1, d), sc2.reshape(b, 1, d),
      ln_g.reshape(1, d), ln_b.reshape(1, d), router_w.T, router_b.reshape(N_EXPERTS, 1))


def _route_kernel(lg_ref, e_ref, r_ref, g_ref, cnt_ref, run_ref):
    @pl.when(pl.program_id(0) == 0)
    def _():
        run_ref[...] = jnp.zeros_like(run_ref)

    logits = lg_ref[...]
    n_tok = logits.shape[1]
    expert = lax.broadcasted_iota(jnp.int32, logits.shape, 0)
    work = logits
    picks, tops = [], []
    for kk in range(TOP_K):
        m = jnp.max(work, axis=0, keepdims=True)
        first = jnp.min(jnp.where(work == m, expert, N_EXPERTS), axis=0, keepdims=True)
        pick = expert == first
        work = jnp.where(pick, -jnp.inf, work)
        picks.append(pick)
        tops.append(m)
        e_ref[kk:kk + 1, :] = first
    weights = [jnp.exp(m - tops[0]) for m in tops]
    denom = weights[0]
    for w in weights[1:]:
        denom = denom + w
    for kk in range(TOP_K):
        g_ref[kk:kk + 1, :] = weights[kk] / denom

    chosen = picks[0]
    for pick in picks[1:]:
        chosen = jnp.logical_or(chosen, pick)
    tri = (lax.broadcasted_iota(jnp.int32, (SCAN_BLOCK, SCAN_BLOCK), 0)
           <= lax.broadcasted_iota(jnp.int32, (SCAN_BLOCK, SCAN_BLOCK), 1)).astype(BF16)
    sel = jnp.where(chosen, 1.0, 0.0).astype(BF16)
    carry = run_ref[:, 0:1]
    for blk in range(n_tok // SCAN_BLOCK):
        lo = blk * SCAN_BLOCK
        run = jnp.dot(sel[:, lo:lo + SCAN_BLOCK], tri, preferred_element_type=F32) + carry
        for kk in range(TOP_K):
            rank = jnp.sum(jnp.where(picks[kk][:, lo:lo + SCAN_BLOCK], run - 1.0, 0.0),
                           axis=0, keepdims=True)
            r_ref[kk:kk + 1, lo:lo + SCAN_BLOCK] = rank.astype(jnp.int32)
        carry = run[:, SCAN_BLOCK - 1:SCAN_BLOCK]
    run_ref[...] = jnp.broadcast_to(carry, run_ref.shape)
    cnt_ref[...] = jnp.broadcast_to(carry, cnt_ref.shape).astype(jnp.int32)


def _route(logits_t):
    n = logits_t.shape[1]
    chunk = min(ROUTE_CHUNK, n)
    assert n % chunk == 0 and chunk % SCAN_BLOCK == 0
    per_k = pl.BlockSpec((TOP_K, chunk), lambda c: (0, c))
    return pl.pallas_call(
        _route_kernel,
        grid=(n // chunk,),
        in_specs=[pl.BlockSpec((N_EXPERTS, chunk), lambda c: (0, c))],
        out_specs=[per_k, per_k, per_k, pl.BlockSpec((N_EXPERTS, LANES), lambda c: (0, 0))],
        out_shape=[jax.ShapeDtypeStruct((TOP_K, n), jnp.int32),
                   jax.ShapeDtypeStruct((TOP_K, n), jnp.int32),
                   jax.ShapeDtypeStruct((TOP_K, n), F32),
                   jax.ShapeDtypeStruct((N_EXPERTS, LANES), jnp.int32)],
        scratch_shapes=[pltpu.VMEM((N_EXPERTS, LANES), F32)],
        compiler_params=_params("arbitrary"),
        name="route",
    )(logits_t)


def _plan(expert_k, rank_k, counts, n_tiles_max):
    padded = (counts + EXPERT_ROWS - 1) // EXPERT_ROWS * EXPERT_ROWS
    group_end = jnp.cumsum(padded)
    group_start = group_end - padded
    start_k = jnp.sum(jnp.where(expert_k[None] == jnp.arange(N_EXPERTS)[:, None, None],
                                group_start[:, None, None], 0), axis=0)
    slot = (start_k + rank_k).T.reshape(-1)
    n_tiles = group_end[-1] // EXPERT_ROWS
    tile = jnp.minimum(jnp.arange(n_tiles_max), n_tiles - 1)
    tile_expert = jnp.sum(tile[:, None] * EXPERT_ROWS >= group_end[None, :], axis=1)
    return (slot.astype(jnp.int32), tile_expert.astype(jnp.int32),
            n_tiles.reshape(1).astype(jnp.int32),
            group_start.astype(jnp.int32), counts.astype(jnp.int32), padded.astype(jnp.int32))


def _dispatch_kernel(slot_ref, start_ref, cnt_ref, pad_ref, nt_ref, t_ref, xs_ref, zero_ref,
                     sem, zsem):
    tokens = t_ref.shape[0] // SUBLANES
    base = pl.program_id(0) * tokens
    tile_rows = zero_ref.shape[0]
    n_tiles_max = xs_ref.shape[0] // tile_rows

    def row_copy(src, row, sem_):
        dst = xs_ref.at[pl.ds(pl.multiple_of(row * SUBLANES, SUBLANES), SUBLANES), :]
        return pltpu.make_async_copy(src, dst, sem_)

    def tile_copy(tile):
        dst = xs_ref.at[pl.ds(pl.multiple_of(tile * tile_rows, tile_rows), tile_rows), :]
        return pltpu.make_async_copy(zero_ref, dst, zsem)

    @pl.when(pl.program_id(0) == 0)
    def _():
        zero_ref[...] = jnp.zeros_like(zero_ref)
        zero_row = zero_ref.at[pl.ds(0, SUBLANES), :]
        for e in range(N_EXPERTS):
            first = start_ref[e] + cnt_ref[e]
            n_pad = pad_ref[e] - cnt_ref[e]

            def fill(i, carry):
                row_copy(zero_row, first + i, zsem).start()
                return carry

            def drain(i, carry):
                row_copy(zero_row, first, zsem).wait()
                return carry

            lax.fori_loop(0, n_pad, fill, 0)
            lax.fori_loop(0, n_pad, drain, 0)

        def fill_tile(i, carry):
            tile_copy(i).start()
            return carry

        def drain_tile(i, carry):
            tile_copy(i).wait()
            return carry

        lax.fori_loop(nt_ref[0], n_tiles_max, fill_tile, 0)
        lax.fori_loop(nt_ref[0], n_tiles_max, drain_tile, 0)

    def scatter(i, carry):
        src = t_ref.at[pl.ds(pl.multiple_of(i * SUBLANES, SUBLANES), SUBLANES), :]
        for kk in range(TOP_K):
            row_copy(src, slot_ref[(base + i) * TOP_K + kk], sem).start()
        return carry

    lax.fori_loop(0, tokens, scatter, 0, unroll=8)
    for kk in range(TOP_K):
        pltpu.make_async_copy(t_ref, xs_ref.at[pl.ds(0, tokens * SUBLANES), :], sem).wait()


def _dispatch(slot, group_start, counts, padded, n_tiles, t_tiles, n_rows):
    n = t_tiles.shape[0] // SUBLANES
    tokens = min(DISPATCH_TOKENS, n)
    assert n % tokens == 0 and n_rows % EXPERT_ROWS == 0
    grid_spec = pltpu.PrefetchScalarGridSpec(
        num_scalar_prefetch=5,
        grid=(n // tokens,),
        in_specs=[pl.BlockSpec((tokens * SUBLANES, LANES), lambda j, *_: (j, 0))],
        out_specs=pl.BlockSpec(memory_space=pl.ANY),
        scratch_shapes=[pltpu.VMEM((EXPERT_ROWS * SUBLANES, LANES), F32),
                        pltpu.SemaphoreType.DMA, pltpu.SemaphoreType.DMA],
    )
    return pl.pallas_call(
        _dispatch_kernel,
        grid_spec=grid_spec,
        out_shape=jax.ShapeDtypeStruct((n_rows * SUBLANES, LANES), F32),
        compiler_params=pltpu.CompilerParams(dimension_semantics=("arbitrary",),
                                             vmem_limit_bytes=VMEM_LIMIT_BYTES,
                                             has_side_effects=True),
        name="dispatch",
    )(slot, group_start, counts, padded, n_tiles, t_tiles)


def _experts_kernel(te_ref, nt_ref, x_ref, wgu_ref, bgu_ref, wd_ref, bd_ref, y_ref,
                    wgu_bf, wd_bf):
    i = pl.program_id(0)
    rows = x_ref.shape[0] // SUBLANES
    d_ff = wd_ref.shape[1]
    n_blocks = wgu_ref.shape[1] // LANES

    new_group = jnp.logical_or(i == 0, te_ref[i] != te_ref[jnp.maximum(i - 1, 0)])

    @pl.when(jnp.logical_and(new_group, i < nt_ref[0]))
    def _():
        wgu_bf[...] = wgu_ref[0].astype(BF16)
        wd_bf[...] = wd_ref[0].astype(BF16)

    @pl.when(i < nt_ref[0])
    def _():
        x = jnp.concatenate([x_ref[pl.ds(s, rows, stride=SUBLANES), :] for s in range(n_blocks)],
                            axis=1).astype(BF16)
        gu = jnp.dot(x, wgu_bf[...], preferred_element_type=F32) + bgu_ref[0]
        gate = jnp.minimum(gu[:, :d_ff], SWIGLU_LIMIT)
        lin = jnp.clip(gu[:, d_ff:], -SWIGLU_LIMIT, SWIGLU_LIMIT)
        act = ((lin + 1.0) * (gate * jax.nn.sigmoid(SWIGLU_ALPHA * gate))).astype(BF16)
        y = jnp.dot(act, wd_bf[...], preferred_element_type=F32) + bd_ref[0]
        for s in range(n_blocks):
            y_ref[pl.ds(s, rows, stride=SUBLANES), :] = y[:, s * LANES:(s + 1) * LANES]

    @pl.when(i >= nt_ref[0])
    def _():
        y_ref[...] = jnp.zeros_like(y_ref)


def _experts(tile_expert, n_tiles, xs, w_gu, b_gu, w_down, b_down):
    n_exp, d, two_ff = w_gu.shape
    d_ff = two_ff // 2
    n_tiles_max = tile_expert.shape[0]
    tile = lambda i, te, nt: (i, 0)
    by_expert = lambda i, te, nt: (te[i], 0, 0)
    rows_spec = pl.BlockSpec((EXPERT_ROWS * SUBLANES, LANES), tile)
    grid_spec = pltpu.PrefetchScalarGridSpec(
        num_scalar_prefetch=2,
        grid=(n_tiles_max,),
        in_specs=[rows_spec,
                  pl.BlockSpec((1, d, two_ff), by_expert),
                  pl.BlockSpec((1, 1, two_ff), by_expert),
                  pl.BlockSpec((1, d_ff, d), by_expert),
                  pl.BlockSpec((1, 1, d), by_expert)],
        out_specs=rows_spec,
        scratch_shapes=[pltpu.VMEM((d, two_ff), BF16), pltpu.VMEM((d_ff, d), BF16)],
    )
    return pl.pallas_call(
        _experts_kernel,
        grid_spec=grid_spec,
        out_shape=jax.ShapeDtypeStruct(xs.shape, F32),
        compiler_params=_params("arbitrary"),
        name="experts",
    )(tile_expert, n_tiles, xs, w_gu, b_gu.reshape(n_exp, 1, two_ff), w_down,
      b_down.reshape(n_exp, 1, d))


def _combine_kernel(slot_ref, ys_ref, gate_ref, x1_ref, g2_ref, lng_ref, lnb_ref, o_ref, buf, sem):
    j = pl.program_id(0)
    tokens = o_ref.shape[0]
    n_blocks = o_ref.shape[1] // LANES
    tile_rows = TOP_K * SUBLANES

    def issue(block, which):
        def gather(i, carry):
            for kk in range(TOP_K):
                row = slot_ref[(block * tokens + i) * TOP_K + kk]
                src = ys_ref.at[pl.ds(pl.multiple_of(row * SUBLANES, SUBLANES), SUBLANES), :]
                dst = buf.at[which, pl.ds(pl.multiple_of((i * TOP_K + kk) * SUBLANES, SUBLANES),
                                          SUBLANES), :]
                pltpu.make_async_copy(src, dst, sem.at[which]).start()
            return carry

        lax.fori_loop(0, tokens, gather, 0, unroll=8)

    @pl.when(j == 0)
    def _():
        issue(0, 0)

    @pl.when(j + 1 < pl.num_programs(0))
    def _():
        issue(j + 1, (j + 1) % 2)

    cur = j % 2
    pltpu.make_async_copy(ys_ref.at[pl.ds(0, tokens * tile_rows), :], buf.at[cur],
                          sem.at[cur]).wait()
    gates = gate_ref[...]
    pieces = []
    for s in range(n_blocks):
        acc = None
        for kk in range(TOP_K):
            part = gates[:, kk:kk + 1] * buf[cur, pl.ds(kk * SUBLANES + s, tokens, stride=tile_rows), :]
            acc = part if acc is None else acc + part
        pieces.append(acc)
    ffn = jnp.concatenate(pieces, axis=1)
    z = DEEPNORM_ALPHA * x1_ref[...] + g2_ref[0] * ffn
    o_ref[...] = _normalize(z) * lng_ref[...] + lnb_ref[...]


def _combine(slot, ys, gates_tok, x1, g2, ln_g, ln_b, tokens_per_sample):
    n, d = x1.shape
    tokens = min(COMBINE_TOKENS, tokens_per_sample)
    assert tokens_per_sample % tokens == 0
    blocks_per_sample = tokens_per_sample // tokens
    vec = pl.BlockSpec((1, d), lambda j, *_: (0, 0))
    grid_spec = pltpu.PrefetchScalarGridSpec(
        num_scalar_prefetch=1,
        grid=(n // tokens,),
        in_specs=[pl.BlockSpec(memory_space=pl.ANY),
                  pl.BlockSpec((tokens, TOP_K), lambda j, *_: (j, 0)),
                  pl.BlockSpec((tokens, d), lambda j, *_: (j, 0)),
                  pl.BlockSpec((1, 1, d), lambda j, *_: (j // blocks_per_sample, 0, 0)),
                  vec, vec],
        out_specs=pl.BlockSpec((tokens, d), lambda j, *_: (j, 0)),
        scratch_shapes=[pltpu.VMEM((2, tokens * TOP_K * SUBLANES, LANES), F32),
                        pltpu.SemaphoreType.DMA((2,))],
    )
    return pl.pallas_call(
        _combine_kernel,
        grid_spec=grid_spec,
        out_shape=jax.ShapeDtypeStruct((n, d), F32),
        compiler_params=_params("arbitrary"),
        name="combine",
    )(slot, ys, gates_tok, x1, g2, ln_g.reshape(1, d), ln_b.reshape(1, d))


def _layer(x, c, ctx, c_ctx, ada_w, ada_b, w_in, rpb, sgu_ln_g, sgu_ln_b, sgu_w, sgu_b, w_out,
           ln1_g, ln1_b, ln2_g, ln2_b, router_w, router_b, w_gu, b_gu, w_down, b_down):
    b, s, d = x.shape
    n = b * s
    assert s % (GRID_W * ATT_ROWS) == 0 and s // GRID_W >= NA_KH

    cond_rows = jnp.zeros((8, d), F32).at[:b].set(c).at[b].set(c_ctx)
    mod = _ada(cond_rows, ada_w, ada_b)
    sh1, sc1, g1, sh2, sc2, g2 = jnp.split(mod[:b], 6, axis=-1)
    csh1, csc1 = mod[b, :d], mod[b, d:2 * d]

    w_in_bf = w_in.astype(BF16)
    k_c, v_c = _ctx_kv(ctx, csh1, csc1, w_in_bf[:, NA_WIDTH:3 * NA_WIDTH])
    q, k, v, sgu = _proj_in(x, sh1, sc1, w_in_bf, sgu_ln_g, sgu_ln_b, sgu_w, sgu_b)
    att = _natten(q, k, v, k_c, v_c, _bias_table(rpb))
    x1, t_tiles, logits_t = _mix_out(att, sgu, x, w_out.astype(BF16), g1, sh2, sc2, ln1_g, ln1_b,
                                     router_w, router_b)

    expert_k, rank_k, gate_k, counts = _route(logits_t)
    n_tiles_max = n * TOP_K // EXPERT_ROWS + N_EXPERTS
    slot, tile_expert, n_tiles, group_start, counts, padded = _plan(
        expert_k, rank_k, counts[:, 0], n_tiles_max)
    xs = _dispatch(slot, group_start, counts, padded, n_tiles, t_tiles, n_tiles_max * EXPERT_ROWS)
    ys = _experts(tile_expert, n_tiles, xs, w_gu, b_gu, w_down, b_down)
    out = _combine(slot, ys, gate_k.T, x1.reshape(n, d), g2.reshape(b, 1, d), ln2_g, ln2_b, s)
    return out.reshape(b, s, d)


def kernel(x, c, ctx, c_ctx, ada_w, ada_b, w_in, rpb, sgu_ln_g, sgu_ln_b, sgu_w, sgu_b, w_out,
           ln1_g, ln1_b, ln2_g, ln2_b, router_w, router_b, exp_w_gu, exp_b_gu, exp_w_down,
           exp_b_down):
    assert ada_w.shape[0] == DEPTH
    return _layer(x, c, ctx, c_ctx, ada_w[0], ada_b[0], w_in[0], rpb[0], sgu_ln_g[0], sgu_ln_b[0],
                  sgu_w[0], sgu_b[0], w_out[0], ln1_g[0], ln1_b[0], ln2_g[0], ln2_b[0],
                  router_w[0], router_b[0], exp_w_gu[0], exp_b_gu[0], exp_w_down[0], exp_b_down[0])
```

```python
import functools

import jax
import jax.numpy as jnp
from jax import lax
from jax.experimental import pallas as pl
from jax.experimental.pallas import tpu as pltpu

F32 = jnp.float32
BF16 = jnp.bfloat16

GRID_W = 64
NA_HEADS = 8
NA_HEAD_DIM = 64
NA_WIDTH = NA_HEADS * NA_HEAD_DIM
NA_KH = 8
NA_KW = 16
SGU_GROUPS = 4
SGU_GROUP_DIM = 128
SGU_WIDTH = SGU_GROUPS * SGU_GROUP_DIM
CHUNK = 128
N_EXPERTS = 32
TOP_K = 4
SWIGLU_LIMIT = 7.0
SWIGLU_ALPHA = 1.702
LN_EPS = 1e-5
DEPTH = 1
DEEPNORM_ALPHA = (2.0 * DEPTH) ** 0.25
MASKED = -1e30

SUBLANES = 8
LANES = 128
VMEM_LIMIT_BYTES = 52 * 1024 * 1024

ROW_TILE = 512
ATT_ROWS = 8
ROUTE_CHUNK = 1024
SCAN_BLOCK = 256
EXPERT_ROWS = 256
DISPATCH_TOKENS = 1024
COMBINE_TOKENS = 256


def _params(*sem):
    return pltpu.CompilerParams(dimension_semantics=sem, vmem_limit_bytes=VMEM_LIMIT_BYTES)


def _normalize(x):
    mu = jnp.mean(x, axis=-1, keepdims=True)
    xc = x - mu
    var = jnp.mean(xc * xc, axis=-1, keepdims=True)
    return xc * lax.rsqrt(var + LN_EPS)


def _ada_kernel(c_ref, w_ref, b_ref, o_ref):
    s = c_ref[...]
    s = s * jax.nn.sigmoid(s)
    o_ref[...] = jnp.dot(s, w_ref[...], precision=lax.Precision.HIGHEST,
                         preferred_element_type=F32) + b_ref[...]


def _ada(cond_rows, ada_w, ada_b):
    d = cond_rows.shape[1]
    n_out = ada_w.shape[1]
    return pl.pallas_call(
        _ada_kernel,
        grid=(n_out // d,),
        in_specs=[pl.BlockSpec((8, d), lambda j: (0, 0)),
                  pl.BlockSpec((d, d), lambda j: (0, j)),
                  pl.BlockSpec((1, d), lambda j: (0, j))],
        out_specs=pl.BlockSpec((8, d), lambda j: (0, j)),
        out_shape=jax.ShapeDtypeStruct((8, n_out), F32),
        compiler_params=_params("arbitrary"),
        name="ada",
    )(cond_rows, ada_w, ada_b.reshape(1, n_out))


def _ctx_kv_kernel(ctx_ref, sh_ref, sc_ref, w_ref, k_ref, v_ref):
    h = _normalize(ctx_ref[0]) * (1.0 + sc_ref[...]) + sh_ref[...]
    kv = jnp.dot(h.astype(BF16), w_ref[...], preferred_element_type=F32)
    k_ref[0] = kv[:, :NA_WIDTH].astype(BF16)
    v_ref[0] = kv[:, NA_WIDTH:].astype(BF16)


def _ctx_kv(ctx, csh1, csc1, w_kv):
    b, l, d = ctx.shape
    out = jax.ShapeDtypeStruct((b, l, NA_WIDTH), BF16)
    return pl.pallas_call(
        _ctx_kv_kernel,
        grid=(b,),
        in_specs=[pl.BlockSpec((1, l, d), lambda i: (i, 0, 0)),
                  pl.BlockSpec((1, d), lambda i: (0, 0)),
                  pl.BlockSpec((1, d), lambda i: (0, 0)),
                  pl.BlockSpec((d, 2 * NA_WIDTH), lambda i: (0, 0))],
        out_specs=[pl.BlockSpec((1, l, NA_WIDTH), lambda i: (i, 0, 0)),
                   pl.BlockSpec((1, l, NA_WIDTH), lambda i: (i, 0, 0))],
        out_shape=[out, out],
        compiler_params=_params("arbitrary"),
        name="ctx_kv",
    )(ctx, csh1.reshape(1, d), csc1.reshape(1, d), w_kv)


def _proj_in_kernel(x_ref, sh_ref, sc_ref, w_ref, lng_ref, lnb_ref, ws_ref, bs_ref,
                    q_ref, k_ref, v_ref, s_ref):
    h = (_normalize(x_ref[0]) * (1.0 + sc_ref[0]) + sh_ref[0]).astype(BF16)

    def proj(lo, width):
        return jnp.dot(h, w_ref[:, lo:lo + width], preferred_element_type=F32)

    q_ref[0] = (proj(0, NA_WIDTH) * (NA_HEAD_DIM ** -0.5)).astype(BF16)
    k_ref[0] = proj(NA_WIDTH, NA_WIDTH).astype(BF16)
    v_ref[0] = proj(2 * NA_WIDTH, NA_WIDTH).astype(BF16)
    u = jax.nn.gelu(proj(3 * NA_WIDTH, SGU_WIDTH))
    g = jax.nn.gelu(proj(3 * NA_WIDTH + SGU_WIDTH, SGU_WIDTH))
    gn = (_normalize(g) * lng_ref[...] + lnb_ref[...]).astype(BF16)
    rows = h.shape[0]
    for n in range(rows // CHUNK):
        r0 = n * CHUNK
        for grp in range(SGU_GROUPS):
            c0 = grp * SGU_GROUP_DIM
            mixed = jnp.dot(ws_ref[grp], gn[r0:r0 + CHUNK, c0:c0 + SGU_GROUP_DIM],
                            preferred_element_type=F32) + bs_ref[grp]
            s_ref[0, r0:r0 + CHUNK, c0:c0 + SGU_GROUP_DIM] = (
                u[r0:r0 + CHUNK, c0:c0 + SGU_GROUP_DIM] * mixed).astype(BF16)


def _proj_in(x, sh1, sc1, w_in, sgu_ln_g, sgu_ln_b, sgu_w, sgu_b):
    b, s, d = x.shape
    d_in = w_in.shape[1]
    tm = min(ROW_TILE, s)
    out = jax.ShapeDtypeStruct((b, s, NA_WIDTH), BF16)
    row_spec = pl.BlockSpec((1, tm, NA_WIDTH), lambda i, j: (i, j, 0))
    mod_spec = pl.BlockSpec((1, 1, d), lambda i, j: (i, 0, 0))
    bs = jnp.broadcast_to(sgu_b[:, :, None], (SGU_GROUPS, CHUNK, SGU_GROUP_DIM))
    return pl.pallas_call(
        _proj_in_kernel,
        grid=(b, s // tm),
        in_specs=[pl.BlockSpec((1, tm, d), lambda i, j: (i, j, 0)),
                  mod_spec, mod_spec,
                  pl.BlockSpec((d, d_in), lambda i, j: (0, 0)),
                  pl.BlockSpec((1, SGU_WIDTH), lambda i, j: (0, 0)),
                  pl.BlockSpec((1, SGU_WIDTH), lambda i, j: (0, 0)),
                  pl.BlockSpec((SGU_GROUPS, CHUNK, CHUNK), lambda i, j: (0, 0, 0)),
                  pl.BlockSpec((SGU_GROUPS, CHUNK, SGU_GROUP_DIM), lambda i, j: (0, 0, 0))],
        out_specs=[row_spec, row_spec, row_spec, row_spec],
        out_shape=[out, out, out, out],
        compiler_params=_params("parallel", "parallel"),
        name="proj_in",
    )(x, sh1.reshape(b, 1, d), sc1.reshape(b, 1, d), w_in,
      sgu_ln_g.reshape(1, SGU_WIDTH), sgu_ln_b.reshape(1, SGU_WIDTH), sgu_w.astype(BF16), bs)


def _bias_table(rpb):
    heads, n_dr, n_dc = rpb.shape
    c = jnp.arange(GRID_W)
    cs = jnp.clip(c - NA_KW // 2, 0, GRID_W - NA_KW)
    kc = jnp.arange(GRID_W)
    valid = (kc[None, :] >= cs[:, None]) & (kc[None, :] < cs[:, None] + NA_KW)
    lead = GRID_W - NA_KW
    padded = jnp.pad(rpb.astype(F32), ((0, 0), (0, 0), (lead, 2 * GRID_W - lead - n_dc)))
    flat = jnp.tile(padded, (1, 1, GRID_W))[:, :, :GRID_W * (2 * GRID_W - 1)]
    toe = flat.reshape(heads, n_dr, GRID_W, 2 * GRID_W - 1)[..., GRID_W - 1:]
    toe = jnp.where(valid, toe, MASKED)
    tab = jnp.stack([toe[:, NA_KH - 1 - o:2 * NA_KH - 1 - o] for o in range(NA_KH)])
    tab = tab.transpose(0, 1, 3, 2, 4)
    return tab.reshape(NA_KH, NA_HEADS * GRID_W, NA_KH * GRID_W)


def _natten_kernel(q_ref, k_ref, v_ref, kc_ref, vc_ref, bias_ref, o_ref, *, grid_rows):
    stacked = NA_HEADS * GRID_W
    row_head = lax.broadcasted_iota(jnp.int32, (stacked, NA_WIDTH), 0) // GRID_W
    lane_head = lax.broadcasted_iota(jnp.int32, (stacked, NA_WIDTH), 1) // NA_HEAD_DIM
    own_head = row_head == lane_head
    kc = kc_ref[0]
    vc = vc_ref[0]
    nt = (((1,), (1,)), ((), ()))

    def one_row(i, carry):
        r = pl.program_id(1) * ATT_ROWS + i
        rs = jnp.clip(r - NA_KH // 2, 0, grid_rows - NA_KH)
        k0 = pl.multiple_of(rs * GRID_W, GRID_W)
        q0 = pl.multiple_of(i * GRID_W, GRID_W)
        q = q_ref[0, pl.ds(q0, GRID_W), :]
        qs = jnp.where(own_head, jnp.concatenate([q] * NA_HEADS, axis=0), jnp.zeros((), BF16))
        kr = k_ref[0, pl.ds(k0, NA_KH * GRID_W), :]
        vr = v_ref[0, pl.ds(k0, NA_KH * GRID_W), :]
        s_nb = lax.dot_general(qs, kr, nt, preferred_element_type=F32) + bias_ref[r - rs]
        s_cx = lax.dot_general(qs, kc, nt, preferred_element_type=F32)
        m = jnp.maximum(jnp.max(s_nb, axis=-1, keepdims=True), jnp.max(s_cx, axis=-1, keepdims=True))
        p_nb = jnp.exp(s_nb - m)
        p_cx = jnp.exp(s_cx - m)
        denom = jnp.sum(p_nb, axis=-1, keepdims=True) + jnp.sum(p_cx, axis=-1, keepdims=True)
        o = (jnp.dot(p_nb.astype(BF16), vr, preferred_element_type=F32)
             + jnp.dot(p_cx.astype(BF16), vc, preferred_element_type=F32)) / denom
        o = jnp.where(own_head, o, 0.0)
        acc = o[0:GRID_W]
        for h in range(1, NA_HEADS):
            acc = acc + o[h * GRID_W:(h + 1) * GRID_W]
        o_ref[0, pl.ds(q0, GRID_W), :] = acc.astype(BF16)
        return carry

    lax.fori_loop(0, ATT_ROWS, one_row, 0, unroll=True)


def _natten(q, k, v, k_c, v_c, bias):
    b, s, w = q.shape
    l = k_c.shape[1]
    grid_rows = s // GRID_W
    tq = ATT_ROWS * GRID_W
    full = pl.BlockSpec((1, s, w), lambda i, j: (i, 0, 0), pipeline_mode=pl.Buffered(1))
    ctx = pl.BlockSpec((1, l, w), lambda i, j: (i, 0, 0))
    return pl.pallas_call(
        functools.partial(_natten_kernel, grid_rows=grid_rows),
        grid=(b, grid_rows // ATT_ROWS),
        in_specs=[pl.BlockSpec((1, tq, w), lambda i, j: (i, j, 0)),
                  full, full, ctx, ctx,
                  pl.BlockSpec(bias.shape, lambda i, j: (0, 0, 0), pipeline_mode=pl.Buffered(1))],
        out_specs=pl.BlockSpec((1, tq, w), lambda i, j: (i, j, 0)),
        out_shape=jax.ShapeDtypeStruct((b, s, w), BF16),
        compiler_params=_params("parallel", "arbitrary"),
        name="natten",
    )(q, k, v, k_c, v_c, bias)


def _mix_out_kernel(att_ref, sgu_ref, x_ref, wo_ref, g1_ref, sh_ref, sc_ref, lng_ref, lnb_ref,
                    wr_ref, br_ref, x1_ref, t_ref, lg_ref):
    mix = (jnp.dot(att_ref[0], wo_ref[:NA_WIDTH, :], preferred_element_type=F32)
           + jnp.dot(sgu_ref[0], wo_ref[NA_WIDTH:, :], preferred_element_type=F32))
    x1 = _normalize(DEEPNORM_ALPHA * x_ref[0] + g1_ref[0] * mix) * lng_ref[...] + lnb_ref[...]
    x1_ref[0] = x1
    t = _normalize(x1) * (1.0 + sc_ref[0]) + sh_ref[0]
    rows = t.shape[0]
    for s in range(t.shape[1] // LANES):
        t_ref[pl.ds(s, rows, stride=SUBLANES), :] = t[:, s * LANES:(s + 1) * LANES]
    lg_ref[...] = lax.dot_general(wr_ref[...], t, (((1,), (1,)), ((), ())),
                                  precision=lax.Precision.HIGHEST,
                                  preferred_element_type=F32) + br_ref[...]


def _mix_out(att, sgu, x, w_out, g1, sh2, sc2, ln_g, ln_b, router_w, router_b):
    b, s, d = x.shape
    assert d == SUBLANES * LANES
    tm = min(ROW_TILE, s)
    nj = s // tm
    row = lambda width: pl.BlockSpec((1, tm, width), lambda i, j: (i, j, 0))
    mod_spec = pl.BlockSpec((1, 1, d), lambda i, j: (i, 0, 0))
    vec_spec = pl.BlockSpec((1, d), lambda i, j: (0, 0))
    return pl.pallas_call(
        _mix_out_kernel,
        grid=(b, nj),
        in_specs=[row(NA_WIDTH), row(SGU_WIDTH), row(d),
                  pl.BlockSpec((d, d), lambda i, j: (0, 0)),
                  mod_spec, mod_spec, mod_spec, vec_spec, vec_spec,
                  pl.BlockSpec((N_EXPERTS, d), lambda i, j: (0, 0)),
                  pl.BlockSpec((N_EXPERTS, 1), lambda i, j: (0, 0))],
        out_specs=[row(d),
                   pl.BlockSpec((tm * SUBLANES, LANES), lambda i, j: (i * nj + j, 0)),
                   pl.BlockSpec((N_EXPERTS, tm), lambda i, j: (0, i * nj + j))],
        out_shape=[jax.ShapeDtypeStruct((b, s, d), F32),
                   jax.ShapeDtypeStruct((b * s * SUBLANES, LANES), F32),
                   jax.ShapeDtypeStruct((N_EXPERTS, b * s), F32)],
        compiler_params=_params("parallel", "parallel"),
        name="mix_out",
    )(att, sgu, x, w_out, g1.reshape(b, 1, d), sh2.reshape(b, 1, d), sc2.reshape(b, 1, d),
      ln_g.reshape(1, d), ln_b.reshape(1, d), router_w.T, router_b.reshape(N_EXPERTS, 1))


def _route_kernel(lg_ref, e_ref, r_ref, g_ref, cnt_ref, run_ref):
    @pl.when(pl.program_id(0) == 0)
    def _():
        run_ref[...] = jnp.zeros_like(run_ref)

    logits = lg_ref[...]
    n_tok = logits.shape[1]
    expert = lax.broadcasted_iota(jnp.int32, logits.shape, 0)
    work = logits
    picks, tops = [], []
    for kk in range(TOP_K):
        m = jnp.max(work, axis=0, keepdims=True)
        first = jnp.min(jnp.where(work == m, expert, N_EXPERTS), axis=0, keepdims=True)
        pick = expert == first
        work = jnp.where(pick, -jnp.inf, work)
        picks.append(pick)
        tops.append(m)
        e_ref[kk:kk + 1, :] = first
    weights = [jnp.exp(m - tops[0]) for m in tops]
    denom = weights[0]
    for w in weights[1:]:
        denom = denom + w
    for kk in range(TOP_K):
        g_ref[kk:kk + 1, :] = weights[kk] / denom

    chosen = picks[0]
    for pick in picks[1:]:
        chosen = jnp.logical_or(chosen, pick)
    tri = (lax.broadcasted_iota(jnp.int32, (SCAN_BLOCK, SCAN_BLOCK), 0)
           <= lax.broadcasted_iota(jnp.int32, (SCAN_BLOCK, SCAN_BLOCK), 1)).astype(BF16)
    sel = jnp.where(chosen, 1.0, 0.0).astype(BF16)
    carry = run_ref[:, 0:1]
    for blk in range(n_tok // SCAN_BLOCK):
        lo = blk * SCAN_BLOCK
        run = jnp.dot(sel[:, lo:lo + SCAN_BLOCK], tri, preferred_element_type=F32) + carry
        for kk in range(TOP_K):
            rank = jnp.sum(jnp.where(picks[kk][:, lo:lo + SCAN_BLOCK], run - 1.0, 0.0),
                           axis=0, keepdims=True)
            r_ref[kk:kk + 1, lo:lo + SCAN_BLOCK] = rank.astype(jnp.int32)
        carry = run[:, SCAN_BLOCK - 1:SCAN_BLOCK]
    run_ref[...] = jnp.broadcast_to(carry, run_ref.shape)
    cnt_ref[...] = jnp.broadcast_to(carry, cnt_ref.shape).astype(jnp.int32)


def _route(logits_t):
    n = logits_t.shape[1]
    chunk = min(ROUTE_CHUNK, n)
    assert n % chunk == 0 and chunk % SCAN_BLOCK == 0
    per_k = pl.BlockSpec((TOP_K, chunk), lambda c: (0, c))
    return pl.pallas_call(
        _route_kernel,
        grid=(n // chunk,),
        in_specs=[pl.BlockSpec((N_EXPERTS, chunk), lambda c: (0, c))],
        out_specs=[per_k, per_k, per_k, pl.BlockSpec((N_EXPERTS, LANES), lambda c: (0, 0))],
        out_shape=[jax.ShapeDtypeStruct((TOP_K, n), jnp.int32),
                   jax.ShapeDtypeStruct((TOP_K, n), jnp.int32),
                   jax.ShapeDtypeStruct((TOP_K, n), F32),
                   jax.ShapeDtypeStruct((N_EXPERTS, LANES), jnp.int32)],
        scratch_shapes=[pltpu.VMEM((N_EXPERTS, LANES), F32)],
        compiler_params=_params("arbitrary"),
        name="route",
    )(logits_t)


def _plan(expert_k, rank_k, counts, n_tiles_max):
    padded = (counts + EXPERT_ROWS - 1) // EXPERT_ROWS * EXPERT_ROWS
    group_end = jnp.cumsum(padded)
    group_start = group_end - padded
    start_k = jnp.sum(jnp.where(expert_k[None] == jnp.arange(N_EXPERTS)[:, None, None],
                                group_start[:, None, None], 0), axis=0)
    slot = (start_k + rank_k).T.reshape(-1)
    n_tiles = group_end[-1] // EXPERT_ROWS
    tile = jnp.minimum(jnp.arange(n_tiles_max), n_tiles - 1)
    tile_expert = jnp.sum(tile[:, None] * EXPERT_ROWS >= group_end[None, :], axis=1)
    return (slot.astype(jnp.int32), tile_expert.astype(jnp.int32),
            n_tiles.reshape(1).astype(jnp.int32),
            group_start.astype(jnp.int32), counts.astype(jnp.int32), padded.astype(jnp.int32))


def _dispatch_kernel(slot_ref, start_ref, cnt_ref, pad_ref, nt_ref, t_ref, xs_ref, zero_ref,
                     sem, zsem):
    tokens = t_ref.shape[0] // SUBLANES
    base = pl.program_id(0) * tokens
    tile_rows = zero_ref.shape[0]
    n_tiles_max = xs_ref.shape[0] // tile_rows

    def row_copy(src, row, sem_):
        dst = xs_ref.at[pl.ds(pl.multiple_of(row * SUBLANES, SUBLANES), SUBLANES), :]
        return pltpu.make_async_copy(src, dst, sem_)

    def tile_copy(tile):
        dst = xs_ref.at[pl.ds(pl.multiple_of(tile * tile_rows, tile_rows), tile_rows), :]
        return pltpu.make_async_copy(zero_ref, dst, zsem)

    @pl.when(pl.program_id(0) == 0)
    def _():
        zero_ref[...] = jnp.zeros_like(zero_ref)
        zero_row = zero_ref.at[pl.ds(0, SUBLANES), :]
        for e in range(N_EXPERTS):
            first = start_ref[e] + cnt_ref[e]
            n_pad = pad_ref[e] - cnt_ref[e]

            def fill(i, carry):
                row_copy(zero_row, first + i, zsem).start()
                return carry

            def drain(i, carry):
                row_copy(zero_row, first, zsem).wait()
                return carry

            lax.fori_loop(0, n_pad, fill, 0)
            lax.fori_loop(0, n_pad, drain, 0)

        def fill_tile(i, carry):
            tile_copy(i).start()
            return carry

        def drain_tile(i, carry):
            tile_copy(i).wait()
            return carry

        lax.fori_loop(nt_ref[0], n_tiles_max, fill_tile, 0)
        lax.fori_loop(nt_ref[0], n_tiles_max, drain_tile, 0)

    def scatter(i, carry):
        src = t_ref.at[pl.ds(pl.multiple_of(i * SUBLANES, SUBLANES), SUBLANES), :]
        for kk in range(TOP_K):
            row_copy(src, slot_ref[(base + i) * TOP_K + kk], sem).start(priority=kk % 2)
        return carry

    lax.fori_loop(0, tokens, scatter, 0, unroll=8)
    for kk in range(TOP_K):
        pltpu.make_async_copy(t_ref, xs_ref.at[pl.ds(0, tokens * SUBLANES), :], sem).wait()


def _dispatch(slot, group_start, counts, padded, n_tiles, t_tiles, n_rows):
    n = t_tiles.shape[0] // SUBLANES
    tokens = min(DISPATCH_TOKENS, n)
    assert n % tokens == 0 and n_rows % EXPERT_ROWS == 0
    grid_spec = pltpu.PrefetchScalarGridSpec(
        num_scalar_prefetch=5,
        grid=(n // tokens,),
        in_specs=[pl.BlockSpec((tokens * SUBLANES, LANES), lambda j, *_: (j, 0))],
        out_specs=pl.BlockSpec(memory_space=pl.ANY),
        scratch_shapes=[pltpu.VMEM((EXPERT_ROWS * SUBLANES, LANES), F32),
                        pltpu.SemaphoreType.DMA, pltpu.SemaphoreType.DMA],
    )
    return pl.pallas_call(
        _dispatch_kernel,
        grid_spec=grid_spec,
        out_shape=jax.ShapeDtypeStruct((n_rows * SUBLANES, LANES), F32),
        compiler_params=pltpu.CompilerParams(dimension_semantics=("arbitrary",),
                                             vmem_limit_bytes=VMEM_LIMIT_BYTES,
                                             has_side_effects=True),
        name="dispatch",
    )(slot, group_start, counts, padded, n_tiles, t_tiles)


def _experts_kernel(te_ref, nt_ref, x_ref, wgu_ref, bgu_ref, wd_ref, bd_ref, y_ref,
                    wgu_bf, wd_bf):
    i = pl.program_id(0)
    rows = x_ref.shape[0] // SUBLANES
    d_ff = wd_ref.shape[1]
    n_blocks = wgu_ref.shape[1] // LANES

    new_group = jnp.logical_or(i == 0, te_ref[i] != te_ref[jnp.maximum(i - 1, 0)])

    @pl.when(jnp.logical_and(new_group, i < nt_ref[0]))
    def _():
        wgu_bf[...] = wgu_ref[0].astype(BF16)
        wd_bf[...] = wd_ref[0].astype(BF16)

    @pl.when(i < nt_ref[0])
    def _():
        x = jnp.concatenate([x_ref[pl.ds(s, rows, stride=SUBLANES), :] for s in range(n_blocks)],
                            axis=1).astype(BF16)
        gu = jnp.dot(x, wgu_bf[...], preferred_element_type=F32) + bgu_ref[0]
        gate = jnp.minimum(gu[:, :d_ff], SWIGLU_LIMIT)
        lin = jnp.clip(gu[:, d_ff:], -SWIGLU_LIMIT, SWIGLU_LIMIT)
        act = ((lin + 1.0) * (gate * jax.nn.sigmoid(SWIGLU_ALPHA * gate))).astype(BF16)
        y = jnp.dot(act, wd_bf[...], preferred_element_type=F32) + bd_ref[0]
        for s in range(n_blocks):
            y_ref[pl.ds(s, rows, stride=SUBLANES), :] = y[:, s * LANES:(s + 1) * LANES]

    @pl.when(i >= nt_ref[0])
    def _():
        y_ref[...] = jnp.zeros_like(y_ref)


def _experts(tile_expert, n_tiles, xs, w_gu, b_gu, w_down, b_down):
    n_exp, d, two_ff = w_gu.shape
    d_ff = two_ff // 2
    n_tiles_max = tile_expert.shape[0]
    tile = lambda i, te, nt: (i, 0)
    by_expert = lambda i, te, nt: (te[i], 0, 0)
    rows_spec = pl.BlockSpec((EXPERT_ROWS * SUBLANES, LANES), tile)
    grid_spec = pltpu.PrefetchScalarGridSpec(
        num_scalar_prefetch=2,
        grid=(n_tiles_max,),
        in_specs=[rows_spec,
                  pl.BlockSpec((1, d, two_ff), by_expert),
                  pl.BlockSpec((1, 1, two_ff), by_expert),
                  pl.BlockSpec((1, d_ff, d), by_expert),
                  pl.BlockSpec((1, 1, d), by_expert)],
        out_specs=rows_spec,
        scratch_shapes=[pltpu.VMEM((d, two_ff), BF16), pltpu.VMEM((d_ff, d), BF16)],
    )
    return pl.pallas_call(
        _experts_kernel,
        grid_spec=grid_spec,
        out_shape=jax.ShapeDtypeStruct(xs.shape, F32),
        compiler_params=_params("arbitrary"),
        name="experts",
    )(tile_expert, n_tiles, xs, w_gu, b_gu.reshape(n_exp, 1, two_ff), w_down,
      b_down.reshape(n_exp, 1, d))


def _combine_kernel(slot_ref, gate_ref, ys_ref, x1_ref, g2_ref, lng_ref, lnb_ref, o_ref,
                    buf, acc_ref, sem):
    j = pl.program_id(0)
    tokens = o_ref.shape[0]
    n_blocks = o_ref.shape[1] // LANES
    tile_rows = TOP_K * SUBLANES
    cur = j % 2

    def gather_token(block, which, i):
        for kk in range(TOP_K):
            row = slot_ref[(block * tokens + i) * TOP_K + kk]
            src = ys_ref.at[pl.ds(pl.multiple_of(row * SUBLANES, SUBLANES), SUBLANES), :]
            dst = buf.at[which, pl.ds(pl.multiple_of((i * TOP_K + kk) * SUBLANES, SUBLANES),
                                      SUBLANES), :]
            pltpu.make_async_copy(src, dst, sem.at[which]).start(priority=kk % 2)

    def reduce_token(i):
        tile = None
        for kk in range(TOP_K):
            row0 = pl.multiple_of((i * TOP_K + kk) * SUBLANES, SUBLANES)
            part = gate_ref[(j * tokens + i) * TOP_K + kk] * buf[cur, pl.ds(row0, SUBLANES), :]
            tile = part if tile is None else tile + part
        acc_ref[pl.ds(pl.multiple_of(i * SUBLANES, SUBLANES), SUBLANES), :] = tile

    @pl.when(j == 0)
    def _():
        def first(i, carry):
            gather_token(0, 0, i)
            return carry
        lax.fori_loop(0, tokens, first, 0, unroll=8)

    pltpu.make_async_copy(ys_ref.at[pl.ds(0, tokens * tile_rows), :], buf.at[cur],
                          sem.at[cur]).wait()

    @pl.when(j + 1 < pl.num_programs(0))
    def _():
        def step(i, carry):
            gather_token(j + 1, 1 - cur, i)
            reduce_token(i)
            return carry
        lax.fori_loop(0, tokens, step, 0, unroll=8)

    @pl.when(j + 1 == pl.num_programs(0))
    def _():
        def step(i, carry):
            reduce_token(i)
            return carry
        lax.fori_loop(0, tokens, step, 0, unroll=8)

    ffn = jnp.concatenate([acc_ref[pl.ds(s, tokens, stride=SUBLANES), :] for s in range(n_blocks)],
                          axis=1)
    z = DEEPNORM_ALPHA * x1_ref[...] + g2_ref[0] * ffn
    o_ref[...] = _normalize(z) * lng_ref[...] + lnb_ref[...]


def _combine(slot, gates_flat, ys, x1, g2, ln_g, ln_b, tokens_per_sample):
    n, d = x1.shape
    tokens = min(COMBINE_TOKENS, tokens_per_sample)
    assert tokens_per_sample % tokens == 0
    blocks_per_sample = tokens_per_sample // tokens
    vec = pl.BlockSpec((1, d), lambda j, *_: (0, 0))
    grid_spec = pltpu.PrefetchScalarGridSpec(
        num_scalar_prefetch=2,
        grid=(n // tokens,),
        in_specs=[pl.BlockSpec(memory_space=pl.ANY),
                  pl.BlockSpec((tokens, d), lambda j, *_: (j, 0)),
                  pl.BlockSpec((1, 1, d), lambda j, *_: (j // blocks_per_sample, 0, 0)),
                  vec, vec],
        out_specs=pl.BlockSpec((tokens, d), lambda j, *_: (j, 0)),
        scratch_shapes=[pltpu.VMEM((2, tokens * TOP_K * SUBLANES, LANES), F32),
                        pltpu.VMEM((tokens * SUBLANES, LANES), F32),
                        pltpu.SemaphoreType.DMA((2,))],
    )
    return pl.pallas_call(
        _combine_kernel,
        grid_spec=grid_spec,
        out_shape=jax.ShapeDtypeStruct((n, d), F32),
        compiler_params=_params("arbitrary"),
        name="combine",
    )(slot, gates_flat, ys, x1, g2, ln_g.reshape(1, d), ln_b.reshape(1, d))


def _layer(x, c, ctx, c_ctx, ada_w, ada_b, w_in, rpb, sgu_ln_g, sgu_ln_b, sgu_w, sgu_b, w_out,
           ln1_g, ln1_b, ln2_g, ln2_b, router_w, router_b, w_gu, b_gu, w_down, b_down):
    b, s, d = x.shape
    n = b * s
    assert s % (GRID_W * ATT_ROWS) == 0 and s // GRID_W >= NA_KH

    cond_rows = jnp.zeros((8, d), F32).at[:b].set(c).at[b].set(c_ctx)
    mod = _ada(cond_rows, ada_w, ada_b)
    sh1, sc1, g1, sh2, sc2, g2 = jnp.split(mod[:b], 6, axis=-1)
    csh1, csc1 = mod[b, :d], mod[b, d:2 * d]

    w_in_bf = w_in.astype(BF16)
    k_c, v_c = _ctx_kv(ctx, csh1, csc1, w_in_bf[:, NA_WIDTH:3 * NA_WIDTH])
    q, k, v, sgu = _proj_in(x, sh1, sc1, w_in_bf, sgu_ln_g, sgu_ln_b, sgu_w, sgu_b)
    att = _natten(q, k, v, k_c, v_c, _bias_table(rpb))
    x1, t_tiles, logits_t = _mix_out(att, sgu, x, w_out.astype(BF16), g1, sh2, sc2, ln1_g, ln1_b,
                                     router_w, router_b)

    expert_k, rank_k, gate_k, counts = _route(logits_t)
    n_tiles_max = n * TOP_K // EXPERT_ROWS + N_EXPERTS
    slot, tile_expert, n_tiles, group_start, counts, padded = _plan(
        expert_k, rank_k, counts[:, 0], n_tiles_max)
    xs = _dispatch(slot, group_start, counts, padded, n_tiles, t_tiles, n_tiles_max * EXPERT_ROWS)
    ys = _experts(tile_expert, n_tiles, xs, w_gu, b_gu, w_down, b_down)
    out = _combine(slot, gate_k.T.reshape(-1), ys, x1.reshape(n, d), g2.reshape(b, 1, d),
                   ln2_g, ln2_b, s)
    return out.reshape(b, s, d)


def kernel(x, c, ctx, c_ctx, ada_w, ada_b, w_in, rpb, sgu_ln_g, sgu_ln_b, sgu_w, sgu_b, w_out,
           ln1_g, ln1_b, ln2_g, ln2_b, router_w, router_b, exp_w_gu, exp_b_gu, exp_w_down,
           exp_b_down):
    assert ada_w.shape[0] == DEPTH
    return _layer(x, c, ctx, c_ctx, ada_w[0], ada_b[0], w_in[0], rpb[0], sgu_ln_g[0], sgu_ln_b[0],
                  sgu_w[0], sgu_b[0], w_out[0], ln1_g[0], ln1_b[0], ln2_g[0], ln2_b[0],
                  router_w[0], router_b[0], exp_w_gu[0], exp_b_gu[0], exp_w_down[0], exp_b_down[0])
```

```python
import functools

import jax
import jax.numpy as jnp
from jax import lax
from jax.experimental import pallas as pl
from jax.experimental.pallas import tpu as pltpu

F32 = jnp.float32
BF16 = jnp.bfloat16

GRID_W = 64
NA_HEADS = 8
NA_HEAD_DIM = 64
NA_WIDTH = NA_HEADS * NA_HEAD_DIM
NA_KH = 8
NA_KW = 16
SGU_GROUPS = 4
SGU_GROUP_DIM = 128
SGU_WIDTH = SGU_GROUPS * SGU_GROUP_DIM
CHUNK = 128
N_EXPERTS = 32
TOP_K = 4
SWIGLU_LIMIT = 7.0
SWIGLU_ALPHA = 1.702
LN_EPS = 1e-5
DEPTH = 1
DEEPNORM_ALPHA = (2.0 * DEPTH) ** 0.25
MASKED = -1e30

SUBLANES = 8
LANES = 128
VMEM_LIMIT_BYTES = 52 * 1024 * 1024

ROW_TILE = 512
ATT_ROWS = 8
ATT_HEAD_GROUP = 4
ROUTE_CHUNK = 1024
SCAN_BLOCK = 256
EXPERT_ROWS = 256
DISPATCH_TOKENS = 1024
COMBINE_TOKENS = 256


def _params(*sem):
    return pltpu.CompilerParams(dimension_semantics=sem, vmem_limit_bytes=VMEM_LIMIT_BYTES)


def _normalize(x):
    mu = jnp.mean(x, axis=-1, keepdims=True)
    xc = x - mu
    var = jnp.mean(xc * xc, axis=-1, keepdims=True)
    return xc * lax.rsqrt(var + LN_EPS)


def _ada_kernel(c_ref, w_ref, b_ref, o_ref):
    s = c_ref[...]
    s = s * jax.nn.sigmoid(s)
    o_ref[...] = jnp.dot(s, w_ref[...], precision=lax.Precision.HIGHEST,
                         preferred_element_type=F32) + b_ref[...]


def _ada(cond_rows, ada_w, ada_b):
    d = cond_rows.shape[1]
    n_out = ada_w.shape[1]
    return pl.pallas_call(
        _ada_kernel,
        grid=(n_out // d,),
        in_specs=[pl.BlockSpec((8, d), lambda j: (0, 0)),
                  pl.BlockSpec((d, d), lambda j: (0, j)),
                  pl.BlockSpec((1, d), lambda j: (0, j))],
        out_specs=pl.BlockSpec((8, d), lambda j: (0, j)),
        out_shape=jax.ShapeDtypeStruct((8, n_out), F32),
        compiler_params=_params("arbitrary"),
        name="ada",
    )(cond_rows, ada_w, ada_b.reshape(1, n_out))


def _ctx_kv_kernel(ctx_ref, sh_ref, sc_ref, w_ref, k_ref, v_ref):
    h = _normalize(ctx_ref[0]) * (1.0 + sc_ref[...]) + sh_ref[...]
    kv = jnp.dot(h.astype(BF16), w_ref[...], preferred_element_type=F32)
    k_ref[0] = kv[:, :NA_WIDTH].astype(BF16)
    v_ref[0] = kv[:, NA_WIDTH:].astype(BF16)


def _ctx_kv(ctx, csh1, csc1, w_kv):
    b, l, d = ctx.shape
    out = jax.ShapeDtypeStruct((b, l, NA_WIDTH), BF16)
    return pl.pallas_call(
        _ctx_kv_kernel,
        grid=(b,),
        in_specs=[pl.BlockSpec((1, l, d), lambda i: (i, 0, 0)),
                  pl.BlockSpec((1, d), lambda i: (0, 0)),
                  pl.BlockSpec((1, d), lambda i: (0, 0)),
                  pl.BlockSpec((d, 2 * NA_WIDTH), lambda i: (0, 0))],
        out_specs=[pl.BlockSpec((1, l, NA_WIDTH), lambda i: (i, 0, 0)),
                   pl.BlockSpec((1, l, NA_WIDTH), lambda i: (i, 0, 0))],
        out_shape=[out, out],
        compiler_params=_params("arbitrary"),
        name="ctx_kv",
    )(ctx, csh1.reshape(1, d), csc1.reshape(1, d), w_kv)


def _proj_in_kernel(x_ref, sh_ref, sc_ref, w_ref, lng_ref, lnb_ref, ws_ref, bs_ref,
                    q_ref, k_ref, v_ref, s_ref):
    h = (_normalize(x_ref[0]) * (1.0 + sc_ref[0]) + sh_ref[0]).astype(BF16)

    def proj(lo, width):
        return jnp.dot(h, w_ref[:, lo:lo + width], preferred_element_type=F32)

    q_ref[0] = (proj(0, NA_WIDTH) * (NA_HEAD_DIM ** -0.5)).astype(BF16)
    k_ref[0] = proj(NA_WIDTH, NA_WIDTH).astype(BF16)
    v_ref[0] = proj(2 * NA_WIDTH, NA_WIDTH).astype(BF16)
    u = jax.nn.gelu(proj(3 * NA_WIDTH, SGU_WIDTH))
    g = jax.nn.gelu(proj(3 * NA_WIDTH + SGU_WIDTH, SGU_WIDTH))
    gn = (_normalize(g) * lng_ref[...] + lnb_ref[...]).astype(BF16)
    rows = h.shape[0]
    for n in range(rows // CHUNK):
        r0 = n * CHUNK
        for grp in range(SGU_GROUPS):
            c0 = grp * SGU_GROUP_DIM
            mixed = jnp.dot(ws_ref[grp], gn[r0:r0 + CHUNK, c0:c0 + SGU_GROUP_DIM],
                            preferred_element_type=F32) + bs_ref[grp]
            s_ref[0, r0:r0 + CHUNK, c0:c0 + SGU_GROUP_DIM] = (
                u[r0:r0 + CHUNK, c0:c0 + SGU_GROUP_DIM] * mixed).astype(BF16)


def _proj_in(x, sh1, sc1, w_in, sgu_ln_g, sgu_ln_b, sgu_w, sgu_b):
    b, s, d = x.shape
    d_in = w_in.shape[1]
    tm = min(ROW_TILE, s)
    out = jax.ShapeDtypeStruct((b, s, NA_WIDTH), BF16)
    row_spec = pl.BlockSpec((1, tm, NA_WIDTH), lambda i, j: (i, j, 0))
    mod_spec = pl.BlockSpec((1, 1, d), lambda i, j: (i, 0, 0))
    bs = jnp.broadcast_to(sgu_b[:, :, None], (SGU_GROUPS, CHUNK, SGU_GROUP_DIM))
    return pl.pallas_call(
        _proj_in_kernel,
        grid=(b, s // tm),
        in_specs=[pl.BlockSpec((1, tm, d), lambda i, j: (i, j, 0)),
                  mod_spec, mod_spec,
                  pl.BlockSpec((d, d_in), lambda i, j: (0, 0)),
                  pl.BlockSpec((1, SGU_WIDTH), lambda i, j: (0, 0)),
                  pl.BlockSpec((1, SGU_WIDTH), lambda i, j: (0, 0)),
                  pl.BlockSpec((SGU_GROUPS, CHUNK, CHUNK), lambda i, j: (0, 0, 0)),
                  pl.BlockSpec((SGU_GROUPS, CHUNK, SGU_GROUP_DIM), lambda i, j: (0, 0, 0))],
        out_specs=[row_spec, row_spec, row_spec, row_spec],
        out_shape=[out, out, out, out],
        compiler_params=_params("parallel", "parallel"),
        name="proj_in",
    )(x, sh1.reshape(b, 1, d), sc1.reshape(b, 1, d), w_in,
      sgu_ln_g.reshape(1, SGU_WIDTH), sgu_ln_b.reshape(1, SGU_WIDTH), sgu_w.astype(BF16), bs)


def _bias_table(rpb):
    heads, n_dr, n_dc = rpb.shape
    c = jnp.arange(GRID_W)
    cs = jnp.clip(c - NA_KW // 2, 0, GRID_W - NA_KW)
    kc = jnp.arange(GRID_W)
    valid = (kc[None, :] >= cs[:, None]) & (kc[None, :] < cs[:, None] + NA_KW)
    lead = GRID_W - NA_KW
    padded = jnp.pad(rpb.astype(F32), ((0, 0), (0, 0), (lead, 2 * GRID_W - lead - n_dc)))
    flat = jnp.tile(padded, (1, 1, GRID_W))[:, :, :GRID_W * (2 * GRID_W - 1)]
    toe = flat.reshape(heads, n_dr, GRID_W, 2 * GRID_W - 1)[..., GRID_W - 1:]
    toe = jnp.where(valid, toe, MASKED)
    tab = jnp.stack([toe[:, NA_KH - 1 - o:2 * NA_KH - 1 - o] for o in range(NA_KH)])
    tab = tab.transpose(0, 1, 3, 2, 4)
    return tab.reshape(NA_KH, NA_HEADS * GRID_W, NA_KH * GRID_W)


def _natten_kernel(q_ref, k_ref, v_ref, kc_ref, vc_ref, bias_ref, o_ref, *, grid_rows):
    group_lanes = ATT_HEAD_GROUP * NA_HEAD_DIM
    group_rows = ATT_HEAD_GROUP * GRID_W
    row_head = lax.broadcasted_iota(jnp.int32, (group_rows, group_lanes), 0) // GRID_W
    lane_head = lax.broadcasted_iota(jnp.int32, (group_rows, group_lanes), 1) // NA_HEAD_DIM
    own_head = row_head == lane_head
    nt = (((1,), (1,)), ((), ()))

    def one_row(i, carry):
        r = pl.program_id(1) * ATT_ROWS + i
        rs = jnp.clip(r - NA_KH // 2, 0, grid_rows - NA_KH)
        k0 = pl.multiple_of(rs * GRID_W, GRID_W)
        q0 = pl.multiple_of(i * GRID_W, GRID_W)
        for grp in range(NA_HEADS // ATT_HEAD_GROUP):
            lanes = slice(grp * group_lanes, (grp + 1) * group_lanes)
            q = q_ref[0, pl.ds(q0, GRID_W), lanes]
            qs = jnp.where(own_head, jnp.concatenate([q] * ATT_HEAD_GROUP, axis=0),
                           jnp.zeros((), BF16))
            kr = k_ref[0, pl.ds(k0, NA_KH * GRID_W), lanes]
            vr = v_ref[0, pl.ds(k0, NA_KH * GRID_W), lanes]
            bias = bias_ref[r - rs, grp * group_rows:(grp + 1) * group_rows, :]
            s_nb = lax.dot_general(qs, kr, nt, preferred_element_type=F32) + bias
            s_cx = lax.dot_general(qs, kc_ref[0, :, lanes], nt, preferred_element_type=F32)
            m = jnp.maximum(jnp.max(s_nb, axis=-1, keepdims=True),
                            jnp.max(s_cx, axis=-1, keepdims=True))
            p_nb = jnp.exp(s_nb - m)
            p_cx = jnp.exp(s_cx - m)
            denom = jnp.sum(p_nb, axis=-1, keepdims=True) + jnp.sum(p_cx, axis=-1, keepdims=True)
            o = (jnp.dot(p_nb.astype(BF16), vr, preferred_element_type=F32)
                 + jnp.dot(p_cx.astype(BF16), vc_ref[0, :, lanes],
                           preferred_element_type=F32)) / denom
            o = jnp.where(own_head, o, 0.0)
            out = o[:GRID_W]
            for h in range(1, ATT_HEAD_GROUP):
                out = out + o[h * GRID_W:(h + 1) * GRID_W]
            o_ref[0, pl.ds(q0, GRID_W), lanes] = out.astype(BF16)
        return carry

    lax.fori_loop(0, ATT_ROWS, one_row, 0, unroll=True)


def _natten(q, k, v, k_c, v_c, bias):
    b, s, w = q.shape
    l = k_c.shape[1]
    grid_rows = s // GRID_W
    tq = ATT_ROWS * GRID_W
    full = pl.BlockSpec((1, s, w), lambda i, j: (i, 0, 0), pipeline_mode=pl.Buffered(1))
    ctx = pl.BlockSpec((1, l, w), lambda i, j: (i, 0, 0))
    return pl.pallas_call(
        functools.partial(_natten_kernel, grid_rows=grid_rows),
        grid=(b, grid_rows // ATT_ROWS),
        in_specs=[pl.BlockSpec((1, tq, w), lambda i, j: (i, j, 0)),
                  full, full, ctx, ctx,
                  pl.BlockSpec(bias.shape, lambda i, j: (0, 0, 0), pipeline_mode=pl.Buffered(1))],
        out_specs=pl.BlockSpec((1, tq, w), lambda i, j: (i, j, 0)),
        out_shape=jax.ShapeDtypeStruct((b, s, w), BF16),
        compiler_params=_params("parallel", "arbitrary"),
        name="natten",
    )(q, k, v, k_c, v_c, bias)


def _mix_out_kernel(att_ref, sgu_ref, x_ref, wo_ref, g1_ref, sh_ref, sc_ref, lng_ref, lnb_ref,
                    wr_ref, br_ref, x1_ref, t_ref, lg_ref):
    mix = (jnp.dot(att_ref[0], wo_ref[:NA_WIDTH, :], preferred_element_type=F32)
           + jnp.dot(sgu_ref[0], wo_ref[NA_WIDTH:, :], preferred_element_type=F32))
    x1 = _normalize(DEEPNORM_ALPHA * x_ref[0] + g1_ref[0] * mix) * lng_ref[...] + lnb_ref[...]
    x1_ref[0] = x1
    t = _normalize(x1) * (1.0 + sc_ref[0]) + sh_ref[0]
    rows = t.shape[0]
    for s in range(t.shape[1] // LANES):
        t_ref[pl.ds(s, rows, stride=SUBLANES), :] = t[:, s * LANES:(s + 1) * LANES]
    lg_ref[...] = lax.dot_general(wr_ref[...], t, (((1,), (1,)), ((), ())),
                                  precision=lax.Precision.HIGHEST,
                                  preferred_element_type=F32) + br_ref[...]


def _mix_out(att, sgu, x, w_out, g1, sh2, sc2, ln_g, ln_b, router_w, router_b):
    b, s, d = x.shape
    assert d == SUBLANES * LANES
    tm = min(ROW_TILE, s)
    nj = s // tm
    row = lambda width: pl.BlockSpec((1, tm, width), lambda i, j: (i, j, 0))
    mod_spec = pl.BlockSpec((1, 1, d), lambda i, j: (i, 0, 0))
    vec_spec = pl.BlockSpec((1, d), lambda i, j: (0, 0))
    return pl.pallas_call(
        _mix_out_kernel,
        grid=(b, nj),
        in_specs=[row(NA_WIDTH), row(SGU_WIDTH), row(d),
                  pl.BlockSpec((d, d), lambda i, j: (0, 0)),
                  mod_spec, mod_spec, mod_spec, vec_spec, vec_spec,
                  pl.BlockSpec((N_EXPERTS, d), lambda i, j: (0, 0)),
                  pl.BlockSpec((N_EXPERTS, 1), lambda i, j: (0, 0))],
        out_specs=[row(d),
                   pl.BlockSpec((tm * SUBLANES, LANES), lambda i, j: (i * nj + j, 0)),
                   pl.BlockSpec((N_EXPERTS, tm), lambda i, j: (0, i * nj + j))],
        out_shape=[jax.ShapeDtypeStruct((b, s, d), F32),
                   jax.ShapeDtypeStruct((b * s * SUBLANES, LANES), F32),
                   jax.ShapeDtypeStruct((N_EXPERTS, b * s), F32)],
        compiler_params=_params("parallel", "parallel"),
        name="mix_out",
    )(att, sgu, x, w_out, g1.reshape(b, 1, d), sh2.reshape(b, 1, d), sc2.reshape(b, 1, d),
      ln_g.reshape(1, d), ln_b.reshape(1, d), router_w.T, router_b.reshape(N_EXPERTS, 1))


def _route_kernel(lg_ref, e_ref, r_ref, g_ref, cnt_ref, run_ref):
    @pl.when(pl.program_id(0) == 0)
    def _():
        run_ref[...] = jnp.zeros_like(run_ref)

    logits = lg_ref[...]
    n_tok = logits.shape[1]
    expert = lax.broadcasted_iota(jnp.int32, logits.shape, 0)
    work = logits
    picks, tops = [], []
    for kk in range(TOP_K):
        m = jnp.max(work, axis=0, keepdims=True)
        first = jnp.min(jnp.where(work == m, expert, N_EXPERTS), axis=0, keepdims=True)
        pick = expert == first
        work = jnp.where(pick, -jnp.inf, work)
        picks.append(pick)
        tops.append(m)
        e_ref[kk:kk + 1, :] = first
    weights = [jnp.exp(m - tops[0]) for m in tops]
    denom = weights[0]
    for w in weights[1:]:
        denom = denom + w
    for kk in range(TOP_K):
        g_ref[kk:kk + 1, :] = weights[kk] / denom

    chosen = picks[0]
    for pick in picks[1:]:
        chosen = jnp.logical_or(chosen, pick)
    tri = (lax.broadcasted_iota(jnp.int32, (SCAN_BLOCK, SCAN_BLOCK), 0)
           <= lax.broadcasted_iota(jnp.int32, (SCAN_BLOCK, SCAN_BLOCK), 1)).astype(BF16)
    sel = jnp.where(chosen, 1.0, 0.0).astype(BF16)
    carry = run_ref[:, 0:1]
    for blk in range(n_tok // SCAN_BLOCK):
        lo = blk * SCAN_BLOCK
        run = jnp.dot(sel[:, lo:lo + SCAN_BLOCK], tri, preferred_element_type=F32) + carry
        for kk in range(TOP_K):
            rank = jnp.sum(jnp.where(picks[kk][:, lo:lo + SCAN_BLOCK], run - 1.0, 0.0),
                           axis=0, keepdims=True)
            r_ref[kk:kk + 1, lo:lo + SCAN_BLOCK] = rank.astype(jnp.int32)
        carry = run[:, SCAN_BLOCK - 1:SCAN_BLOCK]
    run_ref[...] = jnp.broadcast_to(carry, run_ref.shape)
    cnt_ref[...] = jnp.broadcast_to(carry, cnt_ref.shape).astype(jnp.int32)


def _route(logits_t):
    n = logits_t.shape[1]
    chunk = min(ROUTE_CHUNK, n)
    assert n % chunk == 0 and chunk % SCAN_BLOCK == 0
    per_k = pl.BlockSpec((TOP_K, chunk), lambda c: (0, c))
    return pl.pallas_call(
        _route_kernel,
        grid=(n // chunk,),
        in_specs=[pl.BlockSpec((N_EXPERTS, chunk), lambda c: (0, c))],
        out_specs=[per_k, per_k, per_k, pl.BlockSpec((N_EXPERTS, LANES), lambda c: (0, 0))],
        out_shape=[jax.ShapeDtypeStruct((TOP_K, n), jnp.int32),
                   jax.ShapeDtypeStruct((TOP_K, n), jnp.int32),
                   jax.ShapeDtypeStruct((TOP_K, n), F32),
                   jax.ShapeDtypeStruct((N_EXPERTS, LANES), jnp.int32)],
        scratch_shapes=[pltpu.VMEM((N_EXPERTS, LANES), F32)],
        compiler_params=_params("arbitrary"),
        name="route",
    )(logits_t)


def _plan(expert_k, rank_k, counts):
    padded = (counts + EXPERT_ROWS - 1) // EXPERT_ROWS * EXPERT_ROWS
    group_end = jnp.cumsum(padded)
    group_start = group_end - padded
    start_k = jnp.sum(jnp.where(expert_k[None] == jnp.arange(N_EXPERTS)[:, None, None],
                                group_start[:, None, None], 0), axis=0)
    slot = (start_k + rank_k).T.reshape(-1)
    tile_start = jnp.concatenate([group_start, group_end[-1:]]) // EXPERT_ROWS
    return (slot.astype(jnp.int32), tile_start.astype(jnp.int32),
            group_start.astype(jnp.int32), counts.astype(jnp.int32), padded.astype(jnp.int32))


def _dispatch_kernel(slot_ref, start_ref, cnt_ref, pad_ref, nt_ref, t_ref, xs_ref, zero_ref,
                     sem, zsem):
    tokens = t_ref.shape[0] // SUBLANES
    base = pl.program_id(0) * tokens
    tile_rows = zero_ref.shape[0]
    n_tiles_max = xs_ref.shape[0] // tile_rows

    def row_copy(src, row, sem_):
        dst = xs_ref.at[pl.ds(pl.multiple_of(row * SUBLANES, SUBLANES), SUBLANES), :]
        return pltpu.make_async_copy(src, dst, sem_)

    def tile_copy(tile):
        dst = xs_ref.at[pl.ds(pl.multiple_of(tile * tile_rows, tile_rows), tile_rows), :]
        return pltpu.make_async_copy(zero_ref, dst, zsem)

    @pl.when(pl.program_id(0) == 0)
    def _():
        zero_ref[...] = jnp.zeros_like(zero_ref)
        zero_row = zero_ref.at[pl.ds(0, SUBLANES), :]
        for e in range(N_EXPERTS):
            first = start_ref[e] + cnt_ref[e]
            n_pad = pad_ref[e] - cnt_ref[e]

            def fill(i, carry):
                row_copy(zero_row, first + i, zsem).start()
                return carry

            def drain(i, carry):
                row_copy(zero_row, first, zsem).wait()
                return carry

            lax.fori_loop(0, n_pad, fill, 0)
            lax.fori_loop(0, n_pad, drain, 0)

        def fill_tile(i, carry):
            tile_copy(i).start()
            return carry

        def drain_tile(i, carry):
            tile_copy(i).wait()
            return carry

        lax.fori_loop(nt_ref[0], n_tiles_max, fill_tile, 0)
        lax.fori_loop(nt_ref[0], n_tiles_max, drain_tile, 0)

    def scatter(i, carry):
        src = t_ref.at[pl.ds(pl.multiple_of(i * SUBLANES, SUBLANES), SUBLANES), :]
        for kk in range(TOP_K):
            row_copy(src, slot_ref[(base + i) * TOP_K + kk], sem).start(priority=kk % 2)
        return carry

    lax.fori_loop(0, tokens, scatter, 0, unroll=8)
    for kk in range(TOP_K):
        pltpu.make_async_copy(t_ref, xs_ref.at[pl.ds(0, tokens * SUBLANES), :], sem).wait()


def _dispatch(slot, group_start, counts, padded, n_tiles, t_tiles, n_rows):
    n = t_tiles.shape[0] // SUBLANES
    tokens = min(DISPATCH_TOKENS, n)
    assert n % tokens == 0 and n_rows % EXPERT_ROWS == 0
    grid_spec = pltpu.PrefetchScalarGridSpec(
        num_scalar_prefetch=5,
        grid=(n // tokens,),
        in_specs=[pl.BlockSpec((tokens * SUBLANES, LANES), lambda j, *_: (j, 0))],
        out_specs=pl.BlockSpec(memory_space=pl.ANY),
        scratch_shapes=[pltpu.VMEM((EXPERT_ROWS * SUBLANES, LANES), F32),
                        pltpu.SemaphoreType.DMA, pltpu.SemaphoreType.DMA],
    )
    return pl.pallas_call(
        _dispatch_kernel,
        grid_spec=grid_spec,
        out_shape=jax.ShapeDtypeStruct((n_rows * SUBLANES, LANES), F32),
        compiler_params=pltpu.CompilerParams(dimension_semantics=("arbitrary",),
                                             vmem_limit_bytes=VMEM_LIMIT_BYTES,
                                             has_side_effects=True),
        name="dispatch",
    )(slot, group_start, counts, padded, n_tiles, t_tiles)


def _experts_kernel(ts_ref, xs_ref, wgu_ref, bgu_ref, wd_ref, bd_ref, ys_ref,
                    wgu_bf, wd_bf, xbuf, ybuf, xsem, ysem):
    e = pl.program_id(0)
    tile_rows = xbuf.shape[1]
    rows = tile_rows // SUBLANES
    d_ff = wd_ref.shape[1]
    n_blocks = wgu_ref.shape[1] // LANES
    n_total = ts_ref[N_EXPERTS]

    def hbm_tile(ref, t):
        return ref.at[pl.ds(pl.multiple_of(t * tile_rows, tile_rows), tile_rows), :]

    def x_copy(t, which):
        return pltpu.make_async_copy(hbm_tile(xs_ref, t), xbuf.at[which], xsem.at[which])

    def y_copy(t, which):
        return pltpu.make_async_copy(ybuf.at[which], hbm_tile(ys_ref, t), ysem.at[which])

    @pl.when(e == 0)
    def _():
        x_copy(0, 0).start()

    @pl.when(ts_ref[e] < ts_ref[e + 1])
    def _():
        wgu_bf[...] = wgu_ref[0].astype(BF16)
        wd_bf[...] = wd_ref[0].astype(BF16)

        def one_tile(t, carry):
            cur = t % 2

            @pl.when(t + 1 < n_total)
            def _():
                x_copy(t + 1, 1 - cur).start()

            x_copy(t, cur).wait()

            @pl.when(t >= 2)
            def _():
                y_copy(t - 2, cur).wait()

            x = jnp.concatenate(
                [xbuf[cur, pl.ds(s, rows, stride=SUBLANES), :] for s in range(n_blocks)],
                axis=1).astype(BF16)
            gu = jnp.dot(x, wgu_bf[...], preferred_element_type=F32) + bgu_ref[0]
            gate = jnp.minimum(gu[:, :d_ff], SWIGLU_LIMIT)
            lin = jnp.clip(gu[:, d_ff:], -SWIGLU_LIMIT, SWIGLU_LIMIT)
            act = ((lin + 1.0) * (gate * jax.nn.sigmoid(SWIGLU_ALPHA * gate))).astype(BF16)
            y = jnp.dot(act, wd_bf[...], preferred_element_type=F32) + bd_ref[0]
            for s in range(n_blocks):
                ybuf[cur, pl.ds(s, rows, stride=SUBLANES), :] = y[:, s * LANES:(s + 1) * LANES]
            y_copy(t, cur).start()
            return carry

        lax.fori_loop(ts_ref[e], ts_ref[e + 1], one_tile, 0)

    @pl.when(e == pl.num_programs(0) - 1)
    def _():
        y_copy(n_total - 1, (n_total - 1) % 2).wait()

        @pl.when(n_total >= 2)
        def _():
            y_copy(n_total - 2, n_total % 2).wait()


def _experts(tile_start, xs, w_gu, b_gu, w_down, b_down):
    n_exp, d, two_ff = w_gu.shape
    d_ff = two_ff // 2
    by_expert = lambda e, ts: (e, 0, 0)
    tile_rows = EXPERT_ROWS * SUBLANES
    grid_spec = pltpu.PrefetchScalarGridSpec(
        num_scalar_prefetch=1,
        grid=(n_exp,),
        in_specs=[pl.BlockSpec(memory_space=pl.ANY),
                  pl.BlockSpec((1, d, two_ff), by_expert),
                  pl.BlockSpec((1, 1, two_ff), by_expert),
                  pl.BlockSpec((1, d_ff, d), by_expert),
                  pl.BlockSpec((1, 1, d), by_expert)],
        out_specs=pl.BlockSpec(memory_space=pl.ANY),
        scratch_shapes=[pltpu.VMEM((d, two_ff), BF16), pltpu.VMEM((d_ff, d), BF16),
                        pltpu.VMEM((2, tile_rows, LANES), F32),
                        pltpu.VMEM((2, tile_rows, LANES), F32),
                        pltpu.SemaphoreType.DMA((2,)), pltpu.SemaphoreType.DMA((2,))],
    )
    return pl.pallas_call(
        _experts_kernel,
        grid_spec=grid_spec,
        out_shape=jax.ShapeDtypeStruct(xs.shape, F32),
        input_output_aliases={1: 0},
        compiler_params=_params("arbitrary"),
        name="experts",
    )(tile_start, xs, w_gu, b_gu.reshape(n_exp, 1, two_ff), w_down, b_down.reshape(n_exp, 1, d))


def _combine_kernel(slot_ref, gate_ref, ys_ref, x1_ref, g2_ref, lng_ref, lnb_ref, o_ref,
                    buf, acc_ref, sem):
    j = pl.program_id(0)
    tokens = o_ref.shape[0]
    n_blocks = o_ref.shape[1] // LANES
    tile_rows = TOP_K * SUBLANES
    cur = j % 2

    def gather_token(block, which, i):
        for kk in range(TOP_K):
            row = slot_ref[(block * tokens + i) * TOP_K + kk]
            src = ys_ref.at[pl.ds(pl.multiple_of(row * SUBLANES, SUBLANES), SUBLANES), :]
            dst = buf.at[which, pl.ds(pl.multiple_of((i * TOP_K + kk) * SUBLANES, SUBLANES),
                                      SUBLANES), :]
            pltpu.make_async_copy(src, dst, sem.at[which]).start(priority=kk % 2)

    def reduce_token(i):
        tile = None
        for kk in range(TOP_K):
            row0 = pl.multiple_of((i * TOP_K + kk) * SUBLANES, SUBLANES)
            part = gate_ref[(j * tokens + i) * TOP_K + kk] * buf[cur, pl.ds(row0, SUBLANES), :]
            tile = part if tile is None else tile + part
        acc_ref[pl.ds(pl.multiple_of(i * SUBLANES, SUBLANES), SUBLANES), :] = tile

    @pl.when(j == 0)
    def _():
        def first(i, carry):
            gather_token(0, 0, i)
            return carry
        lax.fori_loop(0, tokens, first, 0, unroll=8)

    pltpu.make_async_copy(ys_ref.at[pl.ds(0, tokens * tile_rows), :], buf.at[cur],
                          sem.at[cur]).wait()

    @pl.when(j + 1 < pl.num_programs(0))
    def _():
        def step(i, carry):
            gather_token(j + 1, 1 - cur, i)
            reduce_token(i)
            return carry
        lax.fori_loop(0, tokens, step, 0, unroll=8)

    @pl.when(j + 1 == pl.num_programs(0))
    def _():
        def step(i, carry):
            reduce_token(i)
            return carry
        lax.fori_loop(0, tokens, step, 0, unroll=8)

    ffn = jnp.concatenate([acc_ref[pl.ds(s, tokens, stride=SUBLANES), :] for s in range(n_blocks)],
                          axis=1)
    z = DEEPNORM_ALPHA * x1_ref[...] + g2_ref[0] * ffn
    o_ref[...] = _normalize(z) * lng_ref[...] + lnb_ref[...]


def _combine(slot, gates_flat, ys, x1, g2, ln_g, ln_b, tokens_per_sample):
    n, d = x1.shape
    tokens = min(COMBINE_TOKENS, tokens_per_sample)
    assert tokens_per_sample % tokens == 0
    blocks_per_sample = tokens_per_sample // tokens
    vec = pl.BlockSpec((1, d), lambda j, *_: (0, 0))
    grid_spec = pltpu.PrefetchScalarGridSpec(
        num_scalar_prefetch=2,
        grid=(n // tokens,),
        in_specs=[pl.BlockSpec(memory_space=pl.ANY),
                  pl.BlockSpec((tokens, d), lambda j, *_: (j, 0)),
                  pl.BlockSpec((1, 1, d), lambda j, *_: (j // blocks_per_sample, 0, 0)),
                  vec, vec],
        out_specs=pl.BlockSpec((tokens, d), lambda j, *_: (j, 0)),
        scratch_shapes=[pltpu.VMEM((2, tokens * TOP_K * SUBLANES, LANES), F32),
                        pltpu.VMEM((tokens * SUBLANES, LANES), F32),
                        pltpu.SemaphoreType.DMA((2,))],
    )
    return pl.pallas_call(
        _combine_kernel,
        grid_spec=grid_spec,
        out_shape=jax.ShapeDtypeStruct((n, d), F32),
        compiler_params=_params("arbitrary"),
        name="combine",
    )(slot, gates_flat, ys, x1, g2, ln_g.reshape(1, d), ln_b.reshape(1, d))


def _layer(x, c, ctx, c_ctx, ada_w, ada_b, w_in, rpb, sgu_ln_g, sgu_ln_b, sgu_w, sgu_b, w_out,
           ln1_g, ln1_b, ln2_g, ln2_b, router_w, router_b, w_gu, b_gu, w_down, b_down):
    b, s, d = x.shape
    n = b * s
    assert s % (GRID_W * ATT_ROWS) == 0 and s // GRID_W >= NA_KH

    cond_rows = jnp.zeros((8, d), F32).at[:b].set(c).at[b].set(c_ctx)
    mod = _ada(cond_rows, ada_w, ada_b)
    sh1, sc1, g1, sh2, sc2, g2 = jnp.split(mod[:b], 6, axis=-1)
    csh1, csc1 = mod[b, :d], mod[b, d:2 * d]

    w_in_bf = w_in.astype(BF16)
    k_c, v_c = _ctx_kv(ctx, csh1, csc1, w_in_bf[:, NA_WIDTH:3 * NA_WIDTH])
    q, k, v, sgu = _proj_in(x, sh1, sc1, w_in_bf, sgu_ln_g, sgu_ln_b, sgu_w, sgu_b)
    att = _natten(q, k, v, k_c, v_c, _bias_table(rpb))
    x1, t_tiles, logits_t = _mix_out(att, sgu, x, w_out.astype(BF16), g1, sh2, sc2, ln1_g, ln1_b,
                                     router_w, router_b)

    expert_k, rank_k, gate_k, counts = _route(logits_t)
    n_tiles_max = n * TOP_K // EXPERT_ROWS + N_EXPERTS
    slot, tile_start, group_start, counts, padded = _plan(expert_k, rank_k, counts[:, 0])
    xs = _dispatch(slot, group_start, counts, padded, tile_start[N_EXPERTS:], t_tiles,
                   n_tiles_max * EXPERT_ROWS)
    ys = _experts(tile_start, xs, w_gu, b_gu, w_down, b_down)
    out = _combine(slot, gate_k.T.reshape(-1), ys, x1.reshape(n, d), g2.reshape(b, 1, d),
                   ln2_g, ln2_b, s)
    return out.reshape(b, s, d)


def kernel(x, c, ctx, c_ctx, ada_w, ada_b, w_in, rpb, sgu_ln_g, sgu_ln_b, sgu_w, sgu_b, w_out,
           ln1_g, ln1_b, ln2_g, ln2_b, router_w, router_b, exp_w_gu, exp_b_gu, exp_w_down,
           exp_b_down):
    assert ada_w.shape[0] == DEPTH
    return _layer(x, c, ctx, c_ctx, ada_w[0], ada_b[0], w_in[0], rpb[0], sgu_ln_g[0], sgu_ln_b[0],
                  sgu_w[0], sgu_b[0], w_out[0], ln1_g[0], ln1_b[0], ln2_g[0], ln2_b[0],
                  router_w[0], router_b[0], exp_w_gu[0], exp_b_gu[0], exp_w_down[0], exp_b_down[0])
```

```python
import functools

import jax
import jax.numpy as jnp
from jax import lax
from jax.experimental import pallas as pl
from jax.experimental.pallas import tpu as pltpu

F32 = jnp.float32
BF16 = jnp.bfloat16

GRID_W = 64
NA_HEADS = 8
NA_HEAD_DIM = 64
NA_WIDTH = NA_HEADS * NA_HEAD_DIM
NA_KH = 8
NA_KW = 16
SGU_GROUPS = 4
SGU_GROUP_DIM = 128
SGU_WIDTH = SGU_GROUPS * SGU_GROUP_DIM
CHUNK = 128
N_EXPERTS = 32
TOP_K = 4
SWIGLU_LIMIT = 7.0
SWIGLU_ALPHA = 1.702
LN_EPS = 1e-5
DEPTH = 1
DEEPNORM_ALPHA = (2.0 * DEPTH) ** 0.25
MASKED = -1e30

SUBLANES = 8
LANES = 128
MXU_COLS = 256
VMEM_LIMIT_BYTES = 52 * 1024 * 1024

ROW_TILE = 512
ATT_ROWS = 8
ATT_HEAD_GROUP = 4
ROUTE_CHUNK = 1024
SCAN_BLOCK = 256
EXPERT_ROWS = 256
COMBINE_TOKENS = 256


def _params(*sem):
    return pltpu.CompilerParams(dimension_semantics=sem, vmem_limit_bytes=VMEM_LIMIT_BYTES)


def _normalize(x):
    mu = jnp.mean(x, axis=-1, keepdims=True)
    xc = x - mu
    var = jnp.mean(xc * xc, axis=-1, keepdims=True)
    return xc * lax.rsqrt(var + LN_EPS)


def _ada_kernel(c_ref, w_ref, b_ref, o_ref):
    s = c_ref[...]
    s = s * jax.nn.sigmoid(s)
    o_ref[...] = jnp.dot(s, w_ref[...], precision=lax.Precision.HIGHEST,
                         preferred_element_type=F32) + b_ref[...]


def _ada(cond_rows, ada_w, ada_b):
    d = cond_rows.shape[1]
    n_out = ada_w.shape[1]
    return pl.pallas_call(
        _ada_kernel,
        grid=(n_out // d,),
        in_specs=[pl.BlockSpec((8, d), lambda j: (0, 0)),
                  pl.BlockSpec((d, d), lambda j: (0, j)),
                  pl.BlockSpec((1, d), lambda j: (0, j))],
        out_specs=pl.BlockSpec((8, d), lambda j: (0, j)),
        out_shape=jax.ShapeDtypeStruct((8, n_out), F32),
        compiler_params=_params("arbitrary"),
        name="ada",
    )(cond_rows, ada_w, ada_b.reshape(1, n_out))


def _ctx_kv_kernel(ctx_ref, sh_ref, sc_ref, w_ref, k_ref, v_ref):
    h = _normalize(ctx_ref[0]) * (1.0 + sc_ref[...]) + sh_ref[...]
    kv = jnp.dot(h.astype(BF16), w_ref[...], preferred_element_type=F32)
    k_ref[0] = kv[:, :NA_WIDTH].astype(BF16)
    v_ref[0] = kv[:, NA_WIDTH:].astype(BF16)


def _ctx_kv(ctx, csh1, csc1, w_kv):
    b, l, d = ctx.shape
    out = jax.ShapeDtypeStruct((b, l, NA_WIDTH), BF16)
    return pl.pallas_call(
        _ctx_kv_kernel,
        grid=(b,),
        in_specs=[pl.BlockSpec((1, l, d), lambda i: (i, 0, 0)),
                  pl.BlockSpec((1, d), lambda i: (0, 0)),
                  pl.BlockSpec((1, d), lambda i: (0, 0)),
                  pl.BlockSpec((d, 2 * NA_WIDTH), lambda i: (0, 0))],
        out_specs=[pl.BlockSpec((1, l, NA_WIDTH), lambda i: (i, 0, 0)),
                   pl.BlockSpec((1, l, NA_WIDTH), lambda i: (i, 0, 0))],
        out_shape=[out, out],
        compiler_params=_params("arbitrary"),
        name="ctx_kv",
    )(ctx, csh1.reshape(1, d), csc1.reshape(1, d), w_kv)


def _proj_in_kernel(x_ref, sh_ref, sc_ref, w_ref, lng_ref, lnb_ref, ws_ref, bs_ref,
                    q_ref, k_ref, v_ref, s_ref):
    h = (_normalize(x_ref[0]) * (1.0 + sc_ref[0]) + sh_ref[0]).astype(BF16)

    def proj(lo, width):
        return jnp.dot(h, w_ref[:, lo:lo + width], preferred_element_type=F32)

    q_ref[0] = (proj(0, NA_WIDTH) * (NA_HEAD_DIM ** -0.5)).astype(BF16)
    k_ref[0] = proj(NA_WIDTH, NA_WIDTH).astype(BF16)
    v_ref[0] = proj(2 * NA_WIDTH, NA_WIDTH).astype(BF16)
    u = jax.nn.gelu(proj(3 * NA_WIDTH, SGU_WIDTH))
    g = jax.nn.gelu(proj(3 * NA_WIDTH + SGU_WIDTH, SGU_WIDTH))
    gn = (_normalize(g) * lng_ref[...] + lnb_ref[...]).astype(BF16)
    rows = h.shape[0]
    for n in range(rows // CHUNK):
        r0 = n * CHUNK
        for grp in range(SGU_GROUPS):
            c0 = grp * SGU_GROUP_DIM
            mixed = jnp.dot(ws_ref[grp], gn[r0:r0 + CHUNK, c0:c0 + SGU_GROUP_DIM],
                            preferred_element_type=F32) + bs_ref[grp]
            s_ref[0, r0:r0 + CHUNK, c0:c0 + SGU_GROUP_DIM] = (
                u[r0:r0 + CHUNK, c0:c0 + SGU_GROUP_DIM] * mixed).astype(BF16)


def _proj_in(x, sh1, sc1, w_in, sgu_ln_g, sgu_ln_b, sgu_w, sgu_b):
    b, s, d = x.shape
    d_in = w_in.shape[1]
    tm = min(ROW_TILE, s)
    out = jax.ShapeDtypeStruct((b, s, NA_WIDTH), BF16)
    row_spec = pl.BlockSpec((1, tm, NA_WIDTH), lambda i, j: (i, j, 0))
    mod_spec = pl.BlockSpec((1, 1, d), lambda i, j: (i, 0, 0))
    bs = jnp.broadcast_to(sgu_b[:, :, None], (SGU_GROUPS, CHUNK, SGU_GROUP_DIM))
    return pl.pallas_call(
        _proj_in_kernel,
        grid=(b, s // tm),
        in_specs=[pl.BlockSpec((1, tm, d), lambda i, j: (i, j, 0)),
                  mod_spec, mod_spec,
                  pl.BlockSpec((d, d_in), lambda i, j: (0, 0)),
                  pl.BlockSpec((1, SGU_WIDTH), lambda i, j: (0, 0)),
                  pl.BlockSpec((1, SGU_WIDTH), lambda i, j: (0, 0)),
                  pl.BlockSpec((SGU_GROUPS, CHUNK, CHUNK), lambda i, j: (0, 0, 0)),
                  pl.BlockSpec((SGU_GROUPS, CHUNK, SGU_GROUP_DIM), lambda i, j: (0, 0, 0))],
        out_specs=[row_spec, row_spec, row_spec, row_spec],
        out_shape=[out, out, out, out],
        compiler_params=_params("parallel", "parallel"),
        name="proj_in",
    )(x, sh1.reshape(b, 1, d), sc1.reshape(b, 1, d), w_in,
      sgu_ln_g.reshape(1, SGU_WIDTH), sgu_ln_b.reshape(1, SGU_WIDTH), sgu_w.astype(BF16), bs)


def _bias_table(rpb):
    heads, n_dr, n_dc = rpb.shape
    c = jnp.arange(GRID_W)
    cs = jnp.clip(c - NA_KW // 2, 0, GRID_W - NA_KW)
    kc = jnp.arange(GRID_W)
    valid = (kc[None, :] >= cs[:, None]) & (kc[None, :] < cs[:, None] + NA_KW)
    lead = GRID_W - NA_KW
    padded = jnp.pad(rpb.astype(F32), ((0, 0), (0, 0), (lead, 2 * GRID_W - lead - n_dc)))
    flat = jnp.tile(padded, (1, 1, GRID_W))[:, :, :GRID_W * (2 * GRID_W - 1)]
    toe = flat.reshape(heads, n_dr, GRID_W, 2 * GRID_W - 1)[..., GRID_W - 1:]
    toe = jnp.where(valid, toe, MASKED)
    tab = jnp.stack([toe[:, NA_KH - 1 - o:2 * NA_KH - 1 - o] for o in range(NA_KH)])
    tab = tab.transpose(0, 1, 3, 2, 4)
    return tab.reshape(NA_KH, NA_HEADS * GRID_W, NA_KH * GRID_W)


def _natten_kernel(q_ref, k_ref, v_ref, kc_ref, vc_ref, bias_ref, o_ref, *, grid_rows):
    group_lanes = ATT_HEAD_GROUP * NA_HEAD_DIM
    group_rows = ATT_HEAD_GROUP * GRID_W
    row_head = lax.broadcasted_iota(jnp.int32, (group_rows, group_lanes), 0) // GRID_W
    lane_head = lax.broadcasted_iota(jnp.int32, (group_rows, group_lanes), 1) // NA_HEAD_DIM
    own_head = row_head == lane_head
    nt = (((1,), (1,)), ((), ()))

    def one_row(i, carry):
        r = pl.program_id(1) * ATT_ROWS + i
        rs = jnp.clip(r - NA_KH // 2, 0, grid_rows - NA_KH)
        k0 = pl.multiple_of(rs * GRID_W, GRID_W)
        q0 = pl.multiple_of(i * GRID_W, GRID_W)
        for grp in range(NA_HEADS // ATT_HEAD_GROUP):
            lanes = slice(grp * group_lanes, (grp + 1) * group_lanes)
            q = q_ref[0, pl.ds(q0, GRID_W), lanes]
            qs = jnp.where(own_head, jnp.concatenate([q] * ATT_HEAD_GROUP, axis=0),
                           jnp.zeros((), BF16))
            kr = k_ref[0, pl.ds(k0, NA_KH * GRID_W), lanes]
            vr = v_ref[0, pl.ds(k0, NA_KH * GRID_W), lanes]
            bias = bias_ref[r - rs, grp * group_rows:(grp + 1) * group_rows, :]
            s_nb = lax.dot_general(qs, kr, nt, preferred_element_type=F32) + bias
            s_cx = lax.dot_general(qs, kc_ref[0, :, lanes], nt, preferred_element_type=F32)
            m = jnp.maximum(jnp.max(s_nb, axis=-1, keepdims=True),
                            jnp.max(s_cx, axis=-1, keepdims=True))
            p_nb = jnp.exp(s_nb - m)
            p_cx = jnp.exp(s_cx - m)
            denom = jnp.sum(p_nb, axis=-1, keepdims=True) + jnp.sum(p_cx, axis=-1, keepdims=True)
            o = (jnp.dot(p_nb.astype(BF16), vr, preferred_element_type=F32)
                 + jnp.dot(p_cx.astype(BF16), vc_ref[0, :, lanes],
                           preferred_element_type=F32)) / denom
            o = jnp.where(own_head, o, 0.0)
            out = o[:GRID_W]
            for h in range(1, ATT_HEAD_GROUP):
                out = out + o[h * GRID_W:(h + 1) * GRID_W]
            o_ref[0, pl.ds(q0, GRID_W), lanes] = out.astype(BF16)
        return carry

    lax.fori_loop(0, ATT_ROWS, one_row, 0, unroll=True)


def _natten(q, k, v, k_c, v_c, bias):
    b, s, w = q.shape
    l = k_c.shape[1]
    grid_rows = s // GRID_W
    tq = ATT_ROWS * GRID_W
    full = pl.BlockSpec((1, s, w), lambda i, j: (i, 0, 0), pipeline_mode=pl.Buffered(1))
    ctx = pl.BlockSpec((1, l, w), lambda i, j: (i, 0, 0))
    return pl.pallas_call(
        functools.partial(_natten_kernel, grid_rows=grid_rows),
        grid=(b, grid_rows // ATT_ROWS),
        in_specs=[pl.BlockSpec((1, tq, w), lambda i, j: (i, j, 0)),
                  full, full, ctx, ctx,
                  pl.BlockSpec(bias.shape, lambda i, j: (0, 0, 0), pipeline_mode=pl.Buffered(1))],
        out_specs=pl.BlockSpec((1, tq, w), lambda i, j: (i, j, 0)),
        out_shape=jax.ShapeDtypeStruct((b, s, w), BF16),
        compiler_params=_params("parallel", "arbitrary"),
        name="natten",
    )(q, k, v, k_c, v_c, bias)


def _mix_out_kernel(att_ref, sgu_ref, x_ref, wo_ref, g1_ref, sh_ref, sc_ref, lng_ref, lnb_ref,
                    wr_ref, br_ref, x1_ref, t_ref, lg_ref):
    mix = (jnp.dot(att_ref[0], wo_ref[:NA_WIDTH, :], preferred_element_type=F32)
           + jnp.dot(sgu_ref[0], wo_ref[NA_WIDTH:, :], preferred_element_type=F32))
    x1 = _normalize(DEEPNORM_ALPHA * x_ref[0] + g1_ref[0] * mix) * lng_ref[...] + lnb_ref[...]
    x1_ref[0] = x1
    t = _normalize(x1) * (1.0 + sc_ref[0]) + sh_ref[0]
    rows = t.shape[0]
    for s in range(t.shape[1] // LANES):
        t_ref[pl.ds(s, rows, stride=SUBLANES), :] = t[:, s * LANES:(s + 1) * LANES]
    lg_ref[...] = lax.dot_general(wr_ref[...], t, (((1,), (1,)), ((), ())),
                                  precision=lax.Precision.HIGHEST,
                                  preferred_element_type=F32) + br_ref[...]


def _mix_out(att, sgu, x, w_out, g1, sh2, sc2, ln_g, ln_b, router_w, router_b):
    b, s, d = x.shape
    assert d == SUBLANES * LANES
    tm = min(ROW_TILE, s)
    nj = s // tm
    row = lambda width: pl.BlockSpec((1, tm, width), lambda i, j: (i, j, 0))
    mod_spec = pl.BlockSpec((1, 1, d), lambda i, j: (i, 0, 0))
    vec_spec = pl.BlockSpec((1, d), lambda i, j: (0, 0))
    return pl.pallas_call(
        _mix_out_kernel,
        grid=(b, nj),
        in_specs=[row(NA_WIDTH), row(SGU_WIDTH), row(d),
                  pl.BlockSpec((d, d), lambda i, j: (0, 0)),
                  mod_spec, mod_spec, mod_spec, vec_spec, vec_spec,
                  pl.BlockSpec((N_EXPERTS, d), lambda i, j: (0, 0)),
                  pl.BlockSpec((N_EXPERTS, 1), lambda i, j: (0, 0))],
        out_specs=[row(d),
                   pl.BlockSpec((tm * SUBLANES, LANES), lambda i, j: (i * nj + j, 0)),
                   pl.BlockSpec((N_EXPERTS, tm), lambda i, j: (0, i * nj + j))],
        out_shape=[jax.ShapeDtypeStruct((b, s, d), F32),
                   jax.ShapeDtypeStruct((b * s * SUBLANES, LANES), F32),
                   jax.ShapeDtypeStruct((N_EXPERTS, b * s), F32)],
        compiler_params=_params("parallel", "parallel"),
        name="mix_out",
    )(att, sgu, x, w_out, g1.reshape(b, 1, d), sh2.reshape(b, 1, d), sc2.reshape(b, 1, d),
      ln_g.reshape(1, d), ln_b.reshape(1, d), router_w.T, router_b.reshape(N_EXPERTS, 1))


def _route_kernel(lg_ref, e_ref, r_ref, g_ref, cnt_ref, run_ref):
    @pl.when(pl.program_id(0) == 0)
    def _():
        run_ref[...] = jnp.zeros_like(run_ref)

    logits = lg_ref[...]
    n_tok = logits.shape[1]
    expert = lax.broadcasted_iota(jnp.int32, logits.shape, 0)
    work = logits
    picks, tops = [], []
    for kk in range(TOP_K):
        m = jnp.max(work, axis=0, keepdims=True)
        first = jnp.min(jnp.where(work == m, expert, N_EXPERTS), axis=0, keepdims=True)
        pick = expert == first
        work = jnp.where(pick, -jnp.inf, work)
        picks.append(pick)
        tops.append(m)
        e_ref[kk:kk + 1, :] = first
    weights = [jnp.exp(m - tops[0]) for m in tops]
    denom = weights[0]
    for w in weights[1:]:
        denom = denom + w
    for kk in range(TOP_K):
        g_ref[kk:kk + 1, :] = weights[kk] / denom

    chosen = picks[0]
    for pick in picks[1:]:
        chosen = jnp.logical_or(chosen, pick)
    tri = (lax.broadcasted_iota(jnp.int32, (SCAN_BLOCK, SCAN_BLOCK), 0)
           <= lax.broadcasted_iota(jnp.int32, (SCAN_BLOCK, SCAN_BLOCK), 1)).astype(BF16)
    sel = jnp.where(chosen, 1.0, 0.0).astype(BF16)
    carry = run_ref[:, 0:1]
    for blk in range(n_tok // SCAN_BLOCK):
        lo = blk * SCAN_BLOCK
        run = jnp.dot(sel[:, lo:lo + SCAN_BLOCK], tri, preferred_element_type=F32) + carry
        for kk in range(TOP_K):
            rank = jnp.sum(jnp.where(picks[kk][:, lo:lo + SCAN_BLOCK], run - 1.0, 0.0),
                           axis=0, keepdims=True)
            r_ref[kk:kk + 1, lo:lo + SCAN_BLOCK] = rank.astype(jnp.int32)
        carry = run[:, SCAN_BLOCK - 1:SCAN_BLOCK]
    run_ref[...] = jnp.broadcast_to(carry, run_ref.shape)
    cnt_ref[...] = jnp.broadcast_to(carry, cnt_ref.shape).astype(jnp.int32)


def _route(logits_t):
    n = logits_t.shape[1]
    chunk = min(ROUTE_CHUNK, n)
    assert n % chunk == 0 and chunk % SCAN_BLOCK == 0
    per_k = pl.BlockSpec((TOP_K, chunk), lambda c: (0, c))
    return pl.pallas_call(
        _route_kernel,
        grid=(n // chunk,),
        in_specs=[pl.BlockSpec((N_EXPERTS, chunk), lambda c: (0, c))],
        out_specs=[per_k, per_k, per_k, pl.BlockSpec((N_EXPERTS, LANES), lambda c: (0, 0))],
        out_shape=[jax.ShapeDtypeStruct((TOP_K, n), jnp.int32),
                   jax.ShapeDtypeStruct((TOP_K, n), jnp.int32),
                   jax.ShapeDtypeStruct((TOP_K, n), F32),
                   jax.ShapeDtypeStruct((N_EXPERTS, LANES), jnp.int32)],
        scratch_shapes=[pltpu.VMEM((N_EXPERTS, LANES), F32)],
        compiler_params=_params("arbitrary"),
        name="route",
    )(logits_t)


MAX_PAD_ROWS = N_EXPERTS * EXPERT_ROWS


def _plan(expert_k, rank_k, counts, n_tiles_max):
    n_rows = expert_k.shape[1] * TOP_K
    padded = (counts + EXPERT_ROWS - 1) // EXPERT_ROWS * EXPERT_ROWS
    group_end = jnp.cumsum(padded)
    group_start = group_end - padded
    start_k = jnp.sum(jnp.where(expert_k[None] == jnp.arange(N_EXPERTS)[:, None, None],
                                group_start[:, None, None], 0), axis=0)
    slot = (start_k + rank_k).T.reshape(-1)
    tile_start = jnp.concatenate([group_start, group_end[-1:]]) // EXPERT_ROWS

    n_sorted = (n_tiles_max + 2) * EXPERT_ROWS
    flat = jnp.arange(n_rows, dtype=jnp.int32)
    inverse = jnp.full((n_sorted,), -1, jnp.int32).at[slot].set(flat, unique_indices=True)
    is_pad = inverse < 0
    src = jnp.where(is_pad, 0, inverse // TOP_K)
    pad_rank = jnp.minimum(jnp.cumsum(is_pad.astype(jnp.int32)) - 1, MAX_PAD_ROWS - 1)
    dst = jnp.where(is_pad, n_rows + pad_rank, inverse)
    lead = n_rows + MAX_PAD_ROWS + jnp.arange(EXPERT_ROWS, dtype=jnp.int32)
    dst_shifted = jnp.concatenate([lead, lead, dst[:n_sorted - 2 * EXPERT_ROWS]])
    table = jnp.concatenate([src.reshape(-1, EXPERT_ROWS), dst_shifted.reshape(-1, EXPERT_ROWS)],
                            axis=1)
    return tile_start.astype(jnp.int32), table.astype(jnp.int32)


def _experts_kernel(ts_ref, table_ref, t_ref, wgu_ref, bgu_ref, wd_ref, bd_ref, y_ref,
                    wgu_bf, wd_bf, xb, act, xbuf, ybuf, idx, xsem, ysem, isem, zsem):
    e = pl.program_id(0)
    tile_rows = xbuf.shape[1]
    rows = tile_rows // SUBLANES
    d_ff = wd_ref.shape[1]
    d_model, two_ff = wgu_ref.shape[1:]
    n_blocks = d_model // LANES
    n_total = ts_ref[N_EXPERTS]
    n_tok_rows = t_ref.shape[0] // SUBLANES * TOP_K

    def tile_of(ref, row):
        return ref.at[pl.ds(pl.multiple_of(row * SUBLANES, SUBLANES), SUBLANES), :]

    def table_copy(r, which):
        return pltpu.make_async_copy(table_ref.at[r], idx.at[which], isem.at[which])

    def gather_row(which, i):
        pltpu.make_async_copy(tile_of(t_ref, idx[which, i]), tile_of(xbuf.at[which], i),
                              xsem.at[which]).start(priority=i % 2)

    def scatter_row(src_which, idx_which, i):
        pltpu.make_async_copy(tile_of(ybuf.at[src_which], i),
                              tile_of(y_ref, idx[idx_which, rows + i]),
                              ysem.at[src_which]).start(priority=i % 2)

    def gather_rows(which):
        for i in range(rows):
            gather_row(which, i)

    def scatter_rows(src_which, idx_which):
        for i in range(rows):
            scatter_row(src_which, idx_which, i)

    def wait_gather(which):
        pltpu.make_async_copy(t_ref.at[pl.ds(0, tile_rows), :], xbuf.at[which],
                              xsem.at[which]).wait()

    def wait_scatter(which):
        pltpu.make_async_copy(ybuf.at[which], y_ref.at[pl.ds(0, tile_rows), :],
                              ysem.at[which]).wait()

    @pl.when(e == 0)
    def _():
        ybuf[...] = jnp.zeros_like(ybuf)
        n_dump = y_ref.shape[0] // SUBLANES - n_tok_rows
        fills = [pltpu.make_async_copy(
            ybuf.at[0], y_ref.at[pl.ds((n_tok_rows + k * rows) * SUBLANES, tile_rows), :], zsem)
            for k in range(n_dump // rows)]
        for f in fills:
            f.start()
        for f in fills:
            f.wait()
        table_copy(0, 0).start()
        table_copy(0, 0).wait()
        gather_rows(0)
        table_copy(1, 1).start()

    @pl.when(ts_ref[e] < ts_ref[e + 1])
    def _():
        wgu_bf[...] = wgu_ref[0].astype(BF16)
        wd_bf[...] = wd_ref[0].astype(BF16)

        def one_tile(t, carry):
            cur = t % 2
            nxt = 1 - cur
            table_copy(t + 1, nxt).wait()
            wait_gather(cur)

            @pl.when(t >= 1)
            def _():
                wait_scatter(cur)

            table_copy(t + 2, cur).start()
            starts = []
            for i in range(rows):
                starts.append(functools.partial(gather_row, nxt, i))
                starts.append(functools.partial(scatter_row, nxt, nxt, i))
            n_chunks = (two_ff + d_model) // MXU_COLS
            per_chunk = -(-len(starts) // n_chunks)

            def issue_some():
                for start in starts[:per_chunk]:
                    start()
                del starts[:per_chunk]

            for s in range(n_blocks):
                xb[:, s * LANES:(s + 1) * LANES] = xbuf[cur, pl.ds(s, rows, stride=SUBLANES),
                                                        :].astype(BF16)
            for c in range(d_ff // MXU_COLS):
                cols = slice(c * MXU_COLS, (c + 1) * MXU_COLS)
                up = slice(d_ff + c * MXU_COLS, d_ff + (c + 1) * MXU_COLS)
                gate = jnp.dot(xb[...], wgu_bf[:, cols], preferred_element_type=F32)
                issue_some()
                lin = jnp.dot(xb[...], wgu_bf[:, up], preferred_element_type=F32)
                issue_some()
                gate = jnp.minimum(gate + bgu_ref[0, :, cols], SWIGLU_LIMIT)
                lin = jnp.clip(lin + bgu_ref[0, :, up], -SWIGLU_LIMIT, SWIGLU_LIMIT)
                act[:, cols] = ((lin + 1.0) * (gate * jax.nn.sigmoid(SWIGLU_ALPHA * gate))
                                ).astype(BF16)
            for c in range(d_model // MXU_COLS):
                cols = slice(c * MXU_COLS, (c + 1) * MXU_COLS)
                y = jnp.dot(act[...], wd_bf[:, cols], preferred_element_type=F32) + bd_ref[0, :, cols]
                issue_some()
                for s in range(MXU_COLS // LANES):
                    blk = c * (MXU_COLS // LANES) + s
                    ybuf[cur, pl.ds(blk, rows, stride=SUBLANES), :] = y[:, s * LANES:(s + 1) * LANES]
            assert not starts
            return carry

        lax.fori_loop(ts_ref[e], ts_ref[e + 1], one_tile, 0)

    @pl.when(e == pl.num_programs(0) - 1)
    def _():
        last = (n_total - 1) % 2
        other = 1 - last
        wait_gather(other)
        table_copy(n_total + 1, last).wait()
        wait_scatter(other)
        scatter_rows(last, last)
        wait_scatter(last)


def _experts(tile_start, table, t_tiles, w_gu, b_gu, w_down, b_down):
    n_exp, d, two_ff = w_gu.shape
    d_ff = two_ff // 2
    by_expert = lambda e, ts: (e, 0, 0)
    tile_rows = EXPERT_ROWS * SUBLANES
    n_out_rows = t_tiles.shape[0] // SUBLANES * TOP_K + MAX_PAD_ROWS + EXPERT_ROWS
    grid_spec = pltpu.PrefetchScalarGridSpec(
        num_scalar_prefetch=1,
        grid=(n_exp,),
        in_specs=[pl.BlockSpec(memory_space=pl.ANY),
                  pl.BlockSpec(memory_space=pl.ANY),
                  pl.BlockSpec((1, d, two_ff), by_expert),
                  pl.BlockSpec((1, 1, two_ff), by_expert),
                  pl.BlockSpec((1, d_ff, d), by_expert),
                  pl.BlockSpec((1, 1, d), by_expert)],
        out_specs=pl.BlockSpec(memory_space=pl.ANY),
        scratch_shapes=[pltpu.VMEM((d, two_ff), BF16), pltpu.VMEM((d_ff, d), BF16),
                        pltpu.VMEM((EXPERT_ROWS, d), BF16), pltpu.VMEM((EXPERT_ROWS, d_ff), BF16),
                        pltpu.VMEM((2, tile_rows, LANES), F32),
                        pltpu.VMEM((2, tile_rows, LANES), F32),
                        pltpu.SMEM((2, 2 * EXPERT_ROWS), jnp.int32),
                        pltpu.SemaphoreType.DMA((2,)), pltpu.SemaphoreType.DMA((2,)),
                        pltpu.SemaphoreType.DMA((2,)), pltpu.SemaphoreType.DMA],
    )
    return pl.pallas_call(
        _experts_kernel,
        grid_spec=grid_spec,
        out_shape=jax.ShapeDtypeStruct((n_out_rows * SUBLANES, LANES), F32),
        compiler_params=_params("arbitrary"),
        name="experts",
    )(tile_start, table, t_tiles, w_gu, b_gu.reshape(n_exp, 1, two_ff), w_down,
      b_down.reshape(n_exp, 1, d))


def _combine_kernel(gate_ref, y_ref, x1_ref, g2_ref, lng_ref, lnb_ref, o_ref, acc_ref):
    j = pl.program_id(0)
    tokens = o_ref.shape[0]
    n_blocks = o_ref.shape[1] // LANES

    def reduce_token(i, carry):
        tile = None
        for kk in range(TOP_K):
            row0 = pl.multiple_of((i * TOP_K + kk) * SUBLANES, SUBLANES)
            part = gate_ref[(j * tokens + i) * TOP_K + kk] * y_ref[pl.ds(row0, SUBLANES), :]
            tile = part if tile is None else tile + part
        acc_ref[pl.ds(pl.multiple_of(i * SUBLANES, SUBLANES), SUBLANES), :] = tile
        return carry

    lax.fori_loop(0, tokens, reduce_token, 0, unroll=8)
    ffn = jnp.concatenate([acc_ref[pl.ds(s, tokens, stride=SUBLANES), :] for s in range(n_blocks)],
                          axis=1)
    z = DEEPNORM_ALPHA * x1_ref[...] + g2_ref[0] * ffn
    o_ref[...] = _normalize(z) * lng_ref[...] + lnb_ref[...]


def _combine(gates_flat, y_tiles, x1, g2, ln_g, ln_b, tokens_per_sample):
    n, d = x1.shape
    tokens = min(COMBINE_TOKENS, tokens_per_sample)
    assert tokens_per_sample % tokens == 0
    blocks_per_sample = tokens_per_sample // tokens
    vec = pl.BlockSpec((1, d), lambda j, *_: (0, 0))
    grid_spec = pltpu.PrefetchScalarGridSpec(
        num_scalar_prefetch=1,
        grid=(n // tokens,),
        in_specs=[pl.BlockSpec((tokens * TOP_K * SUBLANES, LANES), lambda j, *_: (j, 0)),
                  pl.BlockSpec((tokens, d), lambda j, *_: (j, 0)),
                  pl.BlockSpec((1, 1, d), lambda j, *_: (j // blocks_per_sample, 0, 0)),
                  vec, vec],
        out_specs=pl.BlockSpec((tokens, d), lambda j, *_: (j, 0)),
        scratch_shapes=[pltpu.VMEM((tokens * SUBLANES, LANES), F32)],
    )
    return pl.pallas_call(
        _combine_kernel,
        grid_spec=grid_spec,
        out_shape=jax.ShapeDtypeStruct((n, d), F32),
        compiler_params=_params("parallel"),
        name="combine",
    )(gates_flat, y_tiles, x1, g2, ln_g.reshape(1, d), ln_b.reshape(1, d))


def _layer(x, c, ctx, c_ctx, ada_w, ada_b, w_in, rpb, sgu_ln_g, sgu_ln_b, sgu_w, sgu_b, w_out,
           ln1_g, ln1_b, ln2_g, ln2_b, router_w, router_b, w_gu, b_gu, w_down, b_down):
    b, s, d = x.shape
    n = b * s
    assert s % (GRID_W * ATT_ROWS) == 0 and s // GRID_W >= NA_KH

    cond_rows = jnp.zeros((8, d), F32).at[:b].set(c).at[b].set(c_ctx)
    mod = _ada(cond_rows, ada_w, ada_b)
    sh1, sc1, g1, sh2, sc2, g2 = jnp.split(mod[:b], 6, axis=-1)
    csh1, csc1 = mod[b, :d], mod[b, d:2 * d]

    w_in_bf = w_in.astype(BF16)
    k_c, v_c = _ctx_kv(ctx, csh1, csc1, w_in_bf[:, NA_WIDTH:3 * NA_WIDTH])
    q, k, v, sgu = _proj_in(x, sh1, sc1, w_in_bf, sgu_ln_g, sgu_ln_b, sgu_w, sgu_b)
    att = _natten(q, k, v, k_c, v_c, _bias_table(rpb))
    x1, t_tiles, logits_t = _mix_out(att, sgu, x, w_out.astype(BF16), g1, sh2, sc2, ln1_g, ln1_b,
                                     router_w, router_b)

    expert_k, rank_k, gate_k, counts = _route(logits_t)
    n_tiles_max = n * TOP_K // EXPERT_ROWS + N_EXPERTS
    tile_start, table = _plan(expert_k, rank_k, counts[:, 0], n_tiles_max)
    y_tiles = _experts(tile_start, table, t_tiles, w_gu, b_gu, w_down, b_down)
    out = _combine(gate_k.T.reshape(-1), y_tiles, x1.reshape(n, d), g2.reshape(b, 1, d),
                   ln2_g, ln2_b, s)
    return out.reshape(b, s, d)


def kernel(x, c, ctx, c_ctx, ada_w, ada_b, w_in, rpb, sgu_ln_g, sgu_ln_b, sgu_w, sgu_b, w_out,
           ln1_g, ln1_b, ln2_g, ln2_b, router_w, router_b, exp_w_gu, exp_b_gu, exp_w_down,
           exp_b_down):
    assert ada_w.shape[0] == DEPTH
    return _layer(x, c, ctx, c_ctx, ada_w[0], ada_b[0], w_in[0], rpb[0], sgu_ln_g[0], sgu_ln_b[0],
                  sgu_w[0], sgu_b[0], w_out[0], ln1_g[0], ln1_b[0], ln2_g[0], ln2_b[0],
                  router_w[0], router_b[0], exp_w_gu[0], exp_b_gu[0], exp_w_down[0], exp_b_down[0])
```

```python
import functools

import jax
import jax.numpy as jnp
from jax import lax
from jax.experimental import pallas as pl
from jax.experimental.pallas import tpu as pltpu

F32 = jnp.float32
BF16 = jnp.bfloat16

GRID_W = 64
NA_HEADS = 8
NA_HEAD_DIM = 64
NA_WIDTH = NA_HEADS * NA_HEAD_DIM
NA_KH = 8
NA_KW = 16
SGU_GROUPS = 4
SGU_GROUP_DIM = 128
SGU_WIDTH = SGU_GROUPS * SGU_GROUP_DIM
CHUNK = 128
N_EXPERTS = 32
TOP_K = 4
SWIGLU_LIMIT = 7.0
SWIGLU_ALPHA = 1.702
LN_EPS = 1e-5
DEPTH = 1
DEEPNORM_ALPHA = (2.0 * DEPTH) ** 0.25
MASKED = -1e30

SUBLANES = 8
LANES = 128
VMEM_LIMIT_BYTES = 52 * 1024 * 1024

ROW_TILE = 512
ATT_ROWS = 8
ATT_HEAD_GROUP = 4
ROUTE_CHUNK = 1024
SCAN_BLOCK = 256
EXPERT_ROWS = 256
DISPATCH_TOKENS = 4096
COMBINE_TOKENS = 512


def _params(*sem):
    return pltpu.CompilerParams(dimension_semantics=sem, vmem_limit_bytes=VMEM_LIMIT_BYTES)


def _normalize(x):
    mu = jnp.mean(x, axis=-1, keepdims=True)
    xc = x - mu
    var = jnp.mean(xc * xc, axis=-1, keepdims=True)
    return xc * lax.rsqrt(var + LN_EPS)


def _ada_kernel(c_ref, w_ref, b_ref, o_ref):
    s = c_ref[...]
    s = s * jax.nn.sigmoid(s)
    o_ref[...] = jnp.dot(s, w_ref[...], precision=lax.Precision.HIGHEST,
                         preferred_element_type=F32) + b_ref[...]


def _ada(cond_rows, ada_w, ada_b):
    d = cond_rows.shape[1]
    n_out = ada_w.shape[1]
    return pl.pallas_call(
        _ada_kernel,
        grid=(n_out // d,),
        in_specs=[pl.BlockSpec((8, d), lambda j: (0, 0)),
                  pl.BlockSpec((d, d), lambda j: (0, j)),
                  pl.BlockSpec((1, d), lambda j: (0, j))],
        out_specs=pl.BlockSpec((8, d), lambda j: (0, j)),
        out_shape=jax.ShapeDtypeStruct((8, n_out), F32),
        compiler_params=_params("arbitrary"),
        name="ada",
    )(cond_rows, ada_w, ada_b.reshape(1, n_out))


def _ctx_kv_kernel(ctx_ref, sh_ref, sc_ref, w_ref, k_ref, v_ref):
    h = _normalize(ctx_ref[0]) * (1.0 + sc_ref[...]) + sh_ref[...]
    kv = jnp.dot(h.astype(BF16), w_ref[...], preferred_element_type=F32)
    k_ref[0] = kv[:, :NA_WIDTH].astype(BF16)
    v_ref[0] = kv[:, NA_WIDTH:].astype(BF16)


def _ctx_kv(ctx, csh1, csc1, w_kv):
    b, l, d = ctx.shape
    out = jax.ShapeDtypeStruct((b, l, NA_WIDTH), BF16)
    return pl.pallas_call(
        _ctx_kv_kernel,
        grid=(b,),
        in_specs=[pl.BlockSpec((1, l, d), lambda i: (i, 0, 0)),
                  pl.BlockSpec((1, d), lambda i: (0, 0)),
                  pl.BlockSpec((1, d), lambda i: (0, 0)),
                  pl.BlockSpec((d, 2 * NA_WIDTH), lambda i: (0, 0))],
        out_specs=[pl.BlockSpec((1, l, NA_WIDTH), lambda i: (i, 0, 0)),
                   pl.BlockSpec((1, l, NA_WIDTH), lambda i: (i, 0, 0))],
        out_shape=[out, out],
        compiler_params=_params("arbitrary"),
        name="ctx_kv",
    )(ctx, csh1.reshape(1, d), csc1.reshape(1, d), w_kv)


def _proj_in_kernel(x_ref, sh_ref, sc_ref, w_ref, lng_ref, lnb_ref, ws_ref, bs_ref,
                    q_ref, k_ref, v_ref, s_ref):
    h = (_normalize(x_ref[0]) * (1.0 + sc_ref[0]) + sh_ref[0]).astype(BF16)

    def proj(lo, width):
        return jnp.dot(h, w_ref[:, lo:lo + width], preferred_element_type=F32)

    q_ref[0] = (proj(0, NA_WIDTH) * (NA_HEAD_DIM ** -0.5)).astype(BF16)
    k_ref[0] = proj(NA_WIDTH, NA_WIDTH).astype(BF16)
    v_ref[0] = proj(2 * NA_WIDTH, NA_WIDTH).astype(BF16)
    u = jax.nn.gelu(proj(3 * NA_WIDTH, SGU_WIDTH))
    g = jax.nn.gelu(proj(3 * NA_WIDTH + SGU_WIDTH, SGU_WIDTH))
    gn = (_normalize(g) * lng_ref[...] + lnb_ref[...]).astype(BF16)
    rows = h.shape[0]
    for n in range(rows // CHUNK):
        r0 = n * CHUNK
        for grp in range(SGU_GROUPS):
            c0 = grp * SGU_GROUP_DIM
            mixed = jnp.dot(ws_ref[grp], gn[r0:r0 + CHUNK, c0:c0 + SGU_GROUP_DIM],
                            preferred_element_type=F32) + bs_ref[grp]
            s_ref[0, r0:r0 + CHUNK, c0:c0 + SGU_GROUP_DIM] = (
                u[r0:r0 + CHUNK, c0:c0 + SGU_GROUP_DIM] * mixed).astype(BF16)


def _proj_in(x, sh1, sc1, w_in, sgu_ln_g, sgu_ln_b, sgu_w, sgu_b):
    b, s, d = x.shape
    d_in = w_in.shape[1]
    tm = min(ROW_TILE, s)
    out = jax.ShapeDtypeStruct((b, s, NA_WIDTH), BF16)
    row_spec = pl.BlockSpec((1, tm, NA_WIDTH), lambda i, j: (i, j, 0))
    mod_spec = pl.BlockSpec((1, 1, d), lambda i, j: (i, 0, 0))
    bs = jnp.broadcast_to(sgu_b[:, :, None], (SGU_GROUPS, CHUNK, SGU_GROUP_DIM))
    return pl.pallas_call(
        _proj_in_kernel,
        grid=(b, s // tm),
        in_specs=[pl.BlockSpec((1, tm, d), lambda i, j: (i, j, 0)),
                  mod_spec, mod_spec,
                  pl.BlockSpec((d, d_in), lambda i, j: (0, 0)),
                  pl.BlockSpec((1, SGU_WIDTH), lambda i, j: (0, 0)),
                  pl.BlockSpec((1, SGU_WIDTH), lambda i, j: (0, 0)),
                  pl.BlockSpec((SGU_GROUPS, CHUNK, CHUNK), lambda i, j: (0, 0, 0)),
                  pl.BlockSpec((SGU_GROUPS, CHUNK, SGU_GROUP_DIM), lambda i, j: (0, 0, 0))],
        out_specs=[row_spec, row_spec, row_spec, row_spec],
        out_shape=[out, out, out, out],
        compiler_params=_params("parallel", "parallel"),
        name="proj_in",
    )(x, sh1.reshape(b, 1, d), sc1.reshape(b, 1, d), w_in,
      sgu_ln_g.reshape(1, SGU_WIDTH), sgu_ln_b.reshape(1, SGU_WIDTH), sgu_w.astype(BF16), bs)


def _bias_table(rpb):
    heads, n_dr, n_dc = rpb.shape
    c = jnp.arange(GRID_W)
    cs = jnp.clip(c - NA_KW // 2, 0, GRID_W - NA_KW)
    kc = jnp.arange(GRID_W)
    valid = (kc[None, :] >= cs[:, None]) & (kc[None, :] < cs[:, None] + NA_KW)
    lead = GRID_W - NA_KW
    padded = jnp.pad(rpb.astype(F32), ((0, 0), (0, 0), (lead, 2 * GRID_W - lead - n_dc)))
    flat = jnp.tile(padded, (1, 1, GRID_W))[:, :, :GRID_W * (2 * GRID_W - 1)]
    toe = flat.reshape(heads, n_dr, GRID_W, 2 * GRID_W - 1)[..., GRID_W - 1:]
    toe = jnp.where(valid, toe, MASKED)
    tab = jnp.stack([toe[:, NA_KH - 1 - o:2 * NA_KH - 1 - o] for o in range(NA_KH)])
    tab = tab.transpose(0, 1, 3, 2, 4)
    return tab.reshape(NA_KH, NA_HEADS * GRID_W, NA_KH * GRID_W)


def _natten_kernel(q_ref, k_ref, v_ref, kc_ref, vc_ref, bias_ref, o_ref, *, grid_rows):
    group_lanes = ATT_HEAD_GROUP * NA_HEAD_DIM
    group_rows = ATT_HEAD_GROUP * GRID_W
    row_head = lax.broadcasted_iota(jnp.int32, (group_rows, group_lanes), 0) // GRID_W
    lane_head = lax.broadcasted_iota(jnp.int32, (group_rows, group_lanes), 1) // NA_HEAD_DIM
    own_head = row_head == lane_head
    nt = (((1,), (1,)), ((), ()))

    def one_row(i, carry):
        r = pl.program_id(1) * ATT_ROWS + i
        rs = jnp.clip(r - NA_KH // 2, 0, grid_rows - NA_KH)
        k0 = pl.multiple_of(rs * GRID_W, GRID_W)
        q0 = pl.multiple_of(i * GRID_W, GRID_W)
        for grp in range(NA_HEADS // ATT_HEAD_GROUP):
            lanes = slice(grp * group_lanes, (grp + 1) * group_lanes)
            q = q_ref[0, pl.ds(q0, GRID_W), lanes]
            qs = jnp.where(own_head, jnp.concatenate([q] * ATT_HEAD_GROUP, axis=0),
                           jnp.zeros((), BF16))
            kr = k_ref[0, pl.ds(k0, NA_KH * GRID_W), lanes]
            vr = v_ref[0, pl.ds(k0, NA_KH * GRID_W), lanes]
            bias = bias_ref[r - rs, grp * group_rows:(grp + 1) * group_rows, :]
            s_nb = lax.dot_general(qs, kr, nt, preferred_element_type=F32) + bias
            s_cx = lax.dot_general(qs, kc_ref[0, :, lanes], nt, preferred_element_type=F32)
            m = jnp.maximum(jnp.max(s_nb, axis=-1, keepdims=True),
                            jnp.max(s_cx, axis=-1, keepdims=True))
            p_nb = jnp.exp(s_nb - m)
            p_cx = jnp.exp(s_cx - m)
            denom = jnp.sum(p_nb, axis=-1, keepdims=True) + jnp.sum(p_cx, axis=-1, keepdims=True)
            o = (jnp.dot(p_nb.astype(BF16), vr, preferred_element_type=F32)
                 + jnp.dot(p_cx.astype(BF16), vc_ref[0, :, lanes],
                           preferred_element_type=F32)) / denom
            o = jnp.where(own_head, o, 0.0)
            out = o[:GRID_W]
            for h in range(1, ATT_HEAD_GROUP):
                out = out + o[h * GRID_W:(h + 1) * GRID_W]
            o_ref[0, pl.ds(q0, GRID_W), lanes] = out.astype(BF16)
        return carry

    lax.fori_loop(0, ATT_ROWS, one_row, 0, unroll=True)


def _natten(q, k, v, k_c, v_c, bias):
    b, s, w = q.shape
    l = k_c.shape[1]
    grid_rows = s // GRID_W
    tq = ATT_ROWS * GRID_W
    full = pl.BlockSpec((1, s, w), lambda i, j: (i, 0, 0), pipeline_mode=pl.Buffered(1))
    ctx = pl.BlockSpec((1, l, w), lambda i, j: (i, 0, 0))
    return pl.pallas_call(
        functools.partial(_natten_kernel, grid_rows=grid_rows),
        grid=(b, grid_rows // ATT_ROWS),
        in_specs=[pl.BlockSpec((1, tq, w), lambda i, j: (i, j, 0)),
                  full, full, ctx, ctx,
                  pl.BlockSpec(bias.shape, lambda i, j: (0, 0, 0), pipeline_mode=pl.Buffered(1))],
        out_specs=pl.BlockSpec((1, tq, w), lambda i, j: (i, j, 0)),
        out_shape=jax.ShapeDtypeStruct((b, s, w), BF16),
        compiler_params=_params("parallel", "arbitrary"),
        name="natten",
    )(q, k, v, k_c, v_c, bias)


def _mix_out_kernel(att_ref, sgu_ref, x_ref, wo_ref, g1_ref, sh_ref, sc_ref, lng_ref, lnb_ref,
                    wr_ref, br_ref, x1_ref, t_ref, lg_ref):
    mix = (jnp.dot(att_ref[0], wo_ref[:NA_WIDTH, :], preferred_element_type=F32)
           + jnp.dot(sgu_ref[0], wo_ref[NA_WIDTH:, :], preferred_element_type=F32))
    x1 = _normalize(DEEPNORM_ALPHA * x_ref[0] + g1_ref[0] * mix) * lng_ref[...] + lnb_ref[...]
    x1_ref[0] = x1
    t = _normalize(x1) * (1.0 + sc_ref[0]) + sh_ref[0]
    rows = t.shape[0]
    for s in range(t.shape[1] // LANES):
        t_ref[pl.ds(s, rows, stride=SUBLANES), :] = t[:, s * LANES:(s + 1) * LANES]
    lg_ref[...] = lax.dot_general(wr_ref[...], t, (((1,), (1,)), ((), ())),
                                  precision=lax.Precision.HIGHEST,
                                  preferred_element_type=F32) + br_ref[...]


def _mix_out(att, sgu, x, w_out, g1, sh2, sc2, ln_g, ln_b, router_w, router_b):
    b, s, d = x.shape
    assert d == SUBLANES * LANES
    tm = min(ROW_TILE, s)
    nj = s // tm
    row = lambda width: pl.BlockSpec((1, tm, width), lambda i, j: (i, j, 0))
    mod_spec = pl.BlockSpec((1, 1, d), lambda i, j: (i, 0, 0))
    vec_spec = pl.BlockSpec((1, d), lambda i, j: (0, 0))
    return pl.pallas_call(
        _mix_out_kernel,
        grid=(b, nj),
        in_specs=[row(NA_WIDTH), row(SGU_WIDTH), row(d),
                  pl.BlockSpec((d, d), lambda i, j: (0, 0)),
                  mod_spec, mod_spec, mod_spec, vec_spec, vec_spec,
                  pl.BlockSpec((N_EXPERTS, d), lambda i, j: (0, 0)),
                  pl.BlockSpec((N_EXPERTS, 1), lambda i, j: (0, 0))],
        out_specs=[row(d),
                   pl.BlockSpec((tm * SUBLANES, LANES), lambda i, j: (i * nj + j, 0)),
                   pl.BlockSpec((N_EXPERTS, tm), lambda i, j: (0, i * nj + j))],
        out_shape=[jax.ShapeDtypeStruct((b, s, d), F32),
                   jax.ShapeDtypeStruct((b * s * SUBLANES, LANES), F32),
                   jax.ShapeDtypeStruct((N_EXPERTS, b * s), F32)],
        compiler_params=_params("parallel", "parallel"),
        name="mix_out",
    )(att, sgu, x, w_out, g1.reshape(b, 1, d), sh2.reshape(b, 1, d), sc2.reshape(b, 1, d),
      ln_g.reshape(1, d), ln_b.reshape(1, d), router_w.T, router_b.reshape(N_EXPERTS, 1))


def _route_kernel(lg_ref, e_ref, r_ref, g_ref, cnt_ref, run_ref):
    @pl.when(pl.program_id(0) == 0)
    def _():
        run_ref[...] = jnp.zeros_like(run_ref)

    logits = lg_ref[...]
    n_tok = logits.shape[1]
    expert = lax.broadcasted_iota(jnp.int32, logits.shape, 0)
    work = logits
    picks, tops = [], []
    for kk in range(TOP_K):
        m = jnp.max(work, axis=0, keepdims=True)
        first = jnp.min(jnp.where(work == m, expert, N_EXPERTS), axis=0, keepdims=True)
        pick = expert == first
        work = jnp.where(pick, -jnp.inf, work)
        picks.append(pick)
        tops.append(m)
        e_ref[kk:kk + 1, :] = first
    weights = [jnp.exp(m - tops[0]) for m in tops]
    denom = weights[0]
    for w in weights[1:]:
        denom = denom + w
    for kk in range(TOP_K):
        g_ref[kk:kk + 1, :] = weights[kk] / denom

    chosen = picks[0]
    for pick in picks[1:]:
        chosen = jnp.logical_or(chosen, pick)
    tri = (lax.broadcasted_iota(jnp.int32, (SCAN_BLOCK, SCAN_BLOCK), 0)
           <= lax.broadcasted_iota(jnp.int32, (SCAN_BLOCK, SCAN_BLOCK), 1)).astype(BF16)
    sel = jnp.where(chosen, 1.0, 0.0).astype(BF16)
    carry = run_ref[:, 0:1]
    for blk in range(n_tok // SCAN_BLOCK):
        lo = blk * SCAN_BLOCK
        run = jnp.dot(sel[:, lo:lo + SCAN_BLOCK], tri, preferred_element_type=F32) + carry
        for kk in range(TOP_K):
            rank = jnp.sum(jnp.where(picks[kk][:, lo:lo + SCAN_BLOCK], run - 1.0, 0.0),
                           axis=0, keepdims=True)
            r_ref[kk:kk + 1, lo:lo + SCAN_BLOCK] = rank.astype(jnp.int32)
        carry = run[:, SCAN_BLOCK - 1:SCAN_BLOCK]
    run_ref[...] = jnp.broadcast_to(carry, run_ref.shape)
    cnt_ref[...] = jnp.broadcast_to(carry, cnt_ref.shape).astype(jnp.int32)


def _route(logits_t):
    n = logits_t.shape[1]
    chunk = min(ROUTE_CHUNK, n)
    assert n % chunk == 0 and chunk % SCAN_BLOCK == 0
    per_k = pl.BlockSpec((TOP_K, chunk), lambda c: (0, c))
    return pl.pallas_call(
        _route_kernel,
        grid=(n // chunk,),
        in_specs=[pl.BlockSpec((N_EXPERTS, chunk), lambda c: (0, c))],
        out_specs=[per_k, per_k, per_k, pl.BlockSpec((N_EXPERTS, LANES), lambda c: (0, 0))],
        out_shape=[jax.ShapeDtypeStruct((TOP_K, n), jnp.int32),
                   jax.ShapeDtypeStruct((TOP_K, n), jnp.int32),
                   jax.ShapeDtypeStruct((TOP_K, n), F32),
                   jax.ShapeDtypeStruct((N_EXPERTS, LANES), jnp.int32)],
        scratch_shapes=[pltpu.VMEM((N_EXPERTS, LANES), F32)],
        compiler_params=_params("arbitrary"),
        name="route",
    )(logits_t)


def _plan(expert_k, rank_k, counts):
    padded = (counts + EXPERT_ROWS - 1) // EXPERT_ROWS * EXPERT_ROWS
    group_end = jnp.cumsum(padded)
    group_start = group_end - padded
    start_k = jnp.sum(jnp.where(expert_k[None] == jnp.arange(N_EXPERTS)[:, None, None],
                                group_start[:, None, None], 0), axis=0)
    slot = (start_k + rank_k).T.reshape(-1)
    tile_start = jnp.concatenate([group_start, group_end[-1:]]) // EXPERT_ROWS
    return (slot.astype(jnp.int32), tile_start.astype(jnp.int32),
            group_start.astype(jnp.int32), counts.astype(jnp.int32), padded.astype(jnp.int32))


def _dispatch_kernel(slot_ref, start_ref, cnt_ref, pad_ref, nt_ref, t_ref, xs_ref, zero_ref,
                     sem, zsem):
    tokens = t_ref.shape[0] // SUBLANES
    base = pl.program_id(0) * tokens
    tile_rows = zero_ref.shape[0]
    n_tiles_max = xs_ref.shape[0] // tile_rows

    def row_copy(src, row, sem_):
        dst = xs_ref.at[pl.ds(pl.multiple_of(row * SUBLANES, SUBLANES), SUBLANES), :]
        return pltpu.make_async_copy(src, dst, sem_)

    def tile_copy(tile):
        dst = xs_ref.at[pl.ds(pl.multiple_of(tile * tile_rows, tile_rows), tile_rows), :]
        return pltpu.make_async_copy(zero_ref, dst, zsem)

    @pl.when(pl.program_id(0) == 0)
    def _():
        zero_ref[...] = jnp.zeros_like(zero_ref)
        zero_row = zero_ref.at[pl.ds(0, SUBLANES), :]
        for e in range(N_EXPERTS):
            first = start_ref[e] + cnt_ref[e]
            n_pad = pad_ref[e] - cnt_ref[e]

            def fill(i, carry):
                row_copy(zero_row, first + i, zsem).start()
                return carry

            def drain(i, carry):
                row_copy(zero_row, first, zsem).wait()
                return carry

            lax.fori_loop(0, n_pad, fill, 0)
            lax.fori_loop(0, n_pad, drain, 0)

        def fill_tile(i, carry):
            tile_copy(i).start()
            return carry

        def drain_tile(i, carry):
            tile_copy(i).wait()
            return carry

        lax.fori_loop(nt_ref[0], n_tiles_max, fill_tile, 0)
        lax.fori_loop(nt_ref[0], n_tiles_max, drain_tile, 0)

    def scatter(i, carry):
        src = t_ref.at[pl.ds(pl.multiple_of(i * SUBLANES, SUBLANES), SUBLANES), :]
        for kk in range(TOP_K):
            row_copy(src, slot_ref[(base + i) * TOP_K + kk], sem).start(priority=kk % 2)
        return carry

    lax.fori_loop(0, tokens, scatter, 0, unroll=8)
    for kk in range(TOP_K):
        pltpu.make_async_copy(t_ref, xs_ref.at[pl.ds(0, tokens * SUBLANES), :], sem).wait()


def _dispatch(slot, group_start, counts, padded, n_tiles, t_tiles, n_rows):
    n = t_tiles.shape[0] // SUBLANES
    tokens = min(DISPATCH_TOKENS, n)
    assert n % tokens == 0 and n_rows % EXPERT_ROWS == 0
    grid_spec = pltpu.PrefetchScalarGridSpec(
        num_scalar_prefetch=5,
        grid=(n // tokens,),
        in_specs=[pl.BlockSpec((tokens * SUBLANES, LANES), lambda j, *_: (j, 0))],
        out_specs=pl.BlockSpec(memory_space=pl.ANY),
        scratch_shapes=[pltpu.VMEM((EXPERT_ROWS * SUBLANES, LANES), F32),
                        pltpu.SemaphoreType.DMA, pltpu.SemaphoreType.DMA],
    )
    return pl.pallas_call(
        _dispatch_kernel,
        grid_spec=grid_spec,
        out_shape=jax.ShapeDtypeStruct((n_rows * SUBLANES, LANES), F32),
        compiler_params=pltpu.CompilerParams(dimension_semantics=("arbitrary",),
                                             vmem_limit_bytes=VMEM_LIMIT_BYTES,
                                             has_side_effects=True),
        name="dispatch",
    )(slot, group_start, counts, padded, n_tiles, t_tiles)


def _experts_kernel(ts_ref, xs_ref, wgu_ref, bgu_ref, wd_ref, bd_ref, ys_ref,
                    wgu_bf, wd_bf, xbuf, ybuf, xsem, ysem):
    e = pl.program_id(0)
    tile_rows = xbuf.shape[1]
    rows = tile_rows // SUBLANES
    d_ff = wd_ref.shape[1]
    n_blocks = wgu_ref.shape[1] // LANES
    n_total = ts_ref[N_EXPERTS]

    def hbm_tile(ref, t):
        return ref.at[pl.ds(pl.multiple_of(t * tile_rows, tile_rows), tile_rows), :]

    def x_copy(t, which):
        return pltpu.make_async_copy(hbm_tile(xs_ref, t), xbuf.at[which], xsem.at[which])

    def y_copy(t, which):
        return pltpu.make_async_copy(ybuf.at[which], hbm_tile(ys_ref, t), ysem.at[which])

    @pl.when(e == 0)
    def _():
        x_copy(0, 0).start()

    @pl.when(ts_ref[e] < ts_ref[e + 1])
    def _():
        wgu_bf[...] = wgu_ref[0].astype(BF16)
        wd_bf[...] = wd_ref[0].astype(BF16)

        def one_tile(t, carry):
            cur = t % 2

            @pl.when(t + 1 < n_total)
            def _():
                x_copy(t + 1, 1 - cur).start(priority=1)

            x_copy(t, cur).wait()

            @pl.when(t >= 2)
            def _():
                y_copy(t - 2, cur).wait()

            x = jnp.concatenate(
                [xbuf[cur, pl.ds(s, rows, stride=SUBLANES), :] for s in range(n_blocks)],
                axis=1).astype(BF16)
            gu = jnp.dot(x, wgu_bf[...], preferred_element_type=F32) + bgu_ref[0]
            gate = jnp.minimum(gu[:, :d_ff], SWIGLU_LIMIT)
            lin = jnp.clip(gu[:, d_ff:], -SWIGLU_LIMIT, SWIGLU_LIMIT)
            act = ((lin + 1.0) * (gate * jax.nn.sigmoid(SWIGLU_ALPHA * gate))).astype(BF16)
            y = jnp.dot(act, wd_bf[...], preferred_element_type=F32) + bd_ref[0]
            for s in range(n_blocks):
                ybuf[cur, pl.ds(s, rows, stride=SUBLANES), :] = y[:, s * LANES:(s + 1) * LANES]
            y_copy(t, cur).start(priority=1)
            return carry

        lax.fori_loop(ts_ref[e], ts_ref[e + 1], one_tile, 0)

    @pl.when(e == pl.num_programs(0) - 1)
    def _():
        y_copy(n_total - 1, (n_total - 1) % 2).wait()

        @pl.when(n_total >= 2)
        def _():
            y_copy(n_total - 2, n_total % 2).wait()


def _experts(tile_start, xs, w_gu, b_gu, w_down, b_down):
    n_exp, d, two_ff = w_gu.shape
    d_ff = two_ff // 2
    by_expert = lambda e, ts: (e, 0, 0)
    tile_rows = EXPERT_ROWS * SUBLANES
    grid_spec = pltpu.PrefetchScalarGridSpec(
        num_scalar_prefetch=1,
        grid=(n_exp,),
        in_specs=[pl.BlockSpec(memory_space=pl.ANY),
                  pl.BlockSpec((1, d, two_ff), by_expert),
                  pl.BlockSpec((1, 1, two_ff), by_expert),
                  pl.BlockSpec((1, d_ff, d), by_expert),
                  pl.BlockSpec((1, 1, d), by_expert)],
        out_specs=pl.BlockSpec(memory_space=pl.ANY),
        scratch_shapes=[pltpu.VMEM((d, two_ff), BF16), pltpu.VMEM((d_ff, d), BF16),
                        pltpu.VMEM((2, tile_rows, LANES), F32),
                        pltpu.VMEM((2, tile_rows, LANES), F32),
                        pltpu.SemaphoreType.DMA((2,)), pltpu.SemaphoreType.DMA((2,))],
    )
    return pl.pallas_call(
        _experts_kernel,
        grid_spec=grid_spec,
        out_shape=jax.ShapeDtypeStruct(xs.shape, F32),
        input_output_aliases={1: 0},
        compiler_params=_params("arbitrary"),
        name="experts",
    )(tile_start, xs, w_gu, b_gu.reshape(n_exp, 1, two_ff), w_down, b_down.reshape(n_exp, 1, d))


def _combine_kernel(slot_ref, gate_ref, ys_ref, x1_ref, g2_ref, lng_ref, lnb_ref, o_ref,
                    buf, acc_ref, sem):
    j = pl.program_id(0)
    tokens = o_ref.shape[0]
    n_blocks = o_ref.shape[1] // LANES
    tile_rows = TOP_K * SUBLANES
    cur = j % 2

    def gather_token(block, which, i):
        for kk in range(TOP_K):
            row = slot_ref[(block * tokens + i) * TOP_K + kk]
            src = ys_ref.at[pl.ds(pl.multiple_of(row * SUBLANES, SUBLANES), SUBLANES), :]
            dst = buf.at[which, pl.ds(pl.multiple_of((i * TOP_K + kk) * SUBLANES, SUBLANES),
                                      SUBLANES), :]
            pltpu.make_async_copy(src, dst, sem.at[which]).start(priority=kk % 2)

    def reduce_token(i):
        tile = None
        for kk in range(TOP_K):
            row0 = pl.multiple_of((i * TOP_K + kk) * SUBLANES, SUBLANES)
            part = gate_ref[(j * tokens + i) * TOP_K + kk] * buf[cur, pl.ds(row0, SUBLANES), :]
            tile = part if tile is None else tile + part
        acc_ref[pl.ds(pl.multiple_of(i * SUBLANES, SUBLANES), SUBLANES), :] = tile

    @pl.when(j == 0)
    def _():
        def first(i, carry):
            gather_token(0, 0, i)
            return carry
        lax.fori_loop(0, tokens, first, 0, unroll=8)

    pltpu.make_async_copy(ys_ref.at[pl.ds(0, tokens * tile_rows), :], buf.at[cur],
                          sem.at[cur]).wait()

    @pl.when(j + 1 < pl.num_programs(0))
    def _():
        def step(i, carry):
            gather_token(j + 1, 1 - cur, i)
            reduce_token(i)
            return carry
        lax.fori_loop(0, tokens, step, 0, unroll=8)

    @pl.when(j + 1 == pl.num_programs(0))
    def _():
        def step(i, carry):
            reduce_token(i)
            return carry
        lax.fori_loop(0, tokens, step, 0, unroll=8)

    ffn = jnp.concatenate([acc_ref[pl.ds(s, tokens, stride=SUBLANES), :] for s in range(n_blocks)],
                          axis=1)
    z = DEEPNORM_ALPHA * x1_ref[...] + g2_ref[0] * ffn
    o_ref[...] = _normalize(z) * lng_ref[...] + lnb_ref[...]


def _combine(slot, gates_flat, ys, x1, g2, ln_g, ln_b, tokens_per_sample):
    n, d = x1.shape
    tokens = min(COMBINE_TOKENS, tokens_per_sample)
    assert tokens_per_sample % tokens == 0
    blocks_per_sample = tokens_per_sample // tokens
    vec = pl.BlockSpec((1, d), lambda j, *_: (0, 0))
    grid_spec = pltpu.PrefetchScalarGridSpec(
        num_scalar_prefetch=2,
        grid=(n // tokens,),
        in_specs=[pl.BlockSpec(memory_space=pl.ANY),
                  pl.BlockSpec((tokens, d), lambda j, *_: (j, 0)),
                  pl.BlockSpec((1, 1, d), lambda j, *_: (j // blocks_per_sample, 0, 0)),
                  vec, vec],
        out_specs=pl.BlockSpec((tokens, d), lambda j, *_: (j, 0)),
        scratch_shapes=[pltpu.VMEM((2, tokens * TOP_K * SUBLANES, LANES), F32),
                        pltpu.VMEM((tokens * SUBLANES, LANES), F32),
                        pltpu.SemaphoreType.DMA((2,))],
    )
    return pl.pallas_call(
        _combine_kernel,
        grid_spec=grid_spec,
        out_shape=jax.ShapeDtypeStruct((n, d), F32),
        compiler_params=_params("arbitrary"),
        name="combine",
    )(slot, gates_flat, ys, x1, g2, ln_g.reshape(1, d), ln_b.reshape(1, d))


def _layer(x, c, ctx, c_ctx, ada_w, ada_b, w_in, rpb, sgu_ln_g, sgu_ln_b, sgu_w, sgu_b, w_out,
           ln1_g, ln1_b, ln2_g, ln2_b, router_w, router_b, w_gu, b_gu, w_down, b_down):
    b, s, d = x.shape
    n = b * s
    assert s % (GRID_W * ATT_ROWS) == 0 and s // GRID_W >= NA_KH

    cond_rows = jnp.zeros((8, d), F32).at[:b].set(c).at[b].set(c_ctx)
    mod = _ada(cond_rows, ada_w, ada_b)
    sh1, sc1, g1, sh2, sc2, g2 = jnp.split(mod[:b], 6, axis=-1)
    csh1, csc1 = mod[b, :d], mod[b, d:2 * d]

    w_in_bf = w_in.astype(BF16)
    k_c, v_c = _ctx_kv(ctx, csh1, csc1, w_in_bf[:, NA_WIDTH:3 * NA_WIDTH])
    q, k, v, sgu = _proj_in(x, sh1, sc1, w_in_bf, sgu_ln_g, sgu_ln_b, sgu_w, sgu_b)
    att = _natten(q, k, v, k_c, v_c, _bias_table(rpb))
    x1, t_tiles, logits_t = _mix_out(att, sgu, x, w_out.astype(BF16), g1, sh2, sc2, ln1_g, ln1_b,
                                     router_w, router_b)

    expert_k, rank_k, gate_k, counts = _route(logits_t)
    n_tiles_max = n * TOP_K // EXPERT_ROWS + N_EXPERTS
    slot, tile_start, group_start, counts, padded = _plan(expert_k, rank_k, counts[:, 0])
    xs = _dispatch(slot, group_start, counts, padded, tile_start[N_EXPERTS:], t_tiles,
                   n_tiles_max * EXPERT_ROWS)
    ys = _experts(tile_start, xs, w_gu, b_gu, w_down, b_down)
    out = _combine(slot, gate_k.T.reshape(-1), ys, x1.reshape(n, d), g2.reshape(b, 1, d),
                   ln2_g, ln2_b, s)
    return out.reshape(b, s, d)


def kernel(x, c, ctx, c_ctx, ada_w, ada_b, w_in, rpb, sgu_ln_g, sgu_ln_b, sgu_w, sgu_b, w_out,
           ln1_g, ln1_b, ln2_g, ln2_b, router_w, router_b, exp_w_gu, exp_b_gu, exp_w_down,
           exp_b_down):
    assert ada_w.shape[0] == DEPTH
    return _layer(x, c, ctx, c_ctx, ada_w[0], ada_b[0], w_in[0], rpb[0], sgu_ln_g[0], sgu_ln_b[0],
                  sgu_w[0], sgu_b[0], w_out[0], ln1_g[0], ln1_b[0], ln2_g[0], ln2_b[0],
                  router_w[0], router_b[0], exp_w_gu[0], exp_b_gu[0], exp_w_down[0], exp_b_down[0])
```

```python
import functools

import jax
import jax.numpy as jnp
from jax import lax
from jax.experimental import pallas as pl
from jax.experimental.pallas import tpu as pltpu

F32 = jnp.float32
BF16 = jnp.bfloat16

GRID_W = 64
NA_HEADS = 8
NA_HEAD_DIM = 64
NA_WIDTH = NA_HEADS * NA_HEAD_DIM
NA_KH = 8
NA_KW = 16
SGU_GROUPS = 4
SGU_GROUP_DIM = 128
SGU_WIDTH = SGU_GROUPS * SGU_GROUP_DIM
CHUNK = 128
N_EXPERTS = 32
TOP_K = 4
SWIGLU_LIMIT = 7.0
SWIGLU_ALPHA = 1.702
LN_EPS = 1e-5
DEPTH = 1
DEEPNORM_ALPHA = (2.0 * DEPTH) ** 0.25
MASKED = -1e30

SUBLANES = 8
LANES = 128
VMEM_LIMIT_BYTES = 52 * 1024 * 1024

ROW_TILE = 1024
ATT_ROWS = 8
ATT_HEAD_GROUP = 4
ROUTE_CHUNK = 1024
SCAN_BLOCK = 256
EXPERT_ROWS = 512
DISPATCH_TOKENS = 4096
COMBINE_TOKENS = 512


def _params(*sem):
    return pltpu.CompilerParams(dimension_semantics=sem, vmem_limit_bytes=VMEM_LIMIT_BYTES)


def _normalize(x):
    mu = jnp.mean(x, axis=-1, keepdims=True)
    xc = x - mu
    var = jnp.mean(xc * xc, axis=-1, keepdims=True)
    return xc * lax.rsqrt(var + LN_EPS)


def _ada_kernel(c_ref, w_ref, b_ref, o_ref):
    s = c_ref[...]
    s = s * jax.nn.sigmoid(s)
    o_ref[...] = jnp.dot(s, w_ref[...], precision=lax.Precision.HIGHEST,
                         preferred_element_type=F32) + b_ref[...]


def _ada(cond_rows, ada_w, ada_b):
    d = cond_rows.shape[1]
    n_out = ada_w.shape[1]
    return pl.pallas_call(
        _ada_kernel,
        grid=(n_out // d,),
        in_specs=[pl.BlockSpec((8, d), lambda j: (0, 0)),
                  pl.BlockSpec((d, d), lambda j: (0, j)),
                  pl.BlockSpec((1, d), lambda j: (0, j))],
        out_specs=pl.BlockSpec((8, d), lambda j: (0, j)),
        out_shape=jax.ShapeDtypeStruct((8, n_out), F32),
        compiler_params=_params("arbitrary"),
        name="ada",
    )(cond_rows, ada_w, ada_b.reshape(1, n_out))


def _ctx_kv_kernel(ctx_ref, sh_ref, sc_ref, w_ref, k_ref, v_ref):
    h = _normalize(ctx_ref[0]) * (1.0 + sc_ref[...]) + sh_ref[...]
    kv = jnp.dot(h.astype(BF16), w_ref[...], preferred_element_type=F32)
    k_ref[0] = kv[:, :NA_WIDTH].astype(BF16)
    v_ref[0] = kv[:, NA_WIDTH:].astype(BF16)


def _ctx_kv(ctx, csh1, csc1, w_kv):
    b, l, d = ctx.shape
    out = jax.ShapeDtypeStruct((b, l, NA_WIDTH), BF16)
    return pl.pallas_call(
        _ctx_kv_kernel,
        grid=(b,),
        in_specs=[pl.BlockSpec((1, l, d), lambda i: (i, 0, 0)),
                  pl.BlockSpec((1, d), lambda i: (0, 0)),
                  pl.BlockSpec((1, d), lambda i: (0, 0)),
                  pl.BlockSpec((d, 2 * NA_WIDTH), lambda i: (0, 0))],
        out_specs=[pl.BlockSpec((1, l, NA_WIDTH), lambda i: (i, 0, 0)),
                   pl.BlockSpec((1, l, NA_WIDTH), lambda i: (i, 0, 0))],
        out_shape=[out, out],
        compiler_params=_params("arbitrary"),
        name="ctx_kv",
    )(ctx, csh1.reshape(1, d), csc1.reshape(1, d), w_kv)


def _proj_in_kernel(x_ref, sh_ref, sc_ref, w_ref, lng_ref, lnb_ref, ws_ref, bs_ref,
                    q_ref, k_ref, v_ref, s_ref):
    h = (_normalize(x_ref[0]) * (1.0 + sc_ref[0]) + sh_ref[0]).astype(BF16)

    def proj(lo, width):
        return jnp.dot(h, w_ref[:, lo:lo + width], preferred_element_type=F32)

    q_ref[0] = (proj(0, NA_WIDTH) * (NA_HEAD_DIM ** -0.5)).astype(BF16)
    k_ref[0] = proj(NA_WIDTH, NA_WIDTH).astype(BF16)
    v_ref[0] = proj(2 * NA_WIDTH, NA_WIDTH).astype(BF16)
    u = jax.nn.gelu(proj(3 * NA_WIDTH, SGU_WIDTH))
    g = jax.nn.gelu(proj(3 * NA_WIDTH + SGU_WIDTH, SGU_WIDTH))
    gn = (_normalize(g) * lng_ref[...] + lnb_ref[...]).astype(BF16)
    rows = h.shape[0]
    for n in range(rows // CHUNK):
        r0 = n * CHUNK
        for grp in range(SGU_GROUPS):
            c0 = grp * SGU_GROUP_DIM
            mixed = jnp.dot(ws_ref[grp], gn[r0:r0 + CHUNK, c0:c0 + SGU_GROUP_DIM],
                            preferred_element_type=F32) + bs_ref[grp]
            s_ref[0, r0:r0 + CHUNK, c0:c0 + SGU_GROUP_DIM] = (
                u[r0:r0 + CHUNK, c0:c0 + SGU_GROUP_DIM] * mixed).astype(BF16)


def _proj_in(x, sh1, sc1, w_in, sgu_ln_g, sgu_ln_b, sgu_w, sgu_b):
    b, s, d = x.shape
    d_in = w_in.shape[1]
    tm = min(ROW_TILE, s)
    out = jax.ShapeDtypeStruct((b, s, NA_WIDTH), BF16)
    row_spec = pl.BlockSpec((1, tm, NA_WIDTH), lambda i, j: (i, j, 0))
    mod_spec = pl.BlockSpec((1, 1, d), lambda i, j: (i, 0, 0))
    bs = jnp.broadcast_to(sgu_b[:, :, None], (SGU_GROUPS, CHUNK, SGU_GROUP_DIM))
    return pl.pallas_call(
        _proj_in_kernel,
        grid=(b, s // tm),
        in_specs=[pl.BlockSpec((1, tm, d), lambda i, j: (i, j, 0)),
                  mod_spec, mod_spec,
                  pl.BlockSpec((d, d_in), lambda i, j: (0, 0)),
                  pl.BlockSpec((1, SGU_WIDTH), lambda i, j: (0, 0)),
                  pl.BlockSpec((1, SGU_WIDTH), lambda i, j: (0, 0)),
                  pl.BlockSpec((SGU_GROUPS, CHUNK, CHUNK), lambda i, j: (0, 0, 0)),
                  pl.BlockSpec((SGU_GROUPS, CHUNK, SGU_GROUP_DIM), lambda i, j: (0, 0, 0))],
        out_specs=[row_spec, row_spec, row_spec, row_spec],
        out_shape=[out, out, out, out],
        compiler_params=_params("parallel", "parallel"),
        name="proj_in",
    )(x, sh1.reshape(b, 1, d), sc1.reshape(b, 1, d), w_in,
      sgu_ln_g.reshape(1, SGU_WIDTH), sgu_ln_b.reshape(1, SGU_WIDTH), sgu_w.astype(BF16), bs)


def _bias_table(rpb):
    heads, n_dr, n_dc = rpb.shape
    c = jnp.arange(GRID_W)
    cs = jnp.clip(c - NA_KW // 2, 0, GRID_W - NA_KW)
    kc = jnp.arange(GRID_W)
    valid = (kc[None, :] >= cs[:, None]) & (kc[None, :] < cs[:, None] + NA_KW)
    lead = GRID_W - NA_KW
    padded = jnp.pad(rpb.astype(F32), ((0, 0), (0, 0), (lead, 2 * GRID_W - lead - n_dc)))
    flat = jnp.tile(padded, (1, 1, GRID_W))[:, :, :GRID_W * (2 * GRID_W - 1)]
    toe = flat.reshape(heads, n_dr, GRID_W, 2 * GRID_W - 1)[..., GRID_W - 1:]
    toe = jnp.where(valid, toe, MASKED)
    tab = jnp.stack([toe[:, NA_KH - 1 - o:2 * NA_KH - 1 - o] for o in range(NA_KH)])
    tab = tab.transpose(0, 1, 3, 2, 4)
    return tab.reshape(NA_KH, NA_HEADS * GRID_W, NA_KH * GRID_W)


def _natten_kernel(q_ref, k_ref, v_ref, kc_ref, vc_ref, bias_ref, o_ref, *, grid_rows):
    group_lanes = ATT_HEAD_GROUP * NA_HEAD_DIM
    group_rows = ATT_HEAD_GROUP * GRID_W
    row_head = lax.broadcasted_iota(jnp.int32, (group_rows, group_lanes), 0) // GRID_W
    lane_head = lax.broadcasted_iota(jnp.int32, (group_rows, group_lanes), 1) // NA_HEAD_DIM
    own_head = row_head == lane_head
    nt = (((1,), (1,)), ((), ()))

    def one_row(i, carry):
        r = pl.program_id(1) * ATT_ROWS + i
        rs = jnp.clip(r - NA_KH // 2, 0, grid_rows - NA_KH)
        k0 = pl.multiple_of(rs * GRID_W, GRID_W)
        q0 = pl.multiple_of(i * GRID_W, GRID_W)
        for grp in range(NA_HEADS // ATT_HEAD_GROUP):
            lanes = slice(grp * group_lanes, (grp + 1) * group_lanes)
            q = q_ref[0, pl.ds(q0, GRID_W), lanes]
            qs = jnp.where(own_head, jnp.concatenate([q] * ATT_HEAD_GROUP, axis=0),
                           jnp.zeros((), BF16))
            kr = k_ref[0, pl.ds(k0, NA_KH * GRID_W), lanes]
            vr = v_ref[0, pl.ds(k0, NA_KH * GRID_W), lanes]
            bias = bias_ref[r - rs, grp * group_rows:(grp + 1) * group_rows, :]
            s_nb = lax.dot_general(qs, kr, nt, preferred_element_type=F32) + bias
            s_cx = lax.dot_general(qs, kc_ref[0, :, lanes], nt, preferred_element_type=F32)
            m = jnp.maximum(jnp.max(s_nb, axis=-1, keepdims=True),
                            jnp.max(s_cx, axis=-1, keepdims=True))
            p_nb = jnp.exp(s_nb - m)
            p_cx = jnp.exp(s_cx - m)
            denom = jnp.sum(p_nb, axis=-1, keepdims=True) + jnp.sum(p_cx, axis=-1, keepdims=True)
            o = (jnp.dot(p_nb.astype(BF16), vr, preferred_element_type=F32)
                 + jnp.dot(p_cx.astype(BF16), vc_ref[0, :, lanes],
                           preferred_element_type=F32)) / denom
            o = jnp.where(own_head, o, 0.0)
            out = o[:GRID_W]
            for h in range(1, ATT_HEAD_GROUP):
                out = out + o[h * GRID_W:(h + 1) * GRID_W]
            o_ref[0, pl.ds(q0, GRID_W), lanes] = out.astype(BF16)
        return carry

    lax.fori_loop(0, ATT_ROWS, one_row, 0, unroll=True)


def _natten(q, k, v, k_c, v_c, bias):
    b, s, w = q.shape
    l = k_c.shape[1]
    grid_rows = s // GRID_W
    tq = ATT_ROWS * GRID_W
    full = pl.BlockSpec((1, s, w), lambda i, j: (i, 0, 0), pipeline_mode=pl.Buffered(1))
    ctx = pl.BlockSpec((1, l, w), lambda i, j: (i, 0, 0))
    return pl.pallas_call(
        functools.partial(_natten_kernel, grid_rows=grid_rows),
        grid=(b, grid_rows // ATT_ROWS),
        in_specs=[pl.BlockSpec((1, tq, w), lambda i, j: (i, j, 0)),
                  full, full, ctx, ctx,
                  pl.BlockSpec(bias.shape, lambda i, j: (0, 0, 0), pipeline_mode=pl.Buffered(1))],
        out_specs=pl.BlockSpec((1, tq, w), lambda i, j: (i, j, 0)),
        out_shape=jax.ShapeDtypeStruct((b, s, w), BF16),
        compiler_params=_params("parallel", "arbitrary"),
        name="natten",
    )(q, k, v, k_c, v_c, bias)


def _mix_out_kernel(att_ref, sgu_ref, x_ref, wo_ref, g1_ref, sh_ref, sc_ref, lng_ref, lnb_ref,
                    wr_ref, br_ref, x1_ref, t_ref, lg_ref):
    mix = (jnp.dot(att_ref[0], wo_ref[:NA_WIDTH, :], preferred_element_type=F32)
           + jnp.dot(sgu_ref[0], wo_ref[NA_WIDTH:, :], preferred_element_type=F32))
    x1 = _normalize(DEEPNORM_ALPHA * x_ref[0] + g1_ref[0] * mix) * lng_ref[...] + lnb_ref[...]
    x1_ref[0] = x1
    t = _normalize(x1) * (1.0 + sc_ref[0]) + sh_ref[0]
    rows = t.shape[0]
    for s in range(t.shape[1] // LANES):
        t_ref[pl.ds(s, rows, stride=SUBLANES), :] = t[:, s * LANES:(s + 1) * LANES]
    nt = (((1,), (1,)), ((), ()))
    t_hi = t.astype(BF16)
    t_lo = (t - t_hi.astype(F32)).astype(BF16)
    w = wr_ref[...]
    w_hi = w.astype(BF16)
    w_lo = (w - w_hi.astype(F32)).astype(BF16)
    lg_ref[...] = (lax.dot_general(w_hi, t_hi, nt, preferred_element_type=F32)
                   + lax.dot_general(w_hi, t_lo, nt, preferred_element_type=F32)
                   + lax.dot_general(w_lo, t_hi, nt, preferred_element_type=F32)) + br_ref[...]


def _mix_out(att, sgu, x, w_out, g1, sh2, sc2, ln_g, ln_b, router_w, router_b):
    b, s, d = x.shape
    assert d == SUBLANES * LANES
    tm = min(ROW_TILE, s)
    nj = s // tm
    row = lambda width: pl.BlockSpec((1, tm, width), lambda i, j: (i, j, 0))
    mod_spec = pl.BlockSpec((1, 1, d), lambda i, j: (i, 0, 0))
    vec_spec = pl.BlockSpec((1, d), lambda i, j: (0, 0))
    return pl.pallas_call(
        _mix_out_kernel,
        grid=(b, nj),
        in_specs=[row(NA_WIDTH), row(SGU_WIDTH), row(d),
                  pl.BlockSpec((d, d), lambda i, j: (0, 0)),
                  mod_spec, mod_spec, mod_spec, vec_spec, vec_spec,
                  pl.BlockSpec((N_EXPERTS, d), lambda i, j: (0, 0)),
                  pl.BlockSpec((N_EXPERTS, 1), lambda i, j: (0, 0))],
        out_specs=[row(d),
                   pl.BlockSpec((tm * SUBLANES, LANES), lambda i, j: (i * nj + j, 0)),
                   pl.BlockSpec((N_EXPERTS, tm), lambda i, j: (0, i * nj + j))],
        out_shape=[jax.ShapeDtypeStruct((b, s, d), F32),
                   jax.ShapeDtypeStruct((b * s * SUBLANES, LANES), F32),
                   jax.ShapeDtypeStruct((N_EXPERTS, b * s), F32)],
        compiler_params=_params("parallel", "parallel"),
        name="mix_out",
    )(att, sgu, x, w_out, g1.reshape(b, 1, d), sh2.reshape(b, 1, d), sc2.reshape(b, 1, d),
      ln_g.reshape(1, d), ln_b.reshape(1, d), router_w.T, router_b.reshape(N_EXPERTS, 1))


def _route_kernel(lg_ref, e_ref, r_ref, g_ref, cnt_ref, run_ref):
    @pl.when(pl.program_id(0) == 0)
    def _():
        run_ref[...] = jnp.zeros_like(run_ref)

    logits = lg_ref[...]
    n_tok = logits.shape[1]
    expert = lax.broadcasted_iota(jnp.int32, logits.shape, 0)
    work = logits
    picks, tops = [], []
    for kk in range(TOP_K):
        m = jnp.max(work, axis=0, keepdims=True)
        first = jnp.min(jnp.where(work == m, expert, N_EXPERTS), axis=0, keepdims=True)
        pick = expert == first
        work = jnp.where(pick, -jnp.inf, work)
        picks.append(pick)
        tops.append(m)
        e_ref[kk:kk + 1, :] = first
    weights = [jnp.exp(m - tops[0]) for m in tops]
    denom = weights[0]
    for w in weights[1:]:
        denom = denom + w
    for kk in range(TOP_K):
        g_ref[kk:kk + 1, :] = weights[kk] / denom

    chosen = picks[0]
    for pick in picks[1:]:
        chosen = jnp.logical_or(chosen, pick)
    tri = (lax.broadcasted_iota(jnp.int32, (SCAN_BLOCK, SCAN_BLOCK), 0)
           <= lax.broadcasted_iota(jnp.int32, (SCAN_BLOCK, SCAN_BLOCK), 1)).astype(BF16)
    sel = jnp.where(chosen, 1.0, 0.0).astype(BF16)
    carry = run_ref[:, 0:1]
    for blk in range(n_tok // SCAN_BLOCK):
        lo = blk * SCAN_BLOCK
        run = jnp.dot(sel[:, lo:lo + SCAN_BLOCK], tri, preferred_element_type=F32) + carry
        for kk in range(TOP_K):
            rank = jnp.sum(jnp.where(picks[kk][:, lo:lo + SCAN_BLOCK], run - 1.0, 0.0),
                           axis=0, keepdims=True)
            r_ref[kk:kk + 1, lo:lo + SCAN_BLOCK] = rank.astype(jnp.int32)
        carry = run[:, SCAN_BLOCK - 1:SCAN_BLOCK]
    run_ref[...] = jnp.broadcast_to(carry, run_ref.shape)
    cnt_ref[...] = jnp.broadcast_to(carry, cnt_ref.shape).astype(jnp.int32)


def _route(logits_t):
    n = logits_t.shape[1]
    chunk = min(ROUTE_CHUNK, n)
    assert n % chunk == 0 and chunk % SCAN_BLOCK == 0
    per_k = pl.BlockSpec((TOP_K, chunk), lambda c: (0, c))
    return pl.pallas_call(
        _route_kernel,
        grid=(n // chunk,),
        in_specs=[pl.BlockSpec((N_EXPERTS, chunk), lambda c: (0, c))],
        out_specs=[per_k, per_k, per_k, pl.BlockSpec((N_EXPERTS, LANES), lambda c: (0, 0))],
        out_shape=[jax.ShapeDtypeStruct((TOP_K, n), jnp.int32),
                   jax.ShapeDtypeStruct((TOP_K, n), jnp.int32),
                   jax.ShapeDtypeStruct((TOP_K, n), F32),
                   jax.ShapeDtypeStruct((N_EXPERTS, LANES), jnp.int32)],
        scratch_shapes=[pltpu.VMEM((N_EXPERTS, LANES), F32)],
        compiler_params=_params("arbitrary"),
        name="route",
    )(logits_t)


def _plan(expert_k, rank_k, counts):
    padded = (counts + EXPERT_ROWS - 1) // EXPERT_ROWS * EXPERT_ROWS
    group_end = jnp.cumsum(padded)
    group_start = group_end - padded
    start_k = jnp.sum(jnp.where(expert_k[None] == jnp.arange(N_EXPERTS)[:, None, None],
                                group_start[:, None, None], 0), axis=0)
    slot = (start_k + rank_k).reshape(-1)
    tile_start = jnp.concatenate([group_start, group_end[-1:]]) // EXPERT_ROWS
    return (slot.astype(jnp.int32), tile_start.astype(jnp.int32),
            group_start.astype(jnp.int32), counts.astype(jnp.int32), padded.astype(jnp.int32))


def _dispatch_kernel(slot_ref, start_ref, cnt_ref, pad_ref, nt_ref, t_ref, xs_ref, zero_ref,
                     sem, zsem):
    tokens = t_ref.shape[0] // SUBLANES
    n_tokens = tokens * pl.num_programs(0)
    base = pl.program_id(0) * tokens
    tile_rows = zero_ref.shape[0]
    n_tiles_max = xs_ref.shape[0] // tile_rows

    def row_copy(src, row, sem_):
        dst = xs_ref.at[pl.ds(pl.multiple_of(row * SUBLANES, SUBLANES), SUBLANES), :]
        return pltpu.make_async_copy(src, dst, sem_)

    def tile_copy(tile):
        dst = xs_ref.at[pl.ds(pl.multiple_of(tile * tile_rows, tile_rows), tile_rows), :]
        return pltpu.make_async_copy(zero_ref, dst, zsem)

    @pl.when(pl.program_id(0) == 0)
    def _():
        zero_ref[...] = jnp.zeros_like(zero_ref)
        zero_row = zero_ref.at[pl.ds(0, SUBLANES), :]
        for e in range(N_EXPERTS):
            first = start_ref[e] + cnt_ref[e]
            n_pad = pad_ref[e] - cnt_ref[e]

            def fill(i, carry):
                row_copy(zero_row, first + i, zsem).start()
                return carry

            def drain(i, carry):
                row_copy(zero_row, first, zsem).wait()
                return carry

            lax.fori_loop(0, n_pad, fill, 0)
            lax.fori_loop(0, n_pad, drain, 0)

        def fill_tile(i, carry):
            tile_copy(i).start()
            return carry

        def drain_tile(i, carry):
            tile_copy(i).wait()
            return carry

        lax.fori_loop(nt_ref[0], n_tiles_max, fill_tile, 0)
        lax.fori_loop(nt_ref[0], n_tiles_max, drain_tile, 0)

    def scatter(i, carry):
        src = t_ref.at[pl.ds(pl.multiple_of(i * SUBLANES, SUBLANES), SUBLANES), :]
        for kk in range(TOP_K):
            row_copy(src, slot_ref[kk * n_tokens + base + i], sem).start(priority=kk % 2)
        return carry

    lax.fori_loop(0, tokens, scatter, 0, unroll=8)
    for kk in range(TOP_K):
        pltpu.make_async_copy(t_ref, xs_ref.at[pl.ds(0, tokens * SUBLANES), :], sem).wait()


def _dispatch(slot, group_start, counts, padded, n_tiles, t_tiles, n_rows):
    n = t_tiles.shape[0] // SUBLANES
    tokens = min(DISPATCH_TOKENS, n)
    assert n % tokens == 0 and n_rows % EXPERT_ROWS == 0
    grid_spec = pltpu.PrefetchScalarGridSpec(
        num_scalar_prefetch=5,
        grid=(n // tokens,),
        in_specs=[pl.BlockSpec((tokens * SUBLANES, LANES), lambda j, *_: (j, 0))],
        out_specs=pl.BlockSpec(memory_space=pl.ANY),
        scratch_shapes=[pltpu.VMEM((EXPERT_ROWS * SUBLANES, LANES), F32),
                        pltpu.SemaphoreType.DMA, pltpu.SemaphoreType.DMA],
    )
    return pl.pallas_call(
        _dispatch_kernel,
        grid_spec=grid_spec,
        out_shape=jax.ShapeDtypeStruct((n_rows * SUBLANES, LANES), F32),
        compiler_params=pltpu.CompilerParams(dimension_semantics=("arbitrary",),
                                             vmem_limit_bytes=VMEM_LIMIT_BYTES,
                                             has_side_effects=True),
        name="dispatch",
    )(slot, group_start, counts, padded, n_tiles, t_tiles)


def _experts_kernel(ts_ref, xs_ref, wgu_ref, bgu_ref, wd_ref, bd_ref, ys_ref,
                    wgu_bf, wd_bf, xbuf, ybuf, xsem, ysem):
    e = pl.program_id(0)
    tile_rows = xbuf.shape[1]
    rows = tile_rows // SUBLANES
    d_ff = wd_ref.shape[1]
    n_blocks = wgu_ref.shape[1] // LANES
    n_total = ts_ref[N_EXPERTS]

    def hbm_tile(ref, t):
        return ref.at[pl.ds(pl.multiple_of(t * tile_rows, tile_rows), tile_rows), :]

    def x_copy(t, which):
        return pltpu.make_async_copy(hbm_tile(xs_ref, t), xbuf.at[which], xsem.at[which])

    def y_copy(t, which):
        return pltpu.make_async_copy(ybuf.at[which], hbm_tile(ys_ref, t), ysem.at[which])

    @pl.when(e == 0)
    def _():
        x_copy(0, 0).start()

    @pl.when(ts_ref[e] < ts_ref[e + 1])
    def _():
        wgu_bf[...] = wgu_ref[0].astype(BF16)
        wd_bf[...] = wd_ref[0].astype(BF16)

        def one_tile(t, carry):
            cur = t % 2

            @pl.when(t + 1 < n_total)
            def _():
                x_copy(t + 1, 1 - cur).start(priority=1)

            x_copy(t, cur).wait()

            @pl.when(t >= 2)
            def _():
                y_copy(t - 2, cur).wait()

            x = jnp.concatenate(
                [xbuf[cur, pl.ds(s, rows, stride=SUBLANES), :] for s in range(n_blocks)],
                axis=1).astype(BF16)
            gu = jnp.dot(x, wgu_bf[...], preferred_element_type=F32) + bgu_ref[0]
            gate = jnp.minimum(gu[:, :d_ff], SWIGLU_LIMIT)
            lin = jnp.clip(gu[:, d_ff:], -SWIGLU_LIMIT, SWIGLU_LIMIT)
            act = ((lin + 1.0) * (gate * jax.nn.sigmoid(SWIGLU_ALPHA * gate))).astype(BF16)
            y = jnp.dot(act, wd_bf[...], preferred_element_type=F32) + bd_ref[0]
            for s in range(n_blocks):
                ybuf[cur, pl.ds(s, rows, stride=SUBLANES), :] = y[:, s * LANES:(s + 1) * LANES]
            y_copy(t, cur).start(priority=1)
            return carry

        lax.fori_loop(ts_ref[e], ts_ref[e + 1], one_tile, 0)

    @pl.when(e == pl.num_programs(0) - 1)
    def _():
        y_copy(n_total - 1, (n_total - 1) % 2).wait()

        @pl.when(n_total >= 2)
        def _():
            y_copy(n_total - 2, n_total % 2).wait()


def _experts(tile_start, xs, w_gu, b_gu, w_down, b_down):
    n_exp, d, two_ff = w_gu.shape
    d_ff = two_ff // 2
    by_expert = lambda e, ts: (e, 0, 0)
    tile_rows = EXPERT_ROWS * SUBLANES
    grid_spec = pltpu.PrefetchScalarGridSpec(
        num_scalar_prefetch=1,
        grid=(n_exp,),
        in_specs=[pl.BlockSpec(memory_space=pl.ANY),
                  pl.BlockSpec((1, d, two_ff), by_expert),
                  pl.BlockSpec((1, 1, two_ff), by_expert),
                  pl.BlockSpec((1, d_ff, d), by_expert),
                  pl.BlockSpec((1, 1, d), by_expert)],
        out_specs=pl.BlockSpec(memory_space=pl.ANY),
        scratch_shapes=[pltpu.VMEM((d, two_ff), BF16), pltpu.VMEM((d_ff, d), BF16),
                        pltpu.VMEM((2, tile_rows, LANES), F32),
                        pltpu.VMEM((2, tile_rows, LANES), F32),
                        pltpu.SemaphoreType.DMA((2,)), pltpu.SemaphoreType.DMA((2,))],
    )
    return pl.pallas_call(
        _experts_kernel,
        grid_spec=grid_spec,
        out_shape=jax.ShapeDtypeStruct(xs.shape, F32),
        input_output_aliases={1: 0},
        compiler_params=_params("arbitrary"),
        name="experts",
    )(tile_start, xs, w_gu, b_gu.reshape(n_exp, 1, two_ff), w_down, b_down.reshape(n_exp, 1, d))


def _combine_kernel(slot_ref, gate_ref, ys_ref, x1_ref, g2_ref, lng_ref, lnb_ref, o_ref,
                    buf, acc_ref, sem):
    j = pl.program_id(0)
    tokens = o_ref.shape[0]
    n_tokens = tokens * pl.num_programs(0)
    n_blocks = o_ref.shape[1] // LANES
    tile_rows = TOP_K * SUBLANES
    cur = j % 2

    def gather_token(block, which, i):
        for kk in range(TOP_K):
            row = slot_ref[kk * n_tokens + block * tokens + i]
            src = ys_ref.at[pl.ds(pl.multiple_of(row * SUBLANES, SUBLANES), SUBLANES), :]
            dst = buf.at[which, pl.ds(pl.multiple_of((i * TOP_K + kk) * SUBLANES, SUBLANES),
                                      SUBLANES), :]
            pltpu.make_async_copy(src, dst, sem.at[which]).start(priority=kk % 2)

    def reduce_token(i):
        tile = None
        for kk in range(TOP_K):
            row0 = pl.multiple_of((i * TOP_K + kk) * SUBLANES, SUBLANES)
            part = gate_ref[kk * n_tokens + j * tokens + i] * buf[cur, pl.ds(row0, SUBLANES), :]
            tile = part if tile is None else tile + part
        acc_ref[pl.ds(pl.multiple_of(i * SUBLANES, SUBLANES), SUBLANES), :] = tile

    @pl.when(j == 0)
    def _():
        def first(i, carry):
            gather_token(0, 0, i)
            return carry
        lax.fori_loop(0, tokens, first, 0, unroll=8)

    pltpu.make_async_copy(ys_ref.at[pl.ds(0, tokens * tile_rows), :], buf.at[cur],
                          sem.at[cur]).wait()

    @pl.when(j + 1 < pl.num_programs(0))
    def _():
        def step(i, carry):
            gather_token(j + 1, 1 - cur, i)
            reduce_token(i)
            return carry
        lax.fori_loop(0, tokens, step, 0, unroll=8)

    @pl.when(j + 1 == pl.num_programs(0))
    def _():
        def step(i, carry):
            reduce_token(i)
            return carry
        lax.fori_loop(0, tokens, step, 0, unroll=8)

    ffn = jnp.concatenate([acc_ref[pl.ds(s, tokens, stride=SUBLANES), :] for s in range(n_blocks)],
                          axis=1)
    z = DEEPNORM_ALPHA * x1_ref[...] + g2_ref[0] * ffn
    o_ref[...] = _normalize(z) * lng_ref[...] + lnb_ref[...]


def _combine(slot, gates_flat, ys, x1, g2, ln_g, ln_b, tokens_per_sample):
    n, d = x1.shape
    tokens = min(COMBINE_TOKENS, tokens_per_sample)
    assert tokens_per_sample % tokens == 0
    blocks_per_sample = tokens_per_sample // tokens
    vec = pl.BlockSpec((1, d), lambda j, *_: (0, 0))
    grid_spec = pltpu.PrefetchScalarGridSpec(
        num_scalar_prefetch=2,
        grid=(n // tokens,),
        in_specs=[pl.BlockSpec(memory_space=pl.ANY),
                  pl.BlockSpec((tokens, d), lambda j, *_: (j, 0)),
                  pl.BlockSpec((1, 1, d), lambda j, *_: (j // blocks_per_sample, 0, 0)),
                  vec, vec],
        out_specs=pl.BlockSpec((tokens, d), lambda j, *_: (j, 0)),
        scratch_shapes=[pltpu.VMEM((2, tokens * TOP_K * SUBLANES, LANES), F32),
                        pltpu.VMEM((tokens * SUBLANES, LANES), F32),
                        pltpu.SemaphoreType.DMA((2,))],
    )
    return pl.pallas_call(
        _combine_kernel,
        grid_spec=grid_spec,
        out_shape=jax.ShapeDtypeStruct((n, d), F32),
        compiler_params=_params("arbitrary"),
        name="combine",
    )(slot, gates_flat, ys, x1, g2, ln_g.reshape(1, d), ln_b.reshape(1, d))


def _layer(x, c, ctx, c_ctx, ada_w, ada_b, w_in, rpb, sgu_ln_g, sgu_ln_b, sgu_w, sgu_b, w_out,
           ln1_g, ln1_b, ln2_g, ln2_b, router_w, router_b, w_gu, b_gu, w_down, b_down):
    b, s, d = x.shape
    n = b * s
    assert s % (GRID_W * ATT_ROWS) == 0 and s // GRID_W >= NA_KH

    cond_rows = jnp.zeros((8, d), F32).at[:b].set(c).at[b].set(c_ctx)
    mod = _ada(cond_rows, ada_w, ada_b)
    sh1, sc1, g1, sh2, sc2, g2 = jnp.split(mod[:b], 6, axis=-1)
    csh1, csc1 = mod[b, :d], mod[b, d:2 * d]

    w_in_bf = w_in.astype(BF16)
    k_c, v_c = _ctx_kv(ctx, csh1, csc1, w_in_bf[:, NA_WIDTH:3 * NA_WIDTH])
    q, k, v, sgu = _proj_in(x, sh1, sc1, w_in_bf, sgu_ln_g, sgu_ln_b, sgu_w, sgu_b)
    att = _natten(q, k, v, k_c, v_c, _bias_table(rpb))
    x1, t_tiles, logits_t = _mix_out(att, sgu, x, w_out.astype(BF16), g1, sh2, sc2, ln1_g, ln1_b,
                                     router_w, router_b)

    expert_k, rank_k, gate_k, counts = _route(logits_t)
    n_tiles_max = n * TOP_K // EXPERT_ROWS + N_EXPERTS
    slot, tile_start, group_start, counts, padded = _plan(expert_k, rank_k, counts[:, 0])
    xs = _dispatch(slot, group_start, counts, padded, tile_start[N_EXPERTS:], t_tiles,
                   n_tiles_max * EXPERT_ROWS)
    ys = _experts(tile_start, xs, w_gu, b_gu, w_down, b_down)
    out = _combine(slot, gate_k.reshape(-1), ys, x1.reshape(n, d), g2.reshape(b, 1, d),
                   ln2_g, ln2_b, s)
    return out.reshape(b, s, d)


def kernel(x, c, ctx, c_ctx, ada_w, ada_b, w_in, rpb, sgu_ln_g, sgu_ln_b, sgu_w, sgu_b, w_out,
           ln1_g, ln1_b, ln2_g, ln2_b, router_w, router_b, exp_w_gu, exp_b_gu, exp_w_down,
           exp_b_down):
    assert ada_w.shape[0] == DEPTH
    return _layer(x, c, ctx, c_ctx, ada_w[0], ada_b[0], w_in[0], rpb[0], sgu_ln_g[0], sgu_ln_b[0],
                  sgu_w[0], sgu_b[0], w_out[0], ln1_g[0], ln1_b[0], ln2_g[0], ln2_b[0],
                  router_w[0], router_b[0], exp_w_gu[0], exp_b_gu[0], exp_w_down[0], exp_b_down[0])
```

```python
import functools

import jax
import jax.numpy as jnp
from jax import lax
from jax.experimental import pallas as pl
from jax.experimental.pallas import tpu as pltpu

F32 = jnp.float32
BF16 = jnp.bfloat16

GRID_W = 64
NA_HEADS = 8
NA_HEAD_DIM = 64
NA_WIDTH = NA_HEADS * NA_HEAD_DIM
NA_KH = 8
NA_KW = 16
SGU_GROUPS = 4
SGU_GROUP_DIM = 128
SGU_WIDTH = SGU_GROUPS * SGU_GROUP_DIM
CHUNK = 128
N_EXPERTS = 32
TOP_K = 4
SWIGLU_LIMIT = 7.0
SWIGLU_ALPHA = 1.702
LN_EPS = 1e-5
DEPTH = 1
DEEPNORM_ALPHA = (2.0 * DEPTH) ** 0.25
MASKED = -1e30

SUBLANES = 8
LANES = 128
VMEM_LIMIT_BYTES = 52 * 1024 * 1024

ROW_TILE = 1024
ATT_ROWS = 8
ATT_HEAD_GROUP = 4
ROUTE_CHUNK = 1024
SCAN_BLOCK = 256
EXPERT_ROWS = 512
DISPATCH_TOKENS = 4096
COMBINE_TOKENS = 512


def _params(*sem):
    return pltpu.CompilerParams(dimension_semantics=sem, vmem_limit_bytes=VMEM_LIMIT_BYTES)


def _dot_split3(a, b, dims):
    a_hi = a.astype(BF16)
    a_lo = (a - a_hi.astype(F32)).astype(BF16)
    b_hi = b.astype(BF16)
    b_lo = (b - b_hi.astype(F32)).astype(BF16)
    dot = functools.partial(lax.dot_general, dimension_numbers=dims, preferred_element_type=F32)
    return dot(a_hi, b_hi) + dot(a_hi, b_lo) + dot(a_lo, b_hi)


def _normalize(x):
    mu = jnp.mean(x, axis=-1, keepdims=True)
    xc = x - mu
    var = jnp.mean(xc * xc, axis=-1, keepdims=True)
    return xc * lax.rsqrt(var + LN_EPS)


def _ada_kernel(c_ref, w_ref, b_ref, o_ref):
    s = c_ref[...]
    s = s * jax.nn.sigmoid(s)
    o_ref[...] = _dot_split3(s, w_ref[...], (((1,), (0,)), ((), ()))) + b_ref[...]


def _ada(cond_rows, ada_w, ada_b):
    d = cond_rows.shape[1]
    n_out = ada_w.shape[1]
    return pl.pallas_call(
        _ada_kernel,
        grid=(n_out // d,),
        in_specs=[pl.BlockSpec((8, d), lambda j: (0, 0)),
                  pl.BlockSpec((d, d), lambda j: (0, j)),
                  pl.BlockSpec((1, d), lambda j: (0, j))],
        out_specs=pl.BlockSpec((8, d), lambda j: (0, j)),
        out_shape=jax.ShapeDtypeStruct((8, n_out), F32),
        compiler_params=_params("arbitrary"),
        name="ada",
    )(cond_rows, ada_w, ada_b.reshape(1, n_out))


def _ctx_kv_kernel(ctx_ref, sh_ref, sc_ref, w_ref, k_ref, v_ref):
    h = _normalize(ctx_ref[0]) * (1.0 + sc_ref[...]) + sh_ref[...]
    kv = jnp.dot(h.astype(BF16), w_ref[...], preferred_element_type=F32)
    k_ref[0] = kv[:, :NA_WIDTH].astype(BF16)
    v_ref[0] = kv[:, NA_WIDTH:].astype(BF16)


def _ctx_kv(ctx, csh1, csc1, w_kv):
    b, l, d = ctx.shape
    out = jax.ShapeDtypeStruct((b, l, NA_WIDTH), BF16)
    return pl.pallas_call(
        _ctx_kv_kernel,
        grid=(b,),
        in_specs=[pl.BlockSpec((1, l, d), lambda i: (i, 0, 0)),
                  pl.BlockSpec((1, d), lambda i: (0, 0)),
                  pl.BlockSpec((1, d), lambda i: (0, 0)),
                  pl.BlockSpec((d, 2 * NA_WIDTH), lambda i: (0, 0))],
        out_specs=[pl.BlockSpec((1, l, NA_WIDTH), lambda i: (i, 0, 0)),
                   pl.BlockSpec((1, l, NA_WIDTH), lambda i: (i, 0, 0))],
        out_shape=[out, out],
        compiler_params=_params("arbitrary"),
        name="ctx_kv",
    )(ctx, csh1.reshape(1, d), csc1.reshape(1, d), w_kv)


def _proj_in_kernel(x_ref, sh_ref, sc_ref, w_ref, lng_ref, lnb_ref, ws_ref, bs_ref,
                    q_ref, k_ref, v_ref, s_ref):
    h = (_normalize(x_ref[0]) * (1.0 + sc_ref[0]) + sh_ref[0]).astype(BF16)

    def proj(lo, width):
        return jnp.dot(h, w_ref[:, lo:lo + width], preferred_element_type=F32)

    q_ref[0] = (proj(0, NA_WIDTH) * (NA_HEAD_DIM ** -0.5)).astype(BF16)
    k_ref[0] = proj(NA_WIDTH, NA_WIDTH).astype(BF16)
    v_ref[0] = proj(2 * NA_WIDTH, NA_WIDTH).astype(BF16)
    u = jax.nn.gelu(proj(3 * NA_WIDTH, SGU_WIDTH))
    g = jax.nn.gelu(proj(3 * NA_WIDTH + SGU_WIDTH, SGU_WIDTH))
    gn = (_normalize(g) * lng_ref[...] + lnb_ref[...]).astype(BF16)
    rows = h.shape[0]
    for n in range(rows // CHUNK):
        r0 = n * CHUNK
        for grp in range(SGU_GROUPS):
            c0 = grp * SGU_GROUP_DIM
            mixed = jnp.dot(ws_ref[grp], gn[r0:r0 + CHUNK, c0:c0 + SGU_GROUP_DIM],
                            preferred_element_type=F32) + bs_ref[grp]
            s_ref[0, r0:r0 + CHUNK, c0:c0 + SGU_GROUP_DIM] = (
                u[r0:r0 + CHUNK, c0:c0 + SGU_GROUP_DIM] * mixed).astype(BF16)


def _proj_in(x, sh1, sc1, w_in, sgu_ln_g, sgu_ln_b, sgu_w, sgu_b):
    b, s, d = x.shape
    d_in = w_in.shape[1]
    tm = min(ROW_TILE, s)
    out = jax.ShapeDtypeStruct((b, s, NA_WIDTH), BF16)
    row_spec = pl.BlockSpec((1, tm, NA_WIDTH), lambda i, j: (i, j, 0))
    mod_spec = pl.BlockSpec((1, 1, d), lambda i, j: (i, 0, 0))
    bs = jnp.broadcast_to(sgu_b[:, :, None], (SGU_GROUPS, CHUNK, SGU_GROUP_DIM))
    return pl.pallas_call(
        _proj_in_kernel,
        grid=(b, s // tm),
        in_specs=[pl.BlockSpec((1, tm, d), lambda i, j: (i, j, 0)),
                  mod_spec, mod_spec,
                  pl.BlockSpec((d, d_in), lambda i, j: (0, 0)),
                  pl.BlockSpec((1, SGU_WIDTH), lambda i, j: (0, 0)),
                  pl.BlockSpec((1, SGU_WIDTH), lambda i, j: (0, 0)),
                  pl.BlockSpec((SGU_GROUPS, CHUNK, CHUNK), lambda i, j: (0, 0, 0)),
                  pl.BlockSpec((SGU_GROUPS, CHUNK, SGU_GROUP_DIM), lambda i, j: (0, 0, 0))],
        out_specs=[row_spec, row_spec, row_spec, row_spec],
        out_shape=[out, out, out, out],
        compiler_params=_params("parallel", "parallel"),
        name="proj_in",
    )(x, sh1.reshape(b, 1, d), sc1.reshape(b, 1, d), w_in,
      sgu_ln_g.reshape(1, SGU_WIDTH), sgu_ln_b.reshape(1, SGU_WIDTH), sgu_w.astype(BF16), bs)


def _bias_table(rpb):
    heads, n_dr, n_dc = rpb.shape
    c = jnp.arange(GRID_W)
    cs = jnp.clip(c - NA_KW // 2, 0, GRID_W - NA_KW)
    kc = jnp.arange(GRID_W)
    valid = (kc[None, :] >= cs[:, None]) & (kc[None, :] < cs[:, None] + NA_KW)
    lead = GRID_W - NA_KW
    padded = jnp.pad(rpb.astype(F32), ((0, 0), (0, 0), (lead, 2 * GRID_W - lead - n_dc)))
    flat = jnp.tile(padded, (1, 1, GRID_W))[:, :, :GRID_W * (2 * GRID_W - 1)]
    toe = flat.reshape(heads, n_dr, GRID_W, 2 * GRID_W - 1)[..., GRID_W - 1:]
    toe = jnp.where(valid, toe, MASKED)
    tab = jnp.stack([
        jnp.concatenate([toe[:, NA_KH - 1 - o + i] for i in range(NA_KH)], axis=-1)
        for o in range(NA_KH)])
    return tab.reshape(NA_KH, NA_HEADS * GRID_W, NA_KH * GRID_W)


def _natten_kernel(q_ref, k_ref, v_ref, kc_ref, vc_ref, bias_ref, o_ref, *, grid_rows):
    group_lanes = ATT_HEAD_GROUP * NA_HEAD_DIM
    group_rows = ATT_HEAD_GROUP * GRID_W
    row_head = lax.broadcasted_iota(jnp.int32, (group_rows, group_lanes), 0) // GRID_W
    lane_head = lax.broadcasted_iota(jnp.int32, (group_rows, group_lanes), 1) // NA_HEAD_DIM
    own_head = row_head == lane_head
    nt = (((1,), (1,)), ((), ()))

    def one_row(i, carry):
        r = pl.program_id(1) * ATT_ROWS + i
        rs = jnp.clip(r - NA_KH // 2, 0, grid_rows - NA_KH)
        k0 = pl.multiple_of(rs * GRID_W, GRID_W)
        q0 = pl.multiple_of(i * GRID_W, GRID_W)
        for grp in range(NA_HEADS // ATT_HEAD_GROUP):
            lanes = slice(grp * group_lanes, (grp + 1) * group_lanes)
            q = q_ref[0, pl.ds(q0, GRID_W), lanes]
            qs = jnp.where(own_head, jnp.concatenate([q] * ATT_HEAD_GROUP, axis=0),
                           jnp.zeros((), BF16))
            kr = k_ref[0, pl.ds(k0, NA_KH * GRID_W), lanes]
            vr = v_ref[0, pl.ds(k0, NA_KH * GRID_W), lanes]
            bias = bias_ref[r - rs, grp * group_rows:(grp + 1) * group_rows, :]
            s_nb = lax.dot_general(qs, kr, nt, preferred_element_type=F32) + bias
            s_cx = lax.dot_general(qs, kc_ref[0, :, lanes], nt, preferred_element_type=F32)
            m = jnp.maximum(jnp.max(s_nb, axis=-1, keepdims=True),
                            jnp.max(s_cx, axis=-1, keepdims=True))
            p_nb = jnp.exp(s_nb - m)
            p_cx = jnp.exp(s_cx - m)
            denom = jnp.sum(p_nb, axis=-1, keepdims=True) + jnp.sum(p_cx, axis=-1, keepdims=True)
            o = (jnp.dot(p_nb.astype(BF16), vr, preferred_element_type=F32)
                 + jnp.dot(p_cx.astype(BF16), vc_ref[0, :, lanes],
                           preferred_element_type=F32)) / denom
            o = jnp.where(own_head, o, 0.0)
            out = o[:GRID_W]
            for h in range(1, ATT_HEAD_GROUP):
                out = out + o[h * GRID_W:(h + 1) * GRID_W]
            o_ref[0, pl.ds(q0, GRID_W), lanes] = out.astype(BF16)
        return carry

    lax.fori_loop(0, ATT_ROWS, one_row, 0, unroll=True)


def _natten(q, k, v, k_c, v_c, bias):
    b, s, w = q.shape
    l = k_c.shape[1]
    grid_rows = s // GRID_W
    tq = ATT_ROWS * GRID_W
    full = pl.BlockSpec((1, s, w), lambda i, j: (i, 0, 0), pipeline_mode=pl.Buffered(1))
    ctx = pl.BlockSpec((1, l, w), lambda i, j: (i, 0, 0))
    return pl.pallas_call(
        functools.partial(_natten_kernel, grid_rows=grid_rows),
        grid=(b, grid_rows // ATT_ROWS),
        in_specs=[pl.BlockSpec((1, tq, w), lambda i, j: (i, j, 0)),
                  full, full, ctx, ctx,
                  pl.BlockSpec(bias.shape, lambda i, j: (0, 0, 0), pipeline_mode=pl.Buffered(1))],
        out_specs=pl.BlockSpec((1, tq, w), lambda i, j: (i, j, 0)),
        out_shape=jax.ShapeDtypeStruct((b, s, w), BF16),
        compiler_params=_params("parallel", "arbitrary"),
        name="natten",
    )(q, k, v, k_c, v_c, bias)


def _mix_out_kernel(att_ref, sgu_ref, x_ref, wo_ref, g1_ref, sh_ref, sc_ref, lng_ref, lnb_ref,
                    wr_ref, br_ref, x1_ref, t_ref, lg_ref):
    mix = (jnp.dot(att_ref[0], wo_ref[:NA_WIDTH, :], preferred_element_type=F32)
           + jnp.dot(sgu_ref[0], wo_ref[NA_WIDTH:, :], preferred_element_type=F32))
    x1 = _normalize(DEEPNORM_ALPHA * x_ref[0] + g1_ref[0] * mix) * lng_ref[...] + lnb_ref[...]
    x1_ref[0] = x1
    t = _normalize(x1) * (1.0 + sc_ref[0]) + sh_ref[0]
    rows = t.shape[0]
    for s in range(t.shape[1] // LANES):
        t_ref[pl.ds(s, rows, stride=SUBLANES), :] = t[:, s * LANES:(s + 1) * LANES]
    lg_ref[...] = _dot_split3(wr_ref[...], t, (((1,), (1,)), ((), ()))) + br_ref[...]


def _mix_out(att, sgu, x, w_out, g1, sh2, sc2, ln_g, ln_b, router_w, router_b):
    b, s, d = x.shape
    assert d == SUBLANES * LANES
    tm = min(ROW_TILE, s)
    nj = s // tm
    row = lambda width: pl.BlockSpec((1, tm, width), lambda i, j: (i, j, 0))
    mod_spec = pl.BlockSpec((1, 1, d), lambda i, j: (i, 0, 0))
    vec_spec = pl.BlockSpec((1, d), lambda i, j: (0, 0))
    return pl.pallas_call(
        _mix_out_kernel,
        grid=(b, nj),
        in_specs=[row(NA_WIDTH), row(SGU_WIDTH), row(d),
                  pl.BlockSpec((d, d), lambda i, j: (0, 0)),
                  mod_spec, mod_spec, mod_spec, vec_spec, vec_spec,
                  pl.BlockSpec((N_EXPERTS, d), lambda i, j: (0, 0)),
                  pl.BlockSpec((N_EXPERTS, 1), lambda i, j: (0, 0))],
        out_specs=[row(d),
                   pl.BlockSpec((tm * SUBLANES, LANES), lambda i, j: (i * nj + j, 0)),
                   pl.BlockSpec((N_EXPERTS, tm), lambda i, j: (0, i * nj + j))],
        out_shape=[jax.ShapeDtypeStruct((b, s, d), F32),
                   jax.ShapeDtypeStruct((b * s * SUBLANES, LANES), F32),
                   jax.ShapeDtypeStruct((N_EXPERTS, b * s), F32)],
        compiler_params=_params("parallel", "parallel"),
        name="mix_out",
    )(att, sgu, x, w_out, g1.reshape(b, 1, d), sh2.reshape(b, 1, d), sc2.reshape(b, 1, d),
      ln_g.reshape(1, d), ln_b.reshape(1, d), router_w.T, router_b.reshape(N_EXPERTS, 1))


def _route_kernel(lg_ref, e_ref, r_ref, g_ref, cnt_ref, run_ref):
    @pl.when(pl.program_id(0) == 0)
    def _():
        run_ref[...] = jnp.zeros_like(run_ref)

    logits = lg_ref[...]
    n_tok = logits.shape[1]
    expert = lax.broadcasted_iota(jnp.int32, logits.shape, 0)
    work = logits
    picks, tops = [], []
    for kk in range(TOP_K):
        m = jnp.max(work, axis=0, keepdims=True)
        first = jnp.min(jnp.where(work == m, expert, N_EXPERTS), axis=0, keepdims=True)
        pick = expert == first
        work = jnp.where(pick, -jnp.inf, work)
        picks.append(pick)
        tops.append(m)
        e_ref[kk:kk + 1, :] = first
    weights = [jnp.exp(m - tops[0]) for m in tops]
    denom = weights[0]
    for w in weights[1:]:
        denom = denom + w
    for kk in range(TOP_K):
        g_ref[kk:kk + 1, :] = weights[kk] / denom

    chosen = picks[0]
    for pick in picks[1:]:
        chosen = jnp.logical_or(chosen, pick)
    tri = (lax.broadcasted_iota(jnp.int32, (SCAN_BLOCK, SCAN_BLOCK), 0)
           <= lax.broadcasted_iota(jnp.int32, (SCAN_BLOCK, SCAN_BLOCK), 1)).astype(BF16)
    sel = jnp.where(chosen, 1.0, 0.0).astype(BF16)
    carry = run_ref[:, 0:1]
    for blk in range(n_tok // SCAN_BLOCK):
        lo = blk * SCAN_BLOCK
        run = jnp.dot(sel[:, lo:lo + SCAN_BLOCK], tri, preferred_element_type=F32) + carry
        for kk in range(TOP_K):
            rank = jnp.sum(jnp.where(picks[kk][:, lo:lo + SCAN_BLOCK], run - 1.0, 0.0),
                           axis=0, keepdims=True)
            r_ref[kk:kk + 1, lo:lo + SCAN_BLOCK] = rank.astype(jnp.int32)
        carry = run[:, SCAN_BLOCK - 1:SCAN_BLOCK]
    run_ref[...] = jnp.broadcast_to(carry, run_ref.shape)
    cnt_ref[...] = jnp.broadcast_to(carry, cnt_ref.shape).astype(jnp.int32)


def _route(logits_t):
    n = logits_t.shape[1]
    chunk = min(ROUTE_CHUNK, n)
    assert n % chunk == 0 and chunk % SCAN_BLOCK == 0
    per_k = pl.BlockSpec((TOP_K, chunk), lambda c: (0, c))
    return pl.pallas_call(
        _route_kernel,
        grid=(n // chunk,),
        in_specs=[pl.BlockSpec((N_EXPERTS, chunk), lambda c: (0, c))],
        out_specs=[per_k, per_k, per_k, pl.BlockSpec((N_EXPERTS, LANES), lambda c: (0, 0))],
        out_shape=[jax.ShapeDtypeStruct((TOP_K, n), jnp.int32),
                   jax.ShapeDtypeStruct((TOP_K, n), jnp.int32),
                   jax.ShapeDtypeStruct((TOP_K, n), F32),
                   jax.ShapeDtypeStruct((N_EXPERTS, LANES), jnp.int32)],
        scratch_shapes=[pltpu.VMEM((N_EXPERTS, LANES), F32)],
        compiler_params=_params("arbitrary"),
        name="route",
    )(logits_t)


def _plan(expert_k, rank_k, counts):
    padded = (counts + EXPERT_ROWS - 1) // EXPERT_ROWS * EXPERT_ROWS
    group_end = jnp.cumsum(padded)
    group_start = group_end - padded
    start_k = jnp.sum(jnp.where(expert_k[None] == jnp.arange(N_EXPERTS)[:, None, None],
                                group_start[:, None, None], 0), axis=0)
    slot = (start_k + rank_k).reshape(-1)
    tile_start = jnp.concatenate([group_start, group_end[-1:]]) // EXPERT_ROWS
    return (slot.astype(jnp.int32), tile_start.astype(jnp.int32),
            group_start.astype(jnp.int32), counts.astype(jnp.int32), padded.astype(jnp.int32))


def _dispatch_kernel(slot_ref, start_ref, cnt_ref, pad_ref, nt_ref, t_ref, xs_ref, zero_ref,
                     sem, zsem):
    tokens = t_ref.shape[0] // SUBLANES
    n_tokens = tokens * pl.num_programs(0)
    base = pl.program_id(0) * tokens
    tile_rows = zero_ref.shape[0]
    n_tiles_max = xs_ref.shape[0] // tile_rows

    def row_copy(src, row, sem_):
        dst = xs_ref.at[pl.ds(pl.multiple_of(row * SUBLANES, SUBLANES), SUBLANES), :]
        return pltpu.make_async_copy(src, dst, sem_)

    def tile_copy(tile):
        dst = xs_ref.at[pl.ds(pl.multiple_of(tile * tile_rows, tile_rows), tile_rows), :]
        return pltpu.make_async_copy(zero_ref, dst, zsem)

    @pl.when(pl.program_id(0) == 0)
    def _():
        zero_ref[...] = jnp.zeros_like(zero_ref)

        def padded_tile(e):
            return (start_ref[e] + pad_ref[e]) // (tile_rows // SUBLANES) - 1

        def fill_group(e, carry):
            @pl.when(pad_ref[e] > cnt_ref[e])
            def _():
                tile_copy(padded_tile(e)).start()
            return carry

        def drain_group(e, carry):
            @pl.when(pad_ref[e] > cnt_ref[e])
            def _():
                tile_copy(padded_tile(e)).wait()
            return carry

        lax.fori_loop(0, N_EXPERTS, fill_group, 0)
        lax.fori_loop(0, N_EXPERTS, drain_group, 0)

        def fill_tile(i, carry):
            tile_copy(i).start()
            return carry

        def drain_tile(i, carry):
            tile_copy(i).wait()
            return carry

        lax.fori_loop(nt_ref[0], n_tiles_max, fill_tile, 0)
        lax.fori_loop(nt_ref[0], n_tiles_max, drain_tile, 0)

    def scatter(i, carry):
        src = t_ref.at[pl.ds(pl.multiple_of(i * SUBLANES, SUBLANES), SUBLANES), :]
        for kk in range(TOP_K):
            row_copy(src, slot_ref[kk * n_tokens + base + i], sem).start(priority=kk % 2)
        return carry

    lax.fori_loop(0, tokens, scatter, 0, unroll=8)
    for kk in range(TOP_K):
        pltpu.make_async_copy(t_ref, xs_ref.at[pl.ds(0, tokens * SUBLANES), :], sem).wait()


def _dispatch(slot, group_start, counts, padded, n_tiles, t_tiles, n_rows):
    n = t_tiles.shape[0] // SUBLANES
    tokens = min(DISPATCH_TOKENS, n)
    assert n % tokens == 0 and n_rows % EXPERT_ROWS == 0
    grid_spec = pltpu.PrefetchScalarGridSpec(
        num_scalar_prefetch=5,
        grid=(n // tokens,),
        in_specs=[pl.BlockSpec((tokens * SUBLANES, LANES), lambda j, *_: (j, 0))],
        out_specs=pl.BlockSpec(memory_space=pl.ANY),
        scratch_shapes=[pltpu.VMEM((EXPERT_ROWS * SUBLANES, LANES), F32),
                        pltpu.SemaphoreType.DMA, pltpu.SemaphoreType.DMA],
    )
    return pl.pallas_call(
        _dispatch_kernel,
        grid_spec=grid_spec,
        out_shape=jax.ShapeDtypeStruct((n_rows * SUBLANES, LANES), F32),
        compiler_params=pltpu.CompilerParams(dimension_semantics=("arbitrary",),
                                             vmem_limit_bytes=VMEM_LIMIT_BYTES,
                                             has_side_effects=True),
        name="dispatch",
    )(slot, group_start, counts, padded, n_tiles, t_tiles)


def _experts_kernel(ts_ref, xs_ref, wgu_ref, bgu_ref, wd_ref, bd_ref, ys_ref,
                    wgu_bf, wd_bf, xbuf, ybuf, xsem, ysem):
    e = pl.program_id(0)
    tile_rows = xbuf.shape[1]
    rows = tile_rows // SUBLANES
    d_ff = wd_ref.shape[1]
    n_blocks = wgu_ref.shape[1] // LANES
    n_total = ts_ref[N_EXPERTS]

    def hbm_tile(ref, t):
        return ref.at[pl.ds(pl.multiple_of(t * tile_rows, tile_rows), tile_rows), :]

    def x_copy(t, which):
        return pltpu.make_async_copy(hbm_tile(xs_ref, t), xbuf.at[which], xsem.at[which])

    def y_copy(t, which):
        return pltpu.make_async_copy(ybuf.at[which], hbm_tile(ys_ref, t), ysem.at[which])

    @pl.when(e == 0)
    def _():
        x_copy(0, 0).start()

    @pl.when(ts_ref[e] < ts_ref[e + 1])
    def _():
        wgu_bf[...] = wgu_ref[0].astype(BF16)
        wd_bf[...] = wd_ref[0].astype(BF16)

        def one_tile(t, carry):
            cur = t % 2

            @pl.when(t + 1 < n_total)
            def _():
                x_copy(t + 1, 1 - cur).start(priority=1)

            x_copy(t, cur).wait()

            @pl.when(t >= 2)
            def _():
                y_copy(t - 2, cur).wait()

            x = jnp.concatenate(
                [xbuf[cur, pl.ds(s, rows, stride=SUBLANES), :] for s in range(n_blocks)],
                axis=1).astype(BF16)
            gu = jnp.dot(x, wgu_bf[...], preferred_element_type=F32) + bgu_ref[0]
            gate = jnp.minimum(gu[:, :d_ff], SWIGLU_LIMIT)
            lin = jnp.clip(gu[:, d_ff:], -SWIGLU_LIMIT, SWIGLU_LIMIT)
            act = ((lin + 1.0) * (gate * jax.nn.sigmoid(SWIGLU_ALPHA * gate))).astype(BF16)
            y = jnp.dot(act, wd_bf[...], preferred_element_type=F32) + bd_ref[0]
            for s in range(n_blocks):
                ybuf[cur, pl.ds(s, rows, stride=SUBLANES), :] = y[:, s * LANES:(s + 1) * LANES]
            y_copy(t, cur).start(priority=1)
            return carry

        lax.fori_loop(ts_ref[e], ts_ref[e + 1], one_tile, 0)

    @pl.when(e == pl.num_programs(0) - 1)
    def _():
        y_copy(n_total - 1, (n_total - 1) % 2).wait()

        @pl.when(n_total >= 2)
        def _():
            y_copy(n_total - 2, n_total % 2).wait()


def _experts(tile_start, xs, w_gu, b_gu, w_down, b_down):
    n_exp, d, two_ff = w_gu.shape
    d_ff = two_ff // 2
    by_expert = lambda e, ts: (e, 0, 0)
    tile_rows = EXPERT_ROWS * SUBLANES
    grid_spec = pltpu.PrefetchScalarGridSpec(
        num_scalar_prefetch=1,
        grid=(n_exp,),
        in_specs=[pl.BlockSpec(memory_space=pl.ANY),
                  pl.BlockSpec((1, d, two_ff), by_expert),
                  pl.BlockSpec((1, 1, two_ff), by_expert),
                  pl.BlockSpec((1, d_ff, d), by_expert),
                  pl.BlockSpec((1, 1, d), by_expert)],
        out_specs=pl.BlockSpec(memory_space=pl.ANY),
        scratch_shapes=[pltpu.VMEM((d, two_ff), BF16), pltpu.VMEM((d_ff, d), BF16),
                        pltpu.VMEM((2, tile_rows, LANES), F32),
                        pltpu.VMEM((2, tile_rows, LANES), F32),
                        pltpu.SemaphoreType.DMA((2,)), pltpu.SemaphoreType.DMA((2,))],
    )
    return pl.pallas_call(
        _experts_kernel,
        grid_spec=grid_spec,
        out_shape=jax.ShapeDtypeStruct(xs.shape, F32),
        input_output_aliases={1: 0},
        compiler_params=_params("arbitrary"),
        name="experts",
    )(tile_start, xs, w_gu, b_gu.reshape(n_exp, 1, two_ff), w_down, b_down.reshape(n_exp, 1, d))


def _combine_kernel(slot_ref, gate_ref, ys_ref, x1_ref, g2_ref, lng_ref, lnb_ref, o_ref,
                    buf, acc_ref, sem):
    j = pl.program_id(0)
    tokens = o_ref.shape[0]
    n_tokens = tokens * pl.num_programs(0)
    n_blocks = o_ref.shape[1] // LANES
    tile_rows = TOP_K * SUBLANES
    cur = j % 2

    def gather_token(block, which, i):
        for kk in range(TOP_K):
            row = slot_ref[kk * n_tokens + block * tokens + i]
            src = ys_ref.at[pl.ds(pl.multiple_of(row * SUBLANES, SUBLANES), SUBLANES), :]
            dst = buf.at[which, pl.ds(pl.multiple_of((i * TOP_K + kk) * SUBLANES, SUBLANES),
                                      SUBLANES), :]
            pltpu.make_async_copy(src, dst, sem.at[which]).start(priority=kk % 2)

    def reduce_token(i):
        tile = None
        for kk in range(TOP_K):
            row0 = pl.multiple_of((i * TOP_K + kk) * SUBLANES, SUBLANES)
            part = gate_ref[kk * n_tokens + j * tokens + i] * buf[cur, pl.ds(row0, SUBLANES), :]
            tile = part if tile is None else tile + part
        acc_ref[pl.ds(pl.multiple_of(i * SUBLANES, SUBLANES), SUBLANES), :] = tile

    @pl.when(j == 0)
    def _():
        def first(i, carry):
            gather_token(0, 0, i)
            return carry
        lax.fori_loop(0, tokens, first, 0, unroll=8)

    pltpu.make_async_copy(ys_ref.at[pl.ds(0, tokens * tile_rows), :], buf.at[cur],
                          sem.at[cur]).wait()

    @pl.when(j + 1 < pl.num_programs(0))
    def _():
        def step(i, carry):
            gather_token(j + 1, 1 - cur, i)
            reduce_token(i)
            return carry
        lax.fori_loop(0, tokens, step, 0, unroll=8)

    @pl.when(j + 1 == pl.num_programs(0))
    def _():
        def step(i, carry):
            reduce_token(i)
            return carry
        lax.fori_loop(0, tokens, step, 0, unroll=8)

    ffn = jnp.concatenate([acc_ref[pl.ds(s, tokens, stride=SUBLANES), :] for s in range(n_blocks)],
                          axis=1)
    z = DEEPNORM_ALPHA * x1_ref[...] + g2_ref[0] * ffn
    o_ref[...] = _normalize(z) * lng_ref[...] + lnb_ref[...]


def _combine(slot, gates_flat, ys, x1, g2, ln_g, ln_b, tokens_per_sample):
    n, d = x1.shape
    tokens = min(COMBINE_TOKENS, tokens_per_sample)
    assert tokens_per_sample % tokens == 0
    blocks_per_sample = tokens_per_sample // tokens
    vec = pl.BlockSpec((1, d), lambda j, *_: (0, 0))
    grid_spec = pltpu.PrefetchScalarGridSpec(
        num_scalar_prefetch=2,
        grid=(n // tokens,),
        in_specs=[pl.BlockSpec(memory_space=pl.ANY),
                  pl.BlockSpec((tokens, d), lambda j, *_: (j, 0)),
                  pl.BlockSpec((1, 1, d), lambda j, *_: (j // blocks_per_sample, 0, 0)),
                  vec, vec],
        out_specs=pl.BlockSpec((tokens, d), lambda j, *_: (j, 0)),
        scratch_shapes=[pltpu.VMEM((2, tokens * TOP_K * SUBLANES, LANES), F32),
                        pltpu.VMEM((tokens * SUBLANES, LANES), F32),
                        pltpu.SemaphoreType.DMA((2,))],
    )
    return pl.pallas_call(
        _combine_kernel,
        grid_spec=grid_spec,
        out_shape=jax.ShapeDtypeStruct((n, d), F32),
        compiler_params=_params("arbitrary"),
        name="combine",
    )(slot, gates_flat, ys, x1, g2, ln_g.reshape(1, d), ln_b.reshape(1, d))


def _layer(x, c, ctx, c_ctx, ada_w, ada_b, w_in, rpb, sgu_ln_g, sgu_ln_b, sgu_w, sgu_b, w_out,
           ln1_g, ln1_b, ln2_g, ln2_b, router_w, router_b, w_gu, b_gu, w_down, b_down):
    b, s, d = x.shape
    n = b * s
    assert s % (GRID_W * ATT_ROWS) == 0 and s // GRID_W >= NA_KH

    cond_rows = jnp.zeros((8, d), F32).at[:b].set(c).at[b].set(c_ctx)
    mod = _ada(cond_rows, ada_w, ada_b)
    sh1, sc1, g1, sh2, sc2, g2 = jnp.split(mod[:b], 6, axis=-1)
    csh1, csc1 = mod[b, :d], mod[b, d:2 * d]

    w_in_bf = w_in.astype(BF16)
    k_c, v_c = _ctx_kv(ctx, csh1, csc1, w_in_bf[:, NA_WIDTH:3 * NA_WIDTH])
    q, k, v, sgu = _proj_in(x, sh1, sc1, w_in_bf, sgu_ln_g, sgu_ln_b, sgu_w, sgu_b)
    att = _natten(q, k, v, k_c, v_c, _bias_table(rpb))
    x1, t_tiles, logits_t = _mix_out(att, sgu, x, w_out.astype(BF16), g1, sh2, sc2, ln1_g, ln1_b,
                                     router_w, router_b)

    expert_k, rank_k, gate_k, counts = _route(logits_t)
    n_tiles_max = n * TOP_K // EXPERT_ROWS + N_EXPERTS
    slot, tile_start, group_start, counts, padded = _plan(expert_k, rank_k, counts[:, 0])
    xs = _dispatch(slot, group_start, counts, padded, tile_start[N_EXPERTS:], t_tiles,
                   n_tiles_max * EXPERT_ROWS)
    ys = _experts(tile_start, xs, w_gu, b_gu, w_down, b_down)
    out = _combine(slot, gate_k.reshape(-1), ys, x1.reshape(n, d), g2.reshape(b, 1, d),
                   ln2_g, ln2_b, s)
    return out.reshape(b, s, d)


def kernel(x, c, ctx, c_ctx, ada_w, ada_b, w_in, rpb, sgu_ln_g, sgu_ln_b, sgu_w, sgu_b, w_out,
           ln1_g, ln1_b, ln2_g, ln2_b, router_w, router_b, exp_w_gu, exp_b_gu, exp_w_down,
           exp_b_down):
    assert ada_w.shape[0] == DEPTH
    return _layer(x, c, ctx, c_ctx, ada_w[0], ada_b[0], w_in[0], rpb[0], sgu_ln_g[0], sgu_ln_b[0],
                  sgu_w[0], sgu_b[0], w_out[0], ln1_g[0], ln1_b[0], ln2_g[0], ln2_b[0],
                  router_w[0], router_b[0], exp_w_gu[0], exp_b_gu[0], exp_w_down[0], exp_b_down[0])
```

```python
import functools

import jax
import jax.numpy as jnp
from jax import lax
from jax.experimental import pallas as pl
from jax.experimental.pallas import tpu as pltpu

F32 = jnp.float32
BF16 = jnp.bfloat16

GRID_W = 64
NA_HEADS = 8
NA_HEAD_DIM = 64
NA_WIDTH = NA_HEADS * NA_HEAD_DIM
NA_KH = 8
NA_KW = 16
SGU_GROUPS = 4
SGU_GROUP_DIM = 128
SGU_WIDTH = SGU_GROUPS * SGU_GROUP_DIM
CHUNK = 128
N_EXPERTS = 32
TOP_K = 4
SWIGLU_LIMIT = 7.0
SWIGLU_ALPHA = 1.702
LN_EPS = 1e-5
DEPTH = 1
DEEPNORM_ALPHA = (2.0 * DEPTH) ** 0.25
MASKED = -1e30

SUBLANES = 8
LANES = 128
VMEM_LIMIT_BYTES = 52 * 1024 * 1024

ROW_TILE = 1024
ATT_ROWS = 8
ATT_HEAD_GROUP = 4
ROUTE_CHUNK = 1024
SCAN_BLOCK = 256
EXPERT_ROWS = 512
DISPATCH_TOKENS = 4096
COMBINE_TOKENS = 512


def _params(*sem):
    return pltpu.CompilerParams(dimension_semantics=sem, vmem_limit_bytes=VMEM_LIMIT_BYTES)


def _dot_split3(a, b, dims):
    a_hi = a.astype(BF16)
    a_lo = (a - a_hi.astype(F32)).astype(BF16)
    b_hi = b.astype(BF16)
    b_lo = (b - b_hi.astype(F32)).astype(BF16)
    dot = functools.partial(lax.dot_general, dimension_numbers=dims, preferred_element_type=F32)
    return dot(a_hi, b_hi) + dot(a_hi, b_lo) + dot(a_lo, b_hi)


def _normalize(x):
    mu = jnp.mean(x, axis=-1, keepdims=True)
    xc = x - mu
    var = jnp.mean(xc * xc, axis=-1, keepdims=True)
    return xc * lax.rsqrt(var + LN_EPS)


def _ada_kernel(c_ref, w_ref, b_ref, o_ref):
    s = c_ref[...]
    s = s * jax.nn.sigmoid(s)
    o_ref[...] = _dot_split3(s, w_ref[...], (((1,), (0,)), ((), ()))) + b_ref[...]


def _ada(cond_rows, ada_w, ada_b):
    d = cond_rows.shape[1]
    n_out = ada_w.shape[1]
    return pl.pallas_call(
        _ada_kernel,
        grid=(n_out // d,),
        in_specs=[pl.BlockSpec((8, d), lambda j: (0, 0)),
                  pl.BlockSpec((d, d), lambda j: (0, j)),
                  pl.BlockSpec((1, d), lambda j: (0, j))],
        out_specs=pl.BlockSpec((8, d), lambda j: (0, j)),
        out_shape=jax.ShapeDtypeStruct((8, n_out), F32),
        compiler_params=_params("arbitrary"),
        name="ada",
    )(cond_rows, ada_w, ada_b.reshape(1, n_out))


def _ctx_kv_kernel(ctx_ref, sh_ref, sc_ref, w_ref, k_ref, v_ref):
    h = _normalize(ctx_ref[0]) * (1.0 + sc_ref[...]) + sh_ref[...]
    kv = jnp.dot(h.astype(BF16), w_ref[...], preferred_element_type=F32)
    k_ref[0] = kv[:, :NA_WIDTH].astype(BF16)
    v_ref[0] = kv[:, NA_WIDTH:].astype(BF16)


def _ctx_kv(ctx, csh1, csc1, w_kv):
    b, l, d = ctx.shape
    out = jax.ShapeDtypeStruct((b, l, NA_WIDTH), BF16)
    return pl.pallas_call(
        _ctx_kv_kernel,
        grid=(b,),
        in_specs=[pl.BlockSpec((1, l, d), lambda i: (i, 0, 0)),
                  pl.BlockSpec((1, d), lambda i: (0, 0)),
                  pl.BlockSpec((1, d), lambda i: (0, 0)),
                  pl.BlockSpec((d, 2 * NA_WIDTH), lambda i: (0, 0))],
        out_specs=[pl.BlockSpec((1, l, NA_WIDTH), lambda i: (i, 0, 0)),
                   pl.BlockSpec((1, l, NA_WIDTH), lambda i: (i, 0, 0))],
        out_shape=[out, out],
        compiler_params=_params("arbitrary"),
        name="ctx_kv",
    )(ctx, csh1.reshape(1, d), csc1.reshape(1, d), w_kv)


def _proj_in_kernel(x_ref, sh_ref, sc_ref, w_ref, lng_ref, lnb_ref, ws_ref, bs_ref,
                    q_ref, k_ref, v_ref, s_ref):
    h = (_normalize(x_ref[0]) * (1.0 + sc_ref[0]) + sh_ref[0]).astype(BF16)

    def proj(lo, width):
        return jnp.dot(h, w_ref[:, lo:lo + width], preferred_element_type=F32)

    q_ref[0] = (proj(0, NA_WIDTH) * (NA_HEAD_DIM ** -0.5)).astype(BF16)
    k_ref[0] = proj(NA_WIDTH, NA_WIDTH).astype(BF16)
    v_ref[0] = proj(2 * NA_WIDTH, NA_WIDTH).astype(BF16)
    u = jax.nn.gelu(proj(3 * NA_WIDTH, SGU_WIDTH))
    g = jax.nn.gelu(proj(3 * NA_WIDTH + SGU_WIDTH, SGU_WIDTH))
    gn = (_normalize(g) * lng_ref[...] + lnb_ref[...]).astype(BF16)
    rows = h.shape[0]
    for n in range(rows // CHUNK):
        r0 = n * CHUNK
        for grp in range(SGU_GROUPS):
            c0 = grp * SGU_GROUP_DIM
            mixed = jnp.dot(ws_ref[grp], gn[r0:r0 + CHUNK, c0:c0 + SGU_GROUP_DIM],
                            preferred_element_type=F32) + bs_ref[grp]
            s_ref[0, r0:r0 + CHUNK, c0:c0 + SGU_GROUP_DIM] = (
                u[r0:r0 + CHUNK, c0:c0 + SGU_GROUP_DIM] * mixed).astype(BF16)


def _proj_in(x, sh1, sc1, w_in, sgu_ln_g, sgu_ln_b, sgu_w, sgu_b):
    b, s, d = x.shape
    d_in = w_in.shape[1]
    tm = min(ROW_TILE, s)
    out = jax.ShapeDtypeStruct((b, s, NA_WIDTH), BF16)
    row_spec = pl.BlockSpec((1, tm, NA_WIDTH), lambda i, j: (i, j, 0))
    mod_spec = pl.BlockSpec((1, 1, d), lambda i, j: (i, 0, 0))
    bs = jnp.broadcast_to(sgu_b[:, :, None], (SGU_GROUPS, CHUNK, SGU_GROUP_DIM))
    return pl.pallas_call(
        _proj_in_kernel,
        grid=(b, s // tm),
        in_specs=[pl.BlockSpec((1, tm, d), lambda i, j: (i, j, 0)),
                  mod_spec, mod_spec,
                  pl.BlockSpec((d, d_in), lambda i, j: (0, 0)),
                  pl.BlockSpec((1, SGU_WIDTH), lambda i, j: (0, 0)),
                  pl.BlockSpec((1, SGU_WIDTH), lambda i, j: (0, 0)),
                  pl.BlockSpec((SGU_GROUPS, CHUNK, CHUNK), lambda i, j: (0, 0, 0)),
                  pl.BlockSpec((SGU_GROUPS, CHUNK, SGU_GROUP_DIM), lambda i, j: (0, 0, 0))],
        out_specs=[row_spec, row_spec, row_spec, row_spec],
        out_shape=[out, out, out, out],
        compiler_params=_params("parallel", "parallel"),
        name="proj_in",
    )(x, sh1.reshape(b, 1, d), sc1.reshape(b, 1, d), w_in,
      sgu_ln_g.reshape(1, SGU_WIDTH), sgu_ln_b.reshape(1, SGU_WIDTH), sgu_w.astype(BF16), bs)


def _bias_table(rpb):
    heads, n_dr, n_dc = rpb.shape
    c = jnp.arange(GRID_W)
    cs = jnp.clip(c - NA_KW // 2, 0, GRID_W - NA_KW)
    kc = jnp.arange(GRID_W)
    valid = (kc[None, :] >= cs[:, None]) & (kc[None, :] < cs[:, None] + NA_KW)
    lead = GRID_W - NA_KW
    padded = jnp.pad(rpb.astype(F32), ((0, 0), (0, 0), (lead, 2 * GRID_W - lead - n_dc)))
    flat = jnp.tile(padded, (1, 1, GRID_W))[:, :, :GRID_W * (2 * GRID_W - 1)]
    toe = flat.reshape(heads, n_dr, GRID_W, 2 * GRID_W - 1)[..., GRID_W - 1:]
    toe = jnp.where(valid, toe, MASKED)
    tab = jnp.stack([
        jnp.concatenate([toe[:, NA_KH - 1 - o + i] for i in range(NA_KH)], axis=-1)
        for o in range(NA_KH)])
    return tab.reshape(NA_KH, NA_HEADS * GRID_W, NA_KH * GRID_W)


def _natten_kernel(q_ref, k_ref, v_ref, kc_ref, vc_ref, bias_ref, o_ref, *, grid_rows):
    group_lanes = ATT_HEAD_GROUP * NA_HEAD_DIM
    group_rows = ATT_HEAD_GROUP * GRID_W
    row_head = lax.broadcasted_iota(jnp.int32, (group_rows, group_lanes), 0) // GRID_W
    lane_head = lax.broadcasted_iota(jnp.int32, (group_rows, group_lanes), 1) // NA_HEAD_DIM
    own_head = row_head == lane_head
    nt = (((1,), (1,)), ((), ()))

    def one_row(i, carry):
        r = pl.program_id(1) * ATT_ROWS + i
        rs = jnp.clip(r - NA_KH // 2, 0, grid_rows - NA_KH)
        k0 = pl.multiple_of(rs * GRID_W, GRID_W)
        q0 = pl.multiple_of(i * GRID_W, GRID_W)
        for grp in range(NA_HEADS // ATT_HEAD_GROUP):
            lanes = slice(grp * group_lanes, (grp + 1) * group_lanes)
            q = q_ref[0, pl.ds(q0, GRID_W), lanes]
            qs = jnp.where(own_head, jnp.concatenate([q] * ATT_HEAD_GROUP, axis=0),
                           jnp.zeros((), BF16))
            kr = k_ref[0, pl.ds(k0, NA_KH * GRID_W), lanes]
            vr = v_ref[0, pl.ds(k0, NA_KH * GRID_W), lanes]
            bias = bias_ref[r - rs, grp * group_rows:(grp + 1) * group_rows, :]
            s_nb = lax.dot_general(qs, kr, nt, preferred_element_type=F32) + bias
            s_cx = lax.dot_general(qs, kc_ref[0, :, lanes], nt, preferred_element_type=F32)
            m = jnp.maximum(jnp.max(s_nb, axis=-1, keepdims=True),
                            jnp.max(s_cx, axis=-1, keepdims=True))
            p_nb = jnp.exp(s_nb - m)
            p_cx = jnp.exp(s_cx - m)
            denom = jnp.sum(p_nb, axis=-1, keepdims=True) + jnp.sum(p_cx, axis=-1, keepdims=True)
            o = (jnp.dot(p_nb.astype(BF16), vr, preferred_element_type=F32)
                 + jnp.dot(p_cx.astype(BF16), vc_ref[0, :, lanes],
                           preferred_element_type=F32)) / denom
            o = jnp.where(own_head, o, 0.0)
            out = o[:GRID_W]
            for h in range(1, ATT_HEAD_GROUP):
                out = out + o[h * GRID_W:(h + 1) * GRID_W]
            o_ref[0, pl.ds(q0, GRID_W), lanes] = out.astype(BF16)
        return carry

    lax.fori_loop(0, ATT_ROWS, one_row, 0, unroll=True)


def _natten(q, k, v, k_c, v_c, bias):
    b, s, w = q.shape
    l = k_c.shape[1]
    grid_rows = s // GRID_W
    tq = ATT_ROWS * GRID_W
    full = pl.BlockSpec((1, s, w), lambda i, j: (i, 0, 0))
    ctx = pl.BlockSpec((1, l, w), lambda i, j: (i, 0, 0))
    return pl.pallas_call(
        functools.partial(_natten_kernel, grid_rows=grid_rows),
        grid=(b, grid_rows // ATT_ROWS),
        in_specs=[pl.BlockSpec((1, tq, w), lambda i, j: (i, j, 0)),
                  full, full, ctx, ctx,
                  pl.BlockSpec(bias.shape, lambda i, j: (0, 0, 0), pipeline_mode=pl.Buffered(1))],
        out_specs=pl.BlockSpec((1, tq, w), lambda i, j: (i, j, 0)),
        out_shape=jax.ShapeDtypeStruct((b, s, w), BF16),
        compiler_params=_params("parallel", "arbitrary"),
        name="natten",
    )(q, k, v, k_c, v_c, bias)


def _mix_out_kernel(att_ref, sgu_ref, x_ref, wo_ref, g1_ref, sh_ref, sc_ref, lng_ref, lnb_ref,
                    wr_ref, br_ref, x1_ref, t_ref, lg_ref):
    mix = (jnp.dot(att_ref[0], wo_ref[:NA_WIDTH, :], preferred_element_type=F32)
           + jnp.dot(sgu_ref[0], wo_ref[NA_WIDTH:, :], preferred_element_type=F32))
    x1 = _normalize(DEEPNORM_ALPHA * x_ref[0] + g1_ref[0] * mix) * lng_ref[...] + lnb_ref[...]
    x1_ref[0] = x1
    t = _normalize(x1) * (1.0 + sc_ref[0]) + sh_ref[0]
    rows = t.shape[0]
    for s in range(t.shape[1] // LANES):
        t_ref[pl.ds(s, rows, stride=SUBLANES), :] = t[:, s * LANES:(s + 1) * LANES]
    lg_ref[...] = _dot_split3(wr_ref[...], t, (((1,), (1,)), ((), ()))) + br_ref[...]


def _mix_out(att, sgu, x, w_out, g1, sh2, sc2, ln_g, ln_b, router_w, router_b):
    b, s, d = x.shape
    assert d == SUBLANES * LANES
    tm = min(ROW_TILE, s)
    nj = s // tm
    row = lambda width: pl.BlockSpec((1, tm, width), lambda i, j: (i, j, 0))
    mod_spec = pl.BlockSpec((1, 1, d), lambda i, j: (i, 0, 0))
    vec_spec = pl.BlockSpec((1, d), lambda i, j: (0, 0))
    return pl.pallas_call(
        _mix_out_kernel,
        grid=(b, nj),
        in_specs=[row(NA_WIDTH), row(SGU_WIDTH), row(d),
                  pl.BlockSpec((d, d), lambda i, j: (0, 0)),
                  mod_spec, mod_spec, mod_spec, vec_spec, vec_spec,
                  pl.BlockSpec((N_EXPERTS, d), lambda i, j: (0, 0)),
                  pl.BlockSpec((N_EXPERTS, 1), lambda i, j: (0, 0))],
        out_specs=[row(d),
                   pl.BlockSpec((tm * SUBLANES, LANES), lambda i, j: (i * nj + j, 0)),
                   pl.BlockSpec((N_EXPERTS, tm), lambda i, j: (0, i * nj + j))],
        out_shape=[jax.ShapeDtypeStruct((b, s, d), F32),
                   jax.ShapeDtypeStruct((b * s * SUBLANES, LANES), F32),
                   jax.ShapeDtypeStruct((N_EXPERTS, b * s), F32)],
        compiler_params=_params("parallel", "parallel"),
        name="mix_out",
    )(att, sgu, x, w_out, g1.reshape(b, 1, d), sh2.reshape(b, 1, d), sc2.reshape(b, 1, d),
      ln_g.reshape(1, d), ln_b.reshape(1, d), router_w.T, router_b.reshape(N_EXPERTS, 1))


def _route_kernel(lg_ref, e_ref, r_ref, g_ref, cnt_ref, run_ref):
    @pl.when(pl.program_id(0) == 0)
    def _():
        run_ref[...] = jnp.zeros_like(run_ref)

    logits = lg_ref[...]
    n_tok = logits.shape[1]
    expert = lax.broadcasted_iota(jnp.int32, logits.shape, 0)
    work = logits
    picks, tops = [], []
    for kk in range(TOP_K):
        m = jnp.max(work, axis=0, keepdims=True)
        first = jnp.min(jnp.where(work == m, expert, N_EXPERTS), axis=0, keepdims=True)
        pick = expert == first
        work = jnp.where(pick, -jnp.inf, work)
        picks.append(pick)
        tops.append(m)
        e_ref[kk:kk + 1, :] = first
    weights = [jnp.exp(m - tops[0]) for m in tops]
    denom = weights[0]
    for w in weights[1:]:
        denom = denom + w
    for kk in range(TOP_K):
        g_ref[kk:kk + 1, :] = weights[kk] / denom

    chosen = picks[0]
    for pick in picks[1:]:
        chosen = jnp.logical_or(chosen, pick)
    tri = (lax.broadcasted_iota(jnp.int32, (SCAN_BLOCK, SCAN_BLOCK), 0)
           <= lax.broadcasted_iota(jnp.int32, (SCAN_BLOCK, SCAN_BLOCK), 1)).astype(BF16)
    sel = jnp.where(chosen, 1.0, 0.0).astype(BF16)
    carry = run_ref[:, 0:1]
    for blk in range(n_tok // SCAN_BLOCK):
        lo = blk * SCAN_BLOCK
        run = jnp.dot(sel[:, lo:lo + SCAN_BLOCK], tri, preferred_element_type=F32) + carry
        for kk in range(TOP_K):
            rank = jnp.sum(jnp.where(picks[kk][:, lo:lo + SCAN_BLOCK], run - 1.0, 0.0),
                           axis=0, keepdims=True)
            r_ref[kk:kk + 1, lo:lo + SCAN_BLOCK] = rank.astype(jnp.int32)
        carry = run[:, SCAN_BLOCK - 1:SCAN_BLOCK]
    run_ref[...] = jnp.broadcast_to(carry, run_ref.shape)
    cnt_ref[...] = jnp.broadcast_to(carry, cnt_ref.shape).astype(jnp.int32)


def _route(logits_t):
    n = logits_t.shape[1]
    chunk = min(ROUTE_CHUNK, n)
    assert n % chunk == 0 and chunk % SCAN_BLOCK == 0
    per_k = pl.BlockSpec((TOP_K, chunk), lambda c: (0, c))
    return pl.pallas_call(
        _route_kernel,
        grid=(n // chunk,),
        in_specs=[pl.BlockSpec((N_EXPERTS, chunk), lambda c: (0, c))],
        out_specs=[per_k, per_k, per_k, pl.BlockSpec((N_EXPERTS, LANES), lambda c: (0, 0))],
        out_shape=[jax.ShapeDtypeStruct((TOP_K, n), jnp.int32),
                   jax.ShapeDtypeStruct((TOP_K, n), jnp.int32),
                   jax.ShapeDtypeStruct((TOP_K, n), F32),
                   jax.ShapeDtypeStruct((N_EXPERTS, LANES), jnp.int32)],
        scratch_shapes=[pltpu.VMEM((N_EXPERTS, LANES), F32)],
        compiler_params=_params("arbitrary"),
        name="route",
    )(logits_t)


def _plan(expert_k, rank_k, counts):
    padded = (counts + EXPERT_ROWS - 1) // EXPERT_ROWS * EXPERT_ROWS
    group_end = jnp.cumsum(padded)
    group_start = group_end - padded
    start_k = jnp.sum(jnp.where(expert_k[None] == jnp.arange(N_EXPERTS)[:, None, None],
                                group_start[:, None, None], 0), axis=0)
    slot = (start_k + rank_k).reshape(-1)
    tile_start = jnp.concatenate([group_start, group_end[-1:]]) // EXPERT_ROWS
    return (slot.astype(jnp.int32), tile_start.astype(jnp.int32),
            group_start.astype(jnp.int32), counts.astype(jnp.int32), padded.astype(jnp.int32))


def _dispatch_kernel(slot_ref, start_ref, cnt_ref, pad_ref, nt_ref, t_ref, xs_ref, zero_ref,
                     sem, zsem):
    tokens = t_ref.shape[0] // SUBLANES
    n_tokens = tokens * pl.num_programs(0)
    base = pl.program_id(0) * tokens
    tile_rows = zero_ref.shape[0]
    n_tiles_max = xs_ref.shape[0] // tile_rows

    def row_copy(src, row, sem_):
        dst = xs_ref.at[pl.ds(pl.multiple_of(row * SUBLANES, SUBLANES), SUBLANES), :]
        return pltpu.make_async_copy(src, dst, sem_)

    def tile_copy(tile):
        dst = xs_ref.at[pl.ds(pl.multiple_of(tile * tile_rows, tile_rows), tile_rows), :]
        return pltpu.make_async_copy(zero_ref, dst, zsem)

    @pl.when(pl.program_id(0) == 0)
    def _():
        zero_ref[...] = jnp.zeros_like(zero_ref)

        def padded_tile(e):
            return (start_ref[e] + pad_ref[e]) // (tile_rows // SUBLANES) - 1

        def fill_group(e, carry):
            @pl.when(pad_ref[e] > cnt_ref[e])
            def _():
                tile_copy(padded_tile(e)).start()
            return carry

        def drain_group(e, carry):
            @pl.when(pad_ref[e] > cnt_ref[e])
            def _():
                tile_copy(padded_tile(e)).wait()
            return carry

        lax.fori_loop(0, N_EXPERTS, fill_group, 0)
        lax.fori_loop(0, N_EXPERTS, drain_group, 0)

        def fill_tile(i, carry):
            tile_copy(i).start()
            return carry

        def drain_tile(i, carry):
            tile_copy(i).wait()
            return carry

        lax.fori_loop(nt_ref[0], n_tiles_max, fill_tile, 0)
        lax.fori_loop(nt_ref[0], n_tiles_max, drain_tile, 0)

    def scatter(i, carry):
        src = t_ref.at[pl.ds(pl.multiple_of(i * SUBLANES, SUBLANES), SUBLANES), :]
        for kk in range(TOP_K):
            row_copy(src, slot_ref[kk * n_tokens + base + i], sem).start(priority=kk % 2)
        return carry

    lax.fori_loop(0, tokens, scatter, 0, unroll=8)
    for kk in range(TOP_K):
        pltpu.make_async_copy(t_ref, xs_ref.at[pl.ds(0, tokens * SUBLANES), :], sem).wait()


def _dispatch(slot, group_start, counts, padded, n_tiles, t_tiles, n_rows):
    n = t_tiles.shape[0] // SUBLANES
    tokens = min(DISPATCH_TOKENS, n)
    assert n % tokens == 0 and n_rows % EXPERT_ROWS == 0
    grid_spec = pltpu.PrefetchScalarGridSpec(
        num_scalar_prefetch=5,
        grid=(n // tokens,),
        in_specs=[pl.BlockSpec((tokens * SUBLANES, LANES), lambda j, *_: (j, 0))],
        out_specs=pl.BlockSpec(memory_space=pl.ANY),
        scratch_shapes=[pltpu.VMEM((EXPERT_ROWS * SUBLANES, LANES), F32),
                        pltpu.SemaphoreType.DMA, pltpu.SemaphoreType.DMA],
    )
    return pl.pallas_call(
        _dispatch_kernel,
        grid_spec=grid_spec,
        out_shape=jax.ShapeDtypeStruct((n_rows * SUBLANES, LANES), F32),
        compiler_params=pltpu.CompilerParams(dimension_semantics=("arbitrary",),
                                             vmem_limit_bytes=VMEM_LIMIT_BYTES,
                                             has_side_effects=True),
        name="dispatch",
    )(slot, group_start, counts, padded, n_tiles, t_tiles)


def _experts_kernel(ts_ref, end_ref, xs_ref, wgu_ref, bgu_ref, wd_ref, bd_ref, ys_ref,
                    wgu_bf, wd_bf, xbuf, ybuf, xsem, ysem):
    e = pl.program_id(0)
    tile_rows = xbuf.shape[1]
    rows = tile_rows // SUBLANES
    d_ff = wd_ref.shape[1]
    n_blocks = wgu_ref.shape[1] // LANES
    n_total = ts_ref[N_EXPERTS]

    def hbm_tile(ref, t):
        return ref.at[pl.ds(pl.multiple_of(t * tile_rows, tile_rows), tile_rows), :]

    def x_copy(t, which):
        return pltpu.make_async_copy(hbm_tile(xs_ref, t), xbuf.at[which], xsem.at[which])

    def y_copy(t, which):
        return pltpu.make_async_copy(ybuf.at[which], hbm_tile(ys_ref, t), ysem.at[which])

    @pl.when(e == 0)
    def _():
        ybuf[...] = jnp.zeros_like(ybuf)
        x_copy(0, 0).start()

    @pl.when(ts_ref[e] < ts_ref[e + 1])
    def _():
        wgu_bf[...] = wgu_ref[0].astype(BF16)
        wd_bf[...] = wd_ref[0].astype(BF16)

        def one_tile(t, carry):
            cur = t % 2

            @pl.when(t + 1 < n_total)
            def _():
                x_copy(t + 1, 1 - cur).start(priority=1)

            x_copy(t, cur).wait()

            @pl.when(t >= 2)
            def _():
                y_copy(t - 2, cur).wait()

            def ffn(n_rows):
                x = jnp.concatenate(
                    [xbuf[cur, pl.ds(s, n_rows, stride=SUBLANES), :] for s in range(n_blocks)],
                    axis=1).astype(BF16)
                gu = jnp.dot(x, wgu_bf[...], preferred_element_type=F32) + bgu_ref[0]
                gate = jnp.minimum(gu[:, :d_ff], SWIGLU_LIMIT)
                lin = jnp.clip(gu[:, d_ff:], -SWIGLU_LIMIT, SWIGLU_LIMIT)
                act = ((lin + 1.0) * (gate * jax.nn.sigmoid(SWIGLU_ALPHA * gate))).astype(BF16)
                y = jnp.dot(act, wd_bf[...], preferred_element_type=F32) + bd_ref[0]
                for s in range(n_blocks):
                    ybuf[cur, pl.ds(s, n_rows, stride=SUBLANES), :] = y[:, s * LANES:(s + 1) * LANES]

            live = end_ref[e] - t * rows

            @pl.when(live > rows // 2)
            def _():
                ffn(rows)

            @pl.when(jnp.logical_and(live > rows // 4, live <= rows // 2))
            def _():
                ffn(rows // 2)

            @pl.when(live <= rows // 4)
            def _():
                ffn(rows // 4)

            y_copy(t, cur).start(priority=1)
            return carry

        lax.fori_loop(ts_ref[e], ts_ref[e + 1], one_tile, 0)

    @pl.when(e == pl.num_programs(0) - 1)
    def _():
        y_copy(n_total - 1, (n_total - 1) % 2).wait()

        @pl.when(n_total >= 2)
        def _():
            y_copy(n_total - 2, n_total % 2).wait()


def _experts(tile_start, token_end, xs, w_gu, b_gu, w_down, b_down):
    n_exp, d, two_ff = w_gu.shape
    d_ff = two_ff // 2
    by_expert = lambda e, ts, end: (e, 0, 0)
    tile_rows = EXPERT_ROWS * SUBLANES
    grid_spec = pltpu.PrefetchScalarGridSpec(
        num_scalar_prefetch=2,
        grid=(n_exp,),
        in_specs=[pl.BlockSpec(memory_space=pl.ANY),
                  pl.BlockSpec((1, d, two_ff), by_expert),
                  pl.BlockSpec((1, 1, two_ff), by_expert),
                  pl.BlockSpec((1, d_ff, d), by_expert),
                  pl.BlockSpec((1, 1, d), by_expert)],
        out_specs=pl.BlockSpec(memory_space=pl.ANY),
        scratch_shapes=[pltpu.VMEM((d, two_ff), BF16), pltpu.VMEM((d_ff, d), BF16),
                        pltpu.VMEM((2, tile_rows, LANES), F32),
                        pltpu.VMEM((2, tile_rows, LANES), F32),
                        pltpu.SemaphoreType.DMA((2,)), pltpu.SemaphoreType.DMA((2,))],
    )
    return pl.pallas_call(
        _experts_kernel,
        grid_spec=grid_spec,
        out_shape=jax.ShapeDtypeStruct(xs.shape, F32),
        input_output_aliases={2: 0},
        compiler_params=_params("arbitrary"),
        name="experts",
    )(tile_start, token_end, xs, w_gu, b_gu.reshape(n_exp, 1, two_ff), w_down,
      b_down.reshape(n_exp, 1, d))


def _combine_kernel(slot_ref, gate_ref, ys_ref, x1_ref, g2_ref, lng_ref, lnb_ref, o_ref,
                    buf, acc_ref, sem):
    j = pl.program_id(0)
    tokens = o_ref.shape[0]
    n_tokens = tokens * pl.num_programs(0)
    n_blocks = o_ref.shape[1] // LANES
    tile_rows = TOP_K * SUBLANES
    cur = j % 2

    def gather_token(block, which, i):
        for kk in range(TOP_K):
            row = slot_ref[kk * n_tokens + block * tokens + i]
            src = ys_ref.at[pl.ds(pl.multiple_of(row * SUBLANES, SUBLANES), SUBLANES), :]
            dst = buf.at[which, pl.ds(pl.multiple_of((i * TOP_K + kk) * SUBLANES, SUBLANES),
                                      SUBLANES), :]
            pltpu.make_async_copy(src, dst, sem.at[which]).start(priority=kk % 2)

    def reduce_token(i):
        tile = None
        for kk in range(TOP_K):
            row0 = pl.multiple_of((i * TOP_K + kk) * SUBLANES, SUBLANES)
            part = gate_ref[kk * n_tokens + j * tokens + i] * buf[cur, pl.ds(row0, SUBLANES), :]
            tile = part if tile is None else tile + part
        acc_ref[pl.ds(pl.multiple_of(i * SUBLANES, SUBLANES), SUBLANES), :] = tile

    @pl.when(j == 0)
    def _():
        def first(i, carry):
            gather_token(0, 0, i)
            return carry
        lax.fori_loop(0, tokens, first, 0, unroll=8)

    pltpu.make_async_copy(ys_ref.at[pl.ds(0, tokens * tile_rows), :], buf.at[cur],
                          sem.at[cur]).wait()

    @pl.when(j + 1 < pl.num_programs(0))
    def _():
        def step(i, carry):
            gather_token(j + 1, 1 - cur, i)
            reduce_token(i)
            return carry
        lax.fori_loop(0, tokens, step, 0, unroll=8)

    @pl.when(j + 1 == pl.num_programs(0))
    def _():
        def step(i, carry):
            reduce_token(i)
            return carry
        lax.fori_loop(0, tokens, step, 0, unroll=8)

    ffn = jnp.concatenate([acc_ref[pl.ds(s, tokens, stride=SUBLANES), :] for s in range(n_blocks)],
                          axis=1)
    z = DEEPNORM_ALPHA * x1_ref[...] + g2_ref[0] * ffn
    o_ref[...] = _normalize(z) * lng_ref[...] + lnb_ref[...]


def _combine(slot, gates_flat, ys, x1, g2, ln_g, ln_b, tokens_per_sample):
    n, d = x1.shape
    tokens = min(COMBINE_TOKENS, tokens_per_sample)
    assert tokens_per_sample % tokens == 0
    blocks_per_sample = tokens_per_sample // tokens
    vec = pl.BlockSpec((1, d), lambda j, *_: (0, 0))
    grid_spec = pltpu.PrefetchScalarGridSpec(
        num_scalar_prefetch=2,
        grid=(n // tokens,),
        in_specs=[pl.BlockSpec(memory_space=pl.ANY),
                  pl.BlockSpec((tokens, d), lambda j, *_: (j, 0)),
                  pl.BlockSpec((1, 1, d), lambda j, *_: (j // blocks_per_sample, 0, 0)),
                  vec, vec],
        out_specs=pl.BlockSpec((tokens, d), lambda j, *_: (j, 0)),
        scratch_shapes=[pltpu.VMEM((2, tokens * TOP_K * SUBLANES, LANES), F32),
                        pltpu.VMEM((tokens * SUBLANES, LANES), F32),
                        pltpu.SemaphoreType.DMA((2,))],
    )
    return pl.pallas_call(
        _combine_kernel,
        grid_spec=grid_spec,
        out_shape=jax.ShapeDtypeStruct((n, d), F32),
        compiler_params=_params("arbitrary"),
        name="combine",
    )(slot, gates_flat, ys, x1, g2, ln_g.reshape(1, d), ln_b.reshape(1, d))


def _layer(x, c, ctx, c_ctx, ada_w, ada_b, w_in, rpb, sgu_ln_g, sgu_ln_b, sgu_w, sgu_b, w_out,
           ln1_g, ln1_b, ln2_g, ln2_b, router_w, router_b, w_gu, b_gu, w_down, b_down):
    b, s, d = x.shape
    n = b * s
    assert s % (GRID_W * ATT_ROWS) == 0 and s // GRID_W >= NA_KH

    cond_rows = jnp.zeros((8, d), F32).at[:b].set(c).at[b].set(c_ctx)
    mod = _ada(cond_rows, ada_w, ada_b)
    sh1, sc1, g1, sh2, sc2, g2 = jnp.split(mod[:b], 6, axis=-1)
    csh1, csc1 = mod[b, :d], mod[b, d:2 * d]

    w_in_bf = w_in.astype(BF16)
    k_c, v_c = _ctx_kv(ctx, csh1, csc1, w_in_bf[:, NA_WIDTH:3 * NA_WIDTH])
    q, k, v, sgu = _proj_in(x, sh1, sc1, w_in_bf, sgu_ln_g, sgu_ln_b, sgu_w, sgu_b)
    att = _natten(q, k, v, k_c, v_c, _bias_table(rpb))
    x1, t_tiles, logits_t = _mix_out(att, sgu, x, w_out.astype(BF16), g1, sh2, sc2, ln1_g, ln1_b,
                                     router_w, router_b)

    expert_k, rank_k, gate_k, counts = _route(logits_t)
    n_tiles_max = n * TOP_K // EXPERT_ROWS + N_EXPERTS
    slot, tile_start, group_start, counts, padded = _plan(expert_k, rank_k, counts[:, 0])
    xs = _dispatch(slot, group_start, counts, padded, tile_start[N_EXPERTS:], t_tiles,
                   n_tiles_max * EXPERT_ROWS)
    ys = _experts(tile_start, group_start + counts, xs, w_gu, b_gu, w_down, b_down)
    out = _combine(slot, gate_k.reshape(-1), ys, x1.reshape(n, d), g2.reshape(b, 1, d),
                   ln2_g, ln2_b, s)
    return out.reshape(b, s, d)


def kernel(x, c, ctx, c_ctx, ada_w, ada_b, w_in, rpb, sgu_ln_g, sgu_ln_b, sgu_w, sgu_b, w_out,
           ln1_g, ln1_b, ln2_g, ln2_b, router_w, router_b, exp_w_gu, exp_b_gu, exp_w_down,
           exp_b_down):
    assert ada_w.shape[0] == DEPTH
    return _layer(x, c, ctx, c_ctx, ada_w[0], ada_b[0], w_in[0], rpb[0], sgu_ln_g[0], sgu_ln_b[0],
                  sgu_w[0], sgu_b[0], w_out[0], ln1_g[0], ln1_b[0], ln2_g[0], ln2_b[0],
                  router_w[0], router_b[0], exp_w_gu[0], exp_b_gu[0], exp_w_down[0], exp_b_down[0])
```

```python
import functools

import jax
import jax.numpy as jnp
from jax import lax
from jax.experimental import pallas as pl
from jax.experimental.pallas import tpu as pltpu

F32 = jnp.float32
BF16 = jnp.bfloat16

GRID_W = 64
NA_HEADS = 8
NA_HEAD_DIM = 64
NA_WIDTH = NA_HEADS * NA_HEAD_DIM
NA_KH = 8
NA_KW = 16
SGU_GROUPS = 4
SGU_GROUP_DIM = 128
SGU_WIDTH = SGU_GROUPS * SGU_GROUP_DIM
CHUNK = 128
N_EXPERTS = 32
TOP_K = 4
SWIGLU_LIMIT = 7.0
SWIGLU_ALPHA = 1.702
LN_EPS = 1e-5
DEPTH = 1
DEEPNORM_ALPHA = (2.0 * DEPTH) ** 0.25
MASKED = -1e30

SUBLANES = 8
LANES = 128
VMEM_LIMIT_BYTES = 52 * 1024 * 1024

ROW_TILE = 1024
ATT_ROWS = 8
ATT_HEAD_GROUP = 4
ROUTE_CHUNK = 1024
SCAN_BLOCK = 256
EXPERT_ROWS = 512
DISPATCH_TOKENS = 4096
COMBINE_TOKENS = 512


def _params(*sem):
    return pltpu.CompilerParams(dimension_semantics=sem, vmem_limit_bytes=VMEM_LIMIT_BYTES)


def _dot_split3(a, b, dims):
    a_hi = a.astype(BF16)
    a_lo = (a - a_hi.astype(F32)).astype(BF16)
    b_hi = b.astype(BF16)
    b_lo = (b - b_hi.astype(F32)).astype(BF16)
    dot = functools.partial(lax.dot_general, dimension_numbers=dims, preferred_element_type=F32)
    return dot(a_hi, b_hi) + dot(a_hi, b_lo) + dot(a_lo, b_hi)


def _normalize(x):
    mu = jnp.mean(x, axis=-1, keepdims=True)
    xc = x - mu
    var = jnp.mean(xc * xc, axis=-1, keepdims=True)
    return xc * lax.rsqrt(var + LN_EPS)


def _ada_kernel(c_ref, w_ref, b_ref, o_ref):
    s = c_ref[...]
    s = s * jax.nn.sigmoid(s)
    o_ref[...] = _dot_split3(s, w_ref[...], (((1,), (0,)), ((), ()))) + b_ref[...]


def _ada(cond_rows, ada_w, ada_b):
    d = cond_rows.shape[1]
    n_out = ada_w.shape[1]
    return pl.pallas_call(
        _ada_kernel,
        grid=(n_out // d,),
        in_specs=[pl.BlockSpec((8, d), lambda j: (0, 0)),
                  pl.BlockSpec((d, d), lambda j: (0, j)),
                  pl.BlockSpec((1, d), lambda j: (0, j))],
        out_specs=pl.BlockSpec((8, d), lambda j: (0, j)),
        out_shape=jax.ShapeDtypeStruct((8, n_out), F32),
        compiler_params=_params("arbitrary"),
        name="ada",
    )(cond_rows, ada_w, ada_b.reshape(1, n_out))


def _ctx_kv_kernel(ctx_ref, sh_ref, sc_ref, w_ref, k_ref, v_ref):
    h = _normalize(ctx_ref[0]) * (1.0 + sc_ref[...]) + sh_ref[...]
    kv = jnp.dot(h.astype(BF16), w_ref[...], preferred_element_type=F32)
    k_ref[0] = kv[:, :NA_WIDTH].astype(BF16)
    v_ref[0] = kv[:, NA_WIDTH:].astype(BF16)


def _ctx_kv(ctx, csh1, csc1, w_kv):
    b, l, d = ctx.shape
    out = jax.ShapeDtypeStruct((b, l, NA_WIDTH), BF16)
    return pl.pallas_call(
        _ctx_kv_kernel,
        grid=(b,),
        in_specs=[pl.BlockSpec((1, l, d), lambda i: (i, 0, 0)),
                  pl.BlockSpec((1, d), lambda i: (0, 0)),
                  pl.BlockSpec((1, d), lambda i: (0, 0)),
                  pl.BlockSpec((d, 2 * NA_WIDTH), lambda i: (0, 0))],
        out_specs=[pl.BlockSpec((1, l, NA_WIDTH), lambda i: (i, 0, 0)),
                   pl.BlockSpec((1, l, NA_WIDTH), lambda i: (i, 0, 0))],
        out_shape=[out, out],
        compiler_params=_params("arbitrary"),
        name="ctx_kv",
    )(ctx, csh1.reshape(1, d), csc1.reshape(1, d), w_kv)


def _proj_in_kernel(x_ref, sh_ref, sc_ref, w_ref, lng_ref, lnb_ref, ws_ref, bs_ref,
                    q_ref, k_ref, v_ref, s_ref):
    h = (_normalize(x_ref[0]) * (1.0 + sc_ref[0]) + sh_ref[0]).astype(BF16)

    def proj(lo, width):
        return jnp.dot(h, w_ref[:, lo:lo + width], preferred_element_type=F32)

    q_ref[0] = (proj(0, NA_WIDTH) * (NA_HEAD_DIM ** -0.5)).astype(BF16)
    k_ref[0] = proj(NA_WIDTH, NA_WIDTH).astype(BF16)
    v_ref[0] = proj(2 * NA_WIDTH, NA_WIDTH).astype(BF16)
    u = jax.nn.gelu(proj(3 * NA_WIDTH, SGU_WIDTH))
    g = jax.nn.gelu(proj(3 * NA_WIDTH + SGU_WIDTH, SGU_WIDTH))
    gn = (_normalize(g) * lng_ref[...] + lnb_ref[...]).astype(BF16)
    rows = h.shape[0]
    for n in range(rows // CHUNK):
        r0 = n * CHUNK
        for grp in range(SGU_GROUPS):
            c0 = grp * SGU_GROUP_DIM
            mixed = jnp.dot(ws_ref[grp], gn[r0:r0 + CHUNK, c0:c0 + SGU_GROUP_DIM],
                            preferred_element_type=F32) + bs_ref[grp]
            s_ref[0, r0:r0 + CHUNK, c0:c0 + SGU_GROUP_DIM] = (
                u[r0:r0 + CHUNK, c0:c0 + SGU_GROUP_DIM] * mixed).astype(BF16)


def _proj_in(x, sh1, sc1, w_in, sgu_ln_g, sgu_ln_b, sgu_w, sgu_b):
    b, s, d = x.shape
    d_in = w_in.shape[1]
    tm = min(ROW_TILE, s)
    out = jax.ShapeDtypeStruct((b, s, NA_WIDTH), BF16)
    row_spec = pl.BlockSpec((1, tm, NA_WIDTH), lambda i, j: (i, j, 0))
    mod_spec = pl.BlockSpec((1, 1, d), lambda i, j: (i, 0, 0))
    bs = jnp.broadcast_to(sgu_b[:, :, None], (SGU_GROUPS, CHUNK, SGU_GROUP_DIM))
    return pl.pallas_call(
        _proj_in_kernel,
        grid=(b, s // tm),
        in_specs=[pl.BlockSpec((1, tm, d), lambda i, j: (i, j, 0)),
                  mod_spec, mod_spec,
                  pl.BlockSpec((d, d_in), lambda i, j: (0, 0)),
                  pl.BlockSpec((1, SGU_WIDTH), lambda i, j: (0, 0)),
                  pl.BlockSpec((1, SGU_WIDTH), lambda i, j: (0, 0)),
                  pl.BlockSpec((SGU_GROUPS, CHUNK, CHUNK), lambda i, j: (0, 0, 0)),
                  pl.BlockSpec((SGU_GROUPS, CHUNK, SGU_GROUP_DIM), lambda i, j: (0, 0, 0))],
        out_specs=[row_spec, row_spec, row_spec, row_spec],
        out_shape=[out, out, out, out],
        compiler_params=_params("parallel", "parallel"),
        name="proj_in",
    )(x, sh1.reshape(b, 1, d), sc1.reshape(b, 1, d), w_in,
      sgu_ln_g.reshape(1, SGU_WIDTH), sgu_ln_b.reshape(1, SGU_WIDTH), sgu_w.astype(BF16), bs)


def _bias_table(rpb):
    heads, n_dr, n_dc = rpb.shape
    c = jnp.arange(GRID_W)
    cs = jnp.clip(c - NA_KW // 2, 0, GRID_W - NA_KW)
    kc = jnp.arange(GRID_W)
    valid = (kc[None, :] >= cs[:, None]) & (kc[None, :] < cs[:, None] + NA_KW)
    lead = GRID_W - NA_KW
    padded = jnp.pad(rpb.astype(F32), ((0, 0), (0, 0), (lead, 2 * GRID_W - lead - n_dc)))
    flat = jnp.tile(padded, (1, 1, GRID_W))[:, :, :GRID_W * (2 * GRID_W - 1)]
    toe = flat.reshape(heads, n_dr, GRID_W, 2 * GRID_W - 1)[..., GRID_W - 1:]
    toe = jnp.where(valid, toe, MASKED)
    tab = jnp.stack([
        jnp.concatenate([toe[:, NA_KH - 1 - o + i] for i in range(NA_KH)], axis=-1)
        for o in range(NA_KH)])
    return tab.reshape(NA_KH, NA_HEADS * GRID_W, NA_KH * GRID_W)


def _natten_kernel(q_ref, k_ref, v_ref, kc_ref, vc_ref, bias_ref, o_ref, *, grid_rows):
    group_lanes = ATT_HEAD_GROUP * NA_HEAD_DIM
    group_rows = ATT_HEAD_GROUP * GRID_W
    row_head = lax.broadcasted_iota(jnp.int32, (group_rows, group_lanes), 0) // GRID_W
    lane_head = lax.broadcasted_iota(jnp.int32, (group_rows, group_lanes), 1) // NA_HEAD_DIM
    own_head = row_head == lane_head
    nt = (((1,), (1,)), ((), ()))

    def one_row(i, carry):
        r = pl.program_id(1) * ATT_ROWS + i
        rs = jnp.clip(r - NA_KH // 2, 0, grid_rows - NA_KH)
        k0 = pl.multiple_of(rs * GRID_W, GRID_W)
        q0 = pl.multiple_of(i * GRID_W, GRID_W)
        for grp in range(NA_HEADS // ATT_HEAD_GROUP):
            lanes = slice(grp * group_lanes, (grp + 1) * group_lanes)
            q = q_ref[0, pl.ds(q0, GRID_W), lanes]
            qs = jnp.where(own_head, jnp.concatenate([q] * ATT_HEAD_GROUP, axis=0),
                           jnp.zeros((), BF16))
            kr = k_ref[0, pl.ds(k0, NA_KH * GRID_W), lanes]
            vr = v_ref[0, pl.ds(k0, NA_KH * GRID_W), lanes]
            bias = bias_ref[r - rs, grp * group_rows:(grp + 1) * group_rows, :]
            s_nb = lax.dot_general(qs, kr, nt, preferred_element_type=F32) + bias
            s_cx = lax.dot_general(qs, kc_ref[0, :, lanes], nt, preferred_element_type=F32)
            m = jnp.maximum(jnp.max(s_nb, axis=-1, keepdims=True),
                            jnp.max(s_cx, axis=-1, keepdims=True))
            p_nb = jnp.exp(s_nb - m)
            p_cx = jnp.exp(s_cx - m)
            denom = jnp.sum(p_nb, axis=-1, keepdims=True) + jnp.sum(p_cx, axis=-1, keepdims=True)
            o = (jnp.dot(p_nb.astype(BF16), vr, preferred_element_type=F32)
                 + jnp.dot(p_cx.astype(BF16), vc_ref[0, :, lanes],
                           preferred_element_type=F32)) / denom
            o = jnp.where(own_head, o, 0.0)
            out = o[:GRID_W]
            for h in range(1, ATT_HEAD_GROUP):
                out = out + o[h * GRID_W:(h + 1) * GRID_W]
            o_ref[0, pl.ds(q0, GRID_W), lanes] = out.astype(BF16)
        return carry

    lax.fori_loop(0, ATT_ROWS, one_row, 0, unroll=True)


def _natten(q, k, v, k_c, v_c, bias):
    b, s, w = q.shape
    l = k_c.shape[1]
    grid_rows = s // GRID_W
    tq = ATT_ROWS * GRID_W
    full = pl.BlockSpec((1, s, w), lambda i, j: (i, 0, 0))
    ctx = pl.BlockSpec((1, l, w), lambda i, j: (i, 0, 0))
    return pl.pallas_call(
        functools.partial(_natten_kernel, grid_rows=grid_rows),
        grid=(b, grid_rows // ATT_ROWS),
        in_specs=[pl.BlockSpec((1, tq, w), lambda i, j: (i, j, 0)),
                  full, full, ctx, ctx,
                  pl.BlockSpec(bias.shape, lambda i, j: (0, 0, 0), pipeline_mode=pl.Buffered(1))],
        out_specs=pl.BlockSpec((1, tq, w), lambda i, j: (i, j, 0)),
        out_shape=jax.ShapeDtypeStruct((b, s, w), BF16),
        compiler_params=_params("parallel", "arbitrary"),
        name="natten",
    )(q, k, v, k_c, v_c, bias)


def _mix_out_kernel(att_ref, sgu_ref, x_ref, wo_ref, g1_ref, sh_ref, sc_ref, lng_ref, lnb_ref,
                    wr_ref, br_ref, x1_ref, t_ref, lg_ref):
    mix = (jnp.dot(att_ref[0], wo_ref[:NA_WIDTH, :], preferred_element_type=F32)
           + jnp.dot(sgu_ref[0], wo_ref[NA_WIDTH:, :], preferred_element_type=F32))
    x1 = _normalize(DEEPNORM_ALPHA * x_ref[0] + g1_ref[0] * mix) * lng_ref[...] + lnb_ref[...]
    x1_ref[0] = x1
    t = _normalize(x1) * (1.0 + sc_ref[0]) + sh_ref[0]
    rows = t.shape[0]
    for s in range(t.shape[1] // LANES):
        t_ref[pl.ds(s, rows, stride=SUBLANES), :] = t[:, s * LANES:(s + 1) * LANES]
    lg_ref[...] = _dot_split3(wr_ref[...], t, (((1,), (1,)), ((), ()))) + br_ref[...]


def _mix_out(att, sgu, x, w_out, g1, sh2, sc2, ln_g, ln_b, router_w, router_b):
    b, s, d = x.shape
    assert d == SUBLANES * LANES
    tm = min(ROW_TILE, s)
    nj = s // tm
    row = lambda width: pl.BlockSpec((1, tm, width), lambda i, j: (i, j, 0))
    mod_spec = pl.BlockSpec((1, 1, d), lambda i, j: (i, 0, 0))
    vec_spec = pl.BlockSpec((1, d), lambda i, j: (0, 0))
    return pl.pallas_call(
        _mix_out_kernel,
        grid=(b, nj),
        in_specs=[row(NA_WIDTH), row(SGU_WIDTH), row(d),
                  pl.BlockSpec((d, d), lambda i, j: (0, 0)),
                  mod_spec, mod_spec, mod_spec, vec_spec, vec_spec,
                  pl.BlockSpec((N_EXPERTS, d), lambda i, j: (0, 0)),
                  pl.BlockSpec((N_EXPERTS, 1), lambda i, j: (0, 0))],
        out_specs=[row(d),
                   pl.BlockSpec((tm * SUBLANES, LANES), lambda i, j: (i * nj + j, 0)),
                   pl.BlockSpec((N_EXPERTS, tm), lambda i, j: (0, i * nj + j))],
        out_shape=[jax.ShapeDtypeStruct((b, s, d), F32),
                   jax.ShapeDtypeStruct((b * s * SUBLANES, LANES), F32),
                   jax.ShapeDtypeStruct((N_EXPERTS, b * s), F32)],
        compiler_params=_params("parallel", "parallel"),
        name="mix_out",
    )(att, sgu, x, w_out, g1.reshape(b, 1, d), sh2.reshape(b, 1, d), sc2.reshape(b, 1, d),
      ln_g.reshape(1, d), ln_b.reshape(1, d), router_w.T, router_b.reshape(N_EXPERTS, 1))


def _route_kernel(lg_ref, e_ref, r_ref, g_ref, cnt_ref, run_ref):
    @pl.when(pl.program_id(0) == 0)
    def _():
        run_ref[...] = jnp.zeros_like(run_ref)

    logits = lg_ref[...]
    n_tok = logits.shape[1]
    expert = lax.broadcasted_iota(jnp.int32, logits.shape, 0)
    work = logits
    picks, tops = [], []
    for kk in range(TOP_K):
        m = jnp.max(work, axis=0, keepdims=True)
        first = jnp.min(jnp.where(work == m, expert, N_EXPERTS), axis=0, keepdims=True)
        pick = expert == first
        work = jnp.where(pick, -jnp.inf, work)
        picks.append(pick)
        tops.append(m)
        e_ref[kk:kk + 1, :] = first
    weights = [jnp.exp(m - tops[0]) for m in tops]
    denom = weights[0]
    for w in weights[1:]:
        denom = denom + w
    for kk in range(TOP_K):
        g_ref[kk:kk + 1, :] = weights[kk] / denom

    chosen = picks[0]
    for pick in picks[1:]:
        chosen = jnp.logical_or(chosen, pick)
    tri = (lax.broadcasted_iota(jnp.int32, (SCAN_BLOCK, SCAN_BLOCK), 0)
           <= lax.broadcasted_iota(jnp.int32, (SCAN_BLOCK, SCAN_BLOCK), 1)).astype(BF16)
    sel = jnp.where(chosen, 1.0, 0.0).astype(BF16)
    carry = run_ref[:, 0:1]
    for blk in range(n_tok // SCAN_BLOCK):
        lo = blk * SCAN_BLOCK
        run = jnp.dot(sel[:, lo:lo + SCAN_BLOCK], tri, preferred_element_type=F32) + carry
        for kk in range(TOP_K):
            rank = jnp.sum(jnp.where(picks[kk][:, lo:lo + SCAN_BLOCK], run - 1.0, 0.0),
                           axis=0, keepdims=True)
            r_ref[kk:kk + 1, lo:lo + SCAN_BLOCK] = rank.astype(jnp.int32)
        carry = run[:, SCAN_BLOCK - 1:SCAN_BLOCK]
    run_ref[...] = jnp.broadcast_to(carry, run_ref.shape)
    cnt_ref[...] = jnp.broadcast_to(carry, cnt_ref.shape).astype(jnp.int32)


def _route(logits_t):
    n = logits_t.shape[1]
    chunk = min(ROUTE_CHUNK, n)
    assert n % chunk == 0 and chunk % SCAN_BLOCK == 0
    per_k = pl.BlockSpec((TOP_K, chunk), lambda c: (0, c))
    return pl.pallas_call(
        _route_kernel,
        grid=(n // chunk,),
        in_specs=[pl.BlockSpec((N_EXPERTS, chunk), lambda c: (0, c))],
        out_specs=[per_k, per_k, per_k, pl.BlockSpec((N_EXPERTS, LANES), lambda c: (0, 0))],
        out_shape=[jax.ShapeDtypeStruct((TOP_K, n), jnp.int32),
                   jax.ShapeDtypeStruct((TOP_K, n), jnp.int32),
                   jax.ShapeDtypeStruct((TOP_K, n), F32),
                   jax.ShapeDtypeStruct((N_EXPERTS, LANES), jnp.int32)],
        scratch_shapes=[pltpu.VMEM((N_EXPERTS, LANES), F32)],
        compiler_params=_params("arbitrary"),
        name="route",
    )(logits_t)


def _plan(expert_k, rank_k, counts):
    padded = (counts + EXPERT_ROWS - 1) // EXPERT_ROWS * EXPERT_ROWS
    group_end = jnp.cumsum(padded)
    group_start = group_end - padded
    start_k = jnp.sum(jnp.where(expert_k[None] == jnp.arange(N_EXPERTS)[:, None, None],
                                group_start[:, None, None], 0), axis=0)
    slot = (start_k + rank_k).reshape(-1)
    tile_start = jnp.concatenate([group_start, group_end[-1:]]) // EXPERT_ROWS
    return (slot.astype(jnp.int32), tile_start.astype(jnp.int32),
            group_start.astype(jnp.int32), counts.astype(jnp.int32), padded.astype(jnp.int32))


def _dispatch_kernel(slot_ref, start_ref, cnt_ref, pad_ref, nt_ref, t_ref, xs_ref, zero_ref,
                     sem, zsem):
    tokens = t_ref.shape[0] // SUBLANES
    n_tokens = tokens * pl.num_programs(0)
    base = pl.program_id(0) * tokens
    tile_rows = zero_ref.shape[0]
    n_tiles_max = xs_ref.shape[0] // tile_rows

    def row_copy(src, row, sem_):
        dst = xs_ref.at[pl.ds(pl.multiple_of(row * SUBLANES, SUBLANES), SUBLANES), :]
        return pltpu.make_async_copy(src, dst, sem_)

    def tile_copy(tile):
        dst = xs_ref.at[pl.ds(pl.multiple_of(tile * tile_rows, tile_rows), tile_rows), :]
        return pltpu.make_async_copy(zero_ref, dst, zsem)

    @pl.when(pl.program_id(0) == 0)
    def _():
        zero_ref[...] = jnp.zeros_like(zero_ref)

        def padded_tile(e):
            return (start_ref[e] + pad_ref[e]) // (tile_rows // SUBLANES) - 1

        def fill_group(e, carry):
            @pl.when(pad_ref[e] > cnt_ref[e])
            def _():
                tile_copy(padded_tile(e)).start()
            return carry

        def drain_group(e, carry):
            @pl.when(pad_ref[e] > cnt_ref[e])
            def _():
                tile_copy(padded_tile(e)).wait()
            return carry

        lax.fori_loop(0, N_EXPERTS, fill_group, 0)
        lax.fori_loop(0, N_EXPERTS, drain_group, 0)

        def fill_tile(i, carry):
            tile_copy(i).start()
            return carry

        def drain_tile(i, carry):
            tile_copy(i).wait()
            return carry

        lax.fori_loop(nt_ref[0], n_tiles_max, fill_tile, 0)
        lax.fori_loop(nt_ref[0], n_tiles_max, drain_tile, 0)

    def scatter(i, carry):
        src = t_ref.at[pl.ds(pl.multiple_of(i * SUBLANES, SUBLANES), SUBLANES), :]
        for kk in range(TOP_K):
            row_copy(src, slot_ref[kk * n_tokens + base + i], sem).start(priority=kk % 2)
        return carry

    lax.fori_loop(0, tokens, scatter, 0, unroll=8)
    for kk in range(TOP_K):
        pltpu.make_async_copy(t_ref, xs_ref.at[pl.ds(0, tokens * SUBLANES), :], sem).wait()


def _dispatch(slot, group_start, counts, padded, n_tiles, t_tiles, n_rows):
    n = t_tiles.shape[0] // SUBLANES
    tokens = min(DISPATCH_TOKENS, n)
    assert n % tokens == 0 and n_rows % EXPERT_ROWS == 0
    grid_spec = pltpu.PrefetchScalarGridSpec(
        num_scalar_prefetch=5,
        grid=(n // tokens,),
        in_specs=[pl.BlockSpec((tokens * SUBLANES, LANES), lambda j, *_: (j, 0))],
        out_specs=pl.BlockSpec(memory_space=pl.ANY),
        scratch_shapes=[pltpu.VMEM((EXPERT_ROWS * SUBLANES, LANES), F32),
                        pltpu.SemaphoreType.DMA, pltpu.SemaphoreType.DMA],
    )
    return pl.pallas_call(
        _dispatch_kernel,
        grid_spec=grid_spec,
        out_shape=jax.ShapeDtypeStruct((n_rows * SUBLANES, LANES), F32),
        compiler_params=pltpu.CompilerParams(dimension_semantics=("arbitrary",),
                                             vmem_limit_bytes=VMEM_LIMIT_BYTES,
                                             has_side_effects=True),
        name="dispatch",
    )(slot, group_start, counts, padded, n_tiles, t_tiles)


def _experts_kernel(ts_ref, end_ref, xs_ref, wgu_ref, bgu_ref, wd_ref, bd_ref, ys_ref,
                    wgu_bf, wd_bf, xbuf, ybuf, xsem, ysem):
    e = pl.program_id(0)
    tile_rows = xbuf.shape[1]
    rows = tile_rows // SUBLANES
    d_ff = wd_ref.shape[1]
    n_blocks = wgu_ref.shape[1] // LANES
    n_total = ts_ref[N_EXPERTS]

    def hbm_tile(ref, t):
        return ref.at[pl.ds(pl.multiple_of(t * tile_rows, tile_rows), tile_rows), :]

    def x_copy(t, which):
        return pltpu.make_async_copy(hbm_tile(xs_ref, t), xbuf.at[which], xsem.at[which])

    def y_copy(t, which):
        return pltpu.make_async_copy(ybuf.at[which], hbm_tile(ys_ref, t), ysem.at[which])

    @pl.when(e == 0)
    def _():
        ybuf[...] = jnp.zeros_like(ybuf)
        x_copy(0, 0).start()

    @pl.when(ts_ref[e] < ts_ref[e + 1])
    def _():
        wgu_bf[...] = wgu_ref[0].astype(BF16)
        wd_bf[...] = wd_ref[0].astype(BF16)

        def one_tile(t, carry):
            cur = t % 2

            @pl.when(t + 1 < n_total)
            def _():
                x_copy(t + 1, 1 - cur).start(priority=1)

            x_copy(t, cur).wait()

            @pl.when(t >= 2)
            def _():
                y_copy(t - 2, cur).wait()

            def ffn(n_rows):
                x = jnp.concatenate(
                    [xbuf[cur, pl.ds(s, n_rows, stride=SUBLANES), :] for s in range(n_blocks)],
                    axis=1).astype(BF16)
                gu = jnp.dot(x, wgu_bf[...], preferred_element_type=F32) + bgu_ref[0]
                gate = jnp.minimum(gu[:, :d_ff], SWIGLU_LIMIT)
                lin = jnp.clip(gu[:, d_ff:], -SWIGLU_LIMIT, SWIGLU_LIMIT)
                act = ((lin + 1.0) * (gate * jax.nn.sigmoid(SWIGLU_ALPHA * gate))).astype(BF16)
                y = jnp.dot(act, wd_bf[...], preferred_element_type=F32) + bd_ref[0]
                for s in range(n_blocks):
                    ybuf[cur, pl.ds(s, n_rows, stride=SUBLANES), :] = y[:, s * LANES:(s + 1) * LANES]

            live = end_ref[e] - t * rows

            quarter = rows // 4
            for q in range(1, 5):
                lo = (q - 1) * quarter if q > 1 else -1
                cond = live > lo if q == 4 else jnp.logical_and(live > lo, live <= q * quarter)
                pl.when(cond)(functools.partial(ffn, q * quarter))

            y_copy(t, cur).start(priority=1)
            return carry

        lax.fori_loop(ts_ref[e], ts_ref[e + 1], one_tile, 0)

    @pl.when(e == pl.num_programs(0) - 1)
    def _():
        y_copy(n_total - 1, (n_total - 1) % 2).wait()

        @pl.when(n_total >= 2)
        def _():
            y_copy(n_total - 2, n_total % 2).wait()


def _experts(tile_start, token_end, xs, w_gu, b_gu, w_down, b_down):
    n_exp, d, two_ff = w_gu.shape
    d_ff = two_ff // 2
    by_expert = lambda e, ts, end: (e, 0, 0)
    tile_rows = EXPERT_ROWS * SUBLANES
    grid_spec = pltpu.PrefetchScalarGridSpec(
        num_scalar_prefetch=2,
        grid=(n_exp,),
        in_specs=[pl.BlockSpec(memory_space=pl.ANY),
                  pl.BlockSpec((1, d, two_ff), by_expert),
                  pl.BlockSpec((1, 1, two_ff), by_expert),
                  pl.BlockSpec((1, d_ff, d), by_expert),
                  pl.BlockSpec((1, 1, d), by_expert)],
        out_specs=pl.BlockSpec(memory_space=pl.ANY),
        scratch_shapes=[pltpu.VMEM((d, two_ff), BF16), pltpu.VMEM((d_ff, d), BF16),
                        pltpu.VMEM((2, tile_rows, LANES), F32),
                        pltpu.VMEM((2, tile_rows, LANES), F32),
                        pltpu.SemaphoreType.DMA((2,)), pltpu.SemaphoreType.DMA((2,))],
    )
    return pl.pallas_call(
        _experts_kernel,
        grid_spec=grid_spec,
        out_shape=jax.ShapeDtypeStruct(xs.shape, F32),
        input_output_aliases={2: 0},
        compiler_params=_params("arbitrary"),
        name="experts",
    )(tile_start, token_end, xs, w_gu, b_gu.reshape(n_exp, 1, two_ff), w_down,
      b_down.reshape(n_exp, 1, d))


def _combine_kernel(slot_ref, gate_ref, ys_ref, x1_ref, g2_ref, lng_ref, lnb_ref, o_ref,
                    buf, acc_ref, sem):
    j = pl.program_id(0)
    tokens = o_ref.shape[0]
    n_tokens = tokens * pl.num_programs(0)
    n_blocks = o_ref.shape[1] // LANES
    tile_rows = TOP_K * SUBLANES

    def gather_token(block, which, i):
        for kk in range(TOP_K):
            row = slot_ref[kk * n_tokens + block * tokens + i]
            src = ys_ref.at[pl.ds(pl.multiple_of(row * SUBLANES, SUBLANES), SUBLANES), :]
            dst = buf.at[which, pl.ds(pl.multiple_of((i * TOP_K + kk) * SUBLANES, SUBLANES),
                                      SUBLANES), :]
            pltpu.make_async_copy(src, dst, sem.at[which]).start(priority=kk % 2)

    def reduce_token(which, i):
        tile = None
        for kk in range(TOP_K):
            row0 = pl.multiple_of((i * TOP_K + kk) * SUBLANES, SUBLANES)
            part = gate_ref[kk * n_tokens + j * tokens + i] * buf[which, pl.ds(row0, SUBLANES), :]
            tile = part if tile is None else tile + part
        acc_ref[pl.ds(pl.multiple_of(i * SUBLANES, SUBLANES), SUBLANES), :] = tile

    @pl.when(j == 0)
    def _():
        def first(i, carry):
            gather_token(0, 0, i)
            return carry
        lax.fori_loop(0, tokens, first, 0, unroll=8)

    def block(cur):
        pltpu.make_async_copy(ys_ref.at[pl.ds(0, tokens * tile_rows), :], buf.at[cur],
                              sem.at[cur]).wait()

        @pl.when(j + 1 < pl.num_programs(0))
        def _():
            def step(i, carry):
                gather_token(j + 1, 1 - cur, i)
                reduce_token(cur, i)
                return carry
            lax.fori_loop(0, tokens, step, 0, unroll=16)

        @pl.when(j + 1 == pl.num_programs(0))
        def _():
            def step(i, carry):
                reduce_token(cur, i)
                return carry
            lax.fori_loop(0, tokens, step, 0, unroll=8)

    for parity in range(2):
        pl.when(j % 2 == parity)(functools.partial(block, parity))

    ffn = jnp.concatenate([acc_ref[pl.ds(s, tokens, stride=SUBLANES), :] for s in range(n_blocks)],
                          axis=1)
    z = DEEPNORM_ALPHA * x1_ref[...] + g2_ref[0] * ffn
    o_ref[...] = _normalize(z) * lng_ref[...] + lnb_ref[...]


def _combine(slot, gates_flat, ys, x1, g2, ln_g, ln_b, tokens_per_sample):
    n, d = x1.shape
    tokens = min(COMBINE_TOKENS, tokens_per_sample)
    assert tokens_per_sample % tokens == 0
    blocks_per_sample = tokens_per_sample // tokens
    vec = pl.BlockSpec((1, d), lambda j, *_: (0, 0))
    grid_spec = pltpu.PrefetchScalarGridSpec(
        num_scalar_prefetch=2,
        grid=(n // tokens,),
        in_specs=[pl.BlockSpec(memory_space=pl.ANY),
                  pl.BlockSpec((tokens, d), lambda j, *_: (j, 0)),
                  pl.BlockSpec((1, 1, d), lambda j, *_: (j // blocks_per_sample, 0, 0)),
                  vec, vec],
        out_specs=pl.BlockSpec((tokens, d), lambda j, *_: (j, 0)),
        scratch_shapes=[pltpu.VMEM((2, tokens * TOP_K * SUBLANES, LANES), F32),
                        pltpu.VMEM((tokens * SUBLANES, LANES), F32),
                        pltpu.SemaphoreType.DMA((2,))],
    )
    return pl.pallas_call(
        _combine_kernel,
        grid_spec=grid_spec,
        out_shape=jax.ShapeDtypeStruct((n, d), F32),
        compiler_params=_params("arbitrary"),
        name="combine",
    )(slot, gates_flat, ys, x1, g2, ln_g.reshape(1, d), ln_b.reshape(1, d))


def _layer(x, c, ctx, c_ctx, ada_w, ada_b, w_in, rpb, sgu_ln_g, sgu_ln_b, sgu_w, sgu_b, w_out,
           ln1_g, ln1_b, ln2_g, ln2_b, router_w, router_b, w_gu, b_gu, w_down, b_down):
    b, s, d = x.shape
    n = b * s
    assert s % (GRID_W * ATT_ROWS) == 0 and s // GRID_W >= NA_KH

    cond_rows = jnp.zeros((8, d), F32).at[:b].set(c).at[b].set(c_ctx)
    mod = _ada(cond_rows, ada_w, ada_b)
    sh1, sc1, g1, sh2, sc2, g2 = jnp.split(mod[:b], 6, axis=-1)
    csh1, csc1 = mod[b, :d], mod[b, d:2 * d]

    w_in_bf = w_in.astype(BF16)
    k_c, v_c = _ctx_kv(ctx, csh1, csc1, w_in_bf[:, NA_WIDTH:3 * NA_WIDTH])
    q, k, v, sgu = _proj_in(x, sh1, sc1, w_in_bf, sgu_ln_g, sgu_ln_b, sgu_w, sgu_b)
    att = _natten(q, k, v, k_c, v_c, _bias_table(rpb))
    x1, t_tiles, logits_t = _mix_out(att, sgu, x, w_out.astype(BF16), g1, sh2, sc2, ln1_g, ln1_b,
                                     router_w, router_b)

    expert_k, rank_k, gate_k, counts = _route(logits_t)
    n_tiles_max = n * TOP_K // EXPERT_ROWS + N_EXPERTS
    slot, tile_start, group_start, counts, padded = _plan(expert_k, rank_k, counts[:, 0])
    xs = _dispatch(slot, group_start, counts, padded, tile_start[N_EXPERTS:], t_tiles,
                   n_tiles_max * EXPERT_ROWS)
    ys = _experts(tile_start, group_start + counts, xs, w_gu, b_gu, w_down, b_down)
    out = _combine(slot, gate_k.reshape(-1), ys, x1.reshape(n, d), g2.reshape(b, 1, d),
                   ln2_g, ln2_b, s)
    return out.reshape(b, s, d)


def kernel(x, c, ctx, c_ctx, ada_w, ada_b, w_in, rpb, sgu_ln_g, sgu_ln_b, sgu_w, sgu_b, w_out,
           ln1_g, ln1_b, ln2_g, ln2_b, router_w, router_b, exp_w_gu, exp_b_gu, exp_w_down,
           exp_b_down):
    assert ada_w.shape[0] == DEPTH
    return _layer(x, c, ctx, c_ctx, ada_w[0], ada_b[0], w_in[0], rpb[0], sgu_ln_g[0], sgu_ln_b[0],
                  sgu_w[0], sgu_b[0], w_out[0], ln1_g[0], ln1_b[0], ln2_g[0], ln2_b[0],
                  router_w[0], router_b[0], exp_w_gu[0], exp_b_gu[0], exp_w_down[0], exp_b_down[0])
```

```python
import functools

import jax
import jax.numpy as jnp
from jax import lax
from jax.experimental import pallas as pl
from jax.experimental.pallas import tpu as pltpu

F32 = jnp.float32
BF16 = jnp.bfloat16

GRID_W = 64
NA_HEADS = 8
NA_HEAD_DIM = 64
NA_WIDTH = NA_HEADS * NA_HEAD_DIM
NA_KH = 8
NA_KW = 16
SGU_GROUPS = 4
SGU_GROUP_DIM = 128
SGU_WIDTH = SGU_GROUPS * SGU_GROUP_DIM
CHUNK = 128
N_EXPERTS = 32
TOP_K = 4
SWIGLU_LIMIT = 7.0
SWIGLU_ALPHA = 1.702
LN_EPS = 1e-5
DEPTH = 1
DEEPNORM_ALPHA = (2.0 * DEPTH) ** 0.25
MASKED = -1e30

SUBLANES = 8
LANES = 128
VMEM_LIMIT_BYTES = 52 * 1024 * 1024

ROW_TILE = 1024
ATT_ROWS = 8
ATT_HEAD_GROUP = 4
ROUTE_CHUNK = 2048
SCAN_BLOCK = 256
EXPERT_ROWS = 512
EXPERT_PATHS = 8
DISPATCH_TOKENS = 4096
COMBINE_TOKENS = 512


def _params(*sem):
    return pltpu.CompilerParams(dimension_semantics=sem, vmem_limit_bytes=VMEM_LIMIT_BYTES)


def _dot_split3(a, b, dims):
    a_hi = a.astype(BF16)
    a_lo = (a - a_hi.astype(F32)).astype(BF16)
    b_hi = b.astype(BF16)
    b_lo = (b - b_hi.astype(F32)).astype(BF16)
    dot = functools.partial(lax.dot_general, dimension_numbers=dims, preferred_element_type=F32)
    return dot(a_hi, b_hi) + dot(a_hi, b_lo) + dot(a_lo, b_hi)


def _normalize(x):
    mu = jnp.mean(x, axis=-1, keepdims=True)
    xc = x - mu
    var = jnp.mean(xc * xc, axis=-1, keepdims=True)
    return xc * lax.rsqrt(var + LN_EPS)


def _ada_kernel(c_ref, w_ref, b_ref, o_ref):
    s = c_ref[...]
    s = s * jax.nn.sigmoid(s)
    o_ref[...] = _dot_split3(s, w_ref[...], (((1,), (0,)), ((), ()))) + b_ref[...]


def _ada(cond_rows, ada_w, ada_b):
    d = cond_rows.shape[1]
    n_out = ada_w.shape[1]
    return pl.pallas_call(
        _ada_kernel,
        grid=(n_out // d,),
        in_specs=[pl.BlockSpec((8, d), lambda j: (0, 0)),
                  pl.BlockSpec((d, d), lambda j: (0, j)),
                  pl.BlockSpec((1, d), lambda j: (0, j))],
        out_specs=pl.BlockSpec((8, d), lambda j: (0, j)),
        out_shape=jax.ShapeDtypeStruct((8, n_out), F32),
        compiler_params=_params("arbitrary"),
        name="ada",
    )(cond_rows, ada_w, ada_b.reshape(1, n_out))


def _ctx_kv_kernel(ctx_ref, sh_ref, sc_ref, w_ref, k_ref, v_ref):
    h = _normalize(ctx_ref[0]) * (1.0 + sc_ref[...]) + sh_ref[...]
    kv = jnp.dot(h.astype(BF16), w_ref[...], preferred_element_type=F32)
    k_ref[0] = kv[:, :NA_WIDTH].astype(BF16)
    v_ref[0] = kv[:, NA_WIDTH:].astype(BF16)


def _ctx_kv(ctx, csh1, csc1, w_kv):
    b, l, d = ctx.shape
    out = jax.ShapeDtypeStruct((b, l, NA_WIDTH), BF16)
    return pl.pallas_call(
        _ctx_kv_kernel,
        grid=(b,),
        in_specs=[pl.BlockSpec((1, l, d), lambda i: (i, 0, 0)),
                  pl.BlockSpec((1, d), lambda i: (0, 0)),
                  pl.BlockSpec((1, d), lambda i: (0, 0)),
                  pl.BlockSpec((d, 2 * NA_WIDTH), lambda i: (0, 0))],
        out_specs=[pl.BlockSpec((1, l, NA_WIDTH), lambda i: (i, 0, 0)),
                   pl.BlockSpec((1, l, NA_WIDTH), lambda i: (i, 0, 0))],
        out_shape=[out, out],
        compiler_params=_params("arbitrary"),
        name="ctx_kv",
    )(ctx, csh1.reshape(1, d), csc1.reshape(1, d), w_kv)


def _proj_in_kernel(x_ref, sh_ref, sc_ref, w_ref, lng_ref, lnb_ref, ws_ref, bs_ref,
                    q_ref, k_ref, v_ref, s_ref):
    h = (_normalize(x_ref[0]) * (1.0 + sc_ref[0]) + sh_ref[0]).astype(BF16)

    def proj(lo, width):
        return jnp.dot(h, w_ref[:, lo:lo + width], preferred_element_type=F32)

    q_ref[0] = (proj(0, NA_WIDTH) * (NA_HEAD_DIM ** -0.5)).astype(BF16)
    k_ref[0] = proj(NA_WIDTH, NA_WIDTH).astype(BF16)
    v_ref[0] = proj(2 * NA_WIDTH, NA_WIDTH).astype(BF16)
    u = jax.nn.gelu(proj(3 * NA_WIDTH, SGU_WIDTH))
    g = jax.nn.gelu(proj(3 * NA_WIDTH + SGU_WIDTH, SGU_WIDTH))
    gn = (_normalize(g) * lng_ref[...] + lnb_ref[...]).astype(BF16)
    rows = h.shape[0]
    for n in range(rows // CHUNK):
        r0 = n * CHUNK
        for grp in range(SGU_GROUPS):
            c0 = grp * SGU_GROUP_DIM
            mixed = jnp.dot(ws_ref[grp], gn[r0:r0 + CHUNK, c0:c0 + SGU_GROUP_DIM],
                            preferred_element_type=F32) + bs_ref[grp]
            s_ref[0, r0:r0 + CHUNK, c0:c0 + SGU_GROUP_DIM] = (
                u[r0:r0 + CHUNK, c0:c0 + SGU_GROUP_DIM] * mixed).astype(BF16)


def _proj_in(x, sh1, sc1, w_in, sgu_ln_g, sgu_ln_b, sgu_w, sgu_b):
    b, s, d = x.shape
    d_in = w_in.shape[1]
    tm = min(ROW_TILE, s)
    out = jax.ShapeDtypeStruct((b, s, NA_WIDTH), BF16)
    row_spec = pl.BlockSpec((1, tm, NA_WIDTH), lambda i, j: (i, j, 0))
    mod_spec = pl.BlockSpec((1, 1, d), lambda i, j: (i, 0, 0))
    bs = jnp.broadcast_to(sgu_b[:, :, None], (SGU_GROUPS, CHUNK, SGU_GROUP_DIM))
    return pl.pallas_call(
        _proj_in_kernel,
        grid=(b, s // tm),
        in_specs=[pl.BlockSpec((1, tm, d), lambda i, j: (i, j, 0)),
                  mod_spec, mod_spec,
                  pl.BlockSpec((d, d_in), lambda i, j: (0, 0)),
                  pl.BlockSpec((1, SGU_WIDTH), lambda i, j: (0, 0)),
                  pl.BlockSpec((1, SGU_WIDTH), lambda i, j: (0, 0)),
                  pl.BlockSpec((SGU_GROUPS, CHUNK, CHUNK), lambda i, j: (0, 0, 0)),
                  pl.BlockSpec((SGU_GROUPS, CHUNK, SGU_GROUP_DIM), lambda i, j: (0, 0, 0))],
        out_specs=[row_spec, row_spec, row_spec, row_spec],
        out_shape=[out, out, out, out],
        compiler_params=_params("parallel", "parallel"),
        name="proj_in",
    )(x, sh1.reshape(b, 1, d), sc1.reshape(b, 1, d), w_in,
      sgu_ln_g.reshape(1, SGU_WIDTH), sgu_ln_b.reshape(1, SGU_WIDTH), sgu_w.astype(BF16), bs)


def _bias_table(rpb):
    heads, n_dr, n_dc = rpb.shape
    c = jnp.arange(GRID_W)
    cs = jnp.clip(c - NA_KW // 2, 0, GRID_W - NA_KW)
    kc = jnp.arange(GRID_W)
    valid = (kc[None, :] >= cs[:, None]) & (kc[None, :] < cs[:, None] + NA_KW)
    lead = GRID_W - NA_KW
    padded = jnp.pad(rpb.astype(F32), ((0, 0), (0, 0), (lead, 2 * GRID_W - lead - n_dc)))
    flat = jnp.tile(padded, (1, 1, GRID_W))[:, :, :GRID_W * (2 * GRID_W - 1)]
    toe = flat.reshape(heads, n_dr, GRID_W, 2 * GRID_W - 1)[..., GRID_W - 1:]
    toe = jnp.where(valid, toe, MASKED)
    tab = jnp.stack([
        jnp.concatenate([toe[:, NA_KH - 1 - o + i] for i in range(NA_KH)], axis=-1)
        for o in range(NA_KH)])
    return tab.reshape(NA_KH, NA_HEADS * GRID_W, NA_KH * GRID_W)


def _natten_kernel(q_ref, k_ref, v_ref, kc_ref, vc_ref, bias_ref, o_ref, *, grid_rows):
    group_lanes = ATT_HEAD_GROUP * NA_HEAD_DIM
    group_rows = ATT_HEAD_GROUP * GRID_W
    row_head = lax.broadcasted_iota(jnp.int32, (group_rows, group_lanes), 0) // GRID_W
    lane_head = lax.broadcasted_iota(jnp.int32, (group_rows, group_lanes), 1) // NA_HEAD_DIM
    own_head = row_head == lane_head
    nt = (((1,), (1,)), ((), ()))

    def one_row(i, carry):
        r = pl.program_id(1) * ATT_ROWS + i
        rs = jnp.clip(r - NA_KH // 2, 0, grid_rows - NA_KH)
        k0 = pl.multiple_of(rs * GRID_W, GRID_W)
        q0 = pl.multiple_of(i * GRID_W, GRID_W)
        for grp in range(NA_HEADS // ATT_HEAD_GROUP):
            lanes = slice(grp * group_lanes, (grp + 1) * group_lanes)
            q = q_ref[0, pl.ds(q0, GRID_W), lanes]
            qs = jnp.where(own_head, jnp.concatenate([q] * ATT_HEAD_GROUP, axis=0),
                           jnp.zeros((), BF16))
            kr = k_ref[0, pl.ds(k0, NA_KH * GRID_W), lanes]
            vr = v_ref[0, pl.ds(k0, NA_KH * GRID_W), lanes]
            bias = bias_ref[r - rs, grp * group_rows:(grp + 1) * group_rows, :]
            s_nb = lax.dot_general(qs, kr, nt, preferred_element_type=F32) + bias
            s_cx = lax.dot_general(qs, kc_ref[0, :, lanes], nt, preferred_element_type=F32)
            m = jnp.maximum(jnp.max(s_nb, axis=-1, keepdims=True),
                            jnp.max(s_cx, axis=-1, keepdims=True))
            p_nb = jnp.exp(s_nb - m)
            p_cx = jnp.exp(s_cx - m)
            denom = jnp.sum(p_nb, axis=-1, keepdims=True) + jnp.sum(p_cx, axis=-1, keepdims=True)
            o = (jnp.dot(p_nb.astype(BF16), vr, preferred_element_type=F32)
                 + jnp.dot(p_cx.astype(BF16), vc_ref[0, :, lanes],
                           preferred_element_type=F32)) / denom
            o = jnp.where(own_head, o, 0.0)
            out = o[:GRID_W]
            for h in range(1, ATT_HEAD_GROUP):
                out = out + o[h * GRID_W:(h + 1) * GRID_W]
            o_ref[0, pl.ds(q0, GRID_W), lanes] = out.astype(BF16)
        return carry

    lax.fori_loop(0, ATT_ROWS, one_row, 0, unroll=True)


def _natten(q, k, v, k_c, v_c, bias):
    b, s, w = q.shape
    l = k_c.shape[1]
    grid_rows = s // GRID_W
    tq = ATT_ROWS * GRID_W
    full = pl.BlockSpec((1, s, w), lambda i, j: (i, 0, 0))
    ctx = pl.BlockSpec((1, l, w), lambda i, j: (i, 0, 0))
    return pl.pallas_call(
        functools.partial(_natten_kernel, grid_rows=grid_rows),
        grid=(b, grid_rows // ATT_ROWS),
        in_specs=[pl.BlockSpec((1, tq, w), lambda i, j: (i, j, 0)),
                  full, full, ctx, ctx,
                  pl.BlockSpec(bias.shape, lambda i, j: (0, 0, 0), pipeline_mode=pl.Buffered(1))],
        out_specs=pl.BlockSpec((1, tq, w), lambda i, j: (i, j, 0)),
        out_shape=jax.ShapeDtypeStruct((b, s, w), BF16),
        compiler_params=_params("parallel", "arbitrary"),
        name="natten",
    )(q, k, v, k_c, v_c, bias)


def _mix_out_kernel(att_ref, sgu_ref, x_ref, wo_ref, g1_ref, sh_ref, sc_ref, lng_ref, lnb_ref,
                    wr_ref, br_ref, x1_ref, t_ref, lg_ref):
    mix = (jnp.dot(att_ref[0], wo_ref[:NA_WIDTH, :], preferred_element_type=F32)
           + jnp.dot(sgu_ref[0], wo_ref[NA_WIDTH:, :], preferred_element_type=F32))
    x1 = _normalize(DEEPNORM_ALPHA * x_ref[0] + g1_ref[0] * mix) * lng_ref[...] + lnb_ref[...]
    x1_ref[0] = x1
    t = _normalize(x1) * (1.0 + sc_ref[0]) + sh_ref[0]
    rows = t.shape[0]
    for s in range(t.shape[1] // LANES):
        t_ref[pl.ds(s, rows, stride=SUBLANES), :] = t[:, s * LANES:(s + 1) * LANES]
    lg_ref[...] = _dot_split3(wr_ref[...], t, (((1,), (1,)), ((), ()))) + br_ref[...]


def _mix_out(att, sgu, x, w_out, g1, sh2, sc2, ln_g, ln_b, router_w, router_b):
    b, s, d = x.shape
    assert d == SUBLANES * LANES
    tm = min(ROW_TILE, s)
    nj = s // tm
    row = lambda width: pl.BlockSpec((1, tm, width), lambda i, j: (i, j, 0))
    mod_spec = pl.BlockSpec((1, 1, d), lambda i, j: (i, 0, 0))
    vec_spec = pl.BlockSpec((1, d), lambda i, j: (0, 0))
    return pl.pallas_call(
        _mix_out_kernel,
        grid=(b, nj),
        in_specs=[row(NA_WIDTH), row(SGU_WIDTH), row(d),
                  pl.BlockSpec((d, d), lambda i, j: (0, 0)),
                  mod_spec, mod_spec, mod_spec, vec_spec, vec_spec,
                  pl.BlockSpec((N_EXPERTS, d), lambda i, j: (0, 0)),
                  pl.BlockSpec((N_EXPERTS, 1), lambda i, j: (0, 0))],
        out_specs=[row(d),
                   pl.BlockSpec((tm * SUBLANES, LANES), lambda i, j: (i * nj + j, 0)),
                   pl.BlockSpec((N_EXPERTS, tm), lambda i, j: (0, i * nj + j))],
        out_shape=[jax.ShapeDtypeStruct((b, s, d), F32),
                   jax.ShapeDtypeStruct((b * s * SUBLANES, LANES), F32),
                   jax.ShapeDtypeStruct((N_EXPERTS, b * s), F32)],
        compiler_params=_params("parallel", "parallel"),
        name="mix_out",
    )(att, sgu, x, w_out, g1.reshape(b, 1, d), sh2.reshape(b, 1, d), sc2.reshape(b, 1, d),
      ln_g.reshape(1, d), ln_b.reshape(1, d), router_w.T, router_b.reshape(N_EXPERTS, 1))


def _route_kernel(lg_ref, e_ref, r_ref, g_ref, cnt_ref, run_ref):
    @pl.when(pl.program_id(0) == 0)
    def _():
        run_ref[...] = jnp.zeros_like(run_ref)

    logits = lg_ref[...]
    n_tok = logits.shape[1]
    expert = lax.broadcasted_iota(jnp.int32, logits.shape, 0)
    work = logits
    picks, tops = [], []
    for kk in range(TOP_K):
        m = jnp.max(work, axis=0, keepdims=True)
        first = jnp.min(jnp.where(work == m, expert, N_EXPERTS), axis=0, keepdims=True)
        pick = expert == first
        work = jnp.where(pick, -jnp.inf, work)
        picks.append(pick)
        tops.append(m)
        e_ref[kk:kk + 1, :] = first
    weights = [jnp.exp(m - tops[0]) for m in tops]
    denom = weights[0]
    for w in weights[1:]:
        denom = denom + w
    for kk in range(TOP_K):
        g_ref[kk:kk + 1, :] = weights[kk] / denom

    chosen = picks[0]
    for pick in picks[1:]:
        chosen = jnp.logical_or(chosen, pick)
    tri = (lax.broadcasted_iota(jnp.int32, (SCAN_BLOCK, SCAN_BLOCK), 0)
           <= lax.broadcasted_iota(jnp.int32, (SCAN_BLOCK, SCAN_BLOCK), 1)).astype(BF16)
    sel = jnp.where(chosen, 1.0, 0.0).astype(BF16)
    carry = run_ref[:, 0:1]
    for blk in range(n_tok // SCAN_BLOCK):
        lo = blk * SCAN_BLOCK
        run = jnp.dot(sel[:, lo:lo + SCAN_BLOCK], tri, preferred_element_type=F32) + carry
        for kk in range(TOP_K):
            rank = jnp.sum(jnp.where(picks[kk][:, lo:lo + SCAN_BLOCK], run - 1.0, 0.0),
                           axis=0, keepdims=True)
            r_ref[kk:kk + 1, lo:lo + SCAN_BLOCK] = rank.astype(jnp.int32)
        carry = run[:, SCAN_BLOCK - 1:SCAN_BLOCK]
    run_ref[...] = jnp.broadcast_to(carry, run_ref.shape)
    cnt_ref[...] = jnp.broadcast_to(carry, cnt_ref.shape).astype(jnp.int32)


def _route(logits_t):
    n = logits_t.shape[1]
    chunk = min(ROUTE_CHUNK, n)
    assert n % chunk == 0 and chunk % SCAN_BLOCK == 0
    per_k = pl.BlockSpec((TOP_K, chunk), lambda c: (0, c))
    return pl.pallas_call(
        _route_kernel,
        grid=(n // chunk,),
        in_specs=[pl.BlockSpec((N_EXPERTS, chunk), lambda c: (0, c))],
        out_specs=[per_k, per_k, per_k, pl.BlockSpec((N_EXPERTS, LANES), lambda c: (0, 0))],
        out_shape=[jax.ShapeDtypeStruct((TOP_K, n), jnp.int32),
                   jax.ShapeDtypeStruct((TOP_K, n), jnp.int32),
                   jax.ShapeDtypeStruct((TOP_K, n), F32),
                   jax.ShapeDtypeStruct((N_EXPERTS, LANES), jnp.int32)],
        scratch_shapes=[pltpu.VMEM((N_EXPERTS, LANES), F32)],
        compiler_params=_params("arbitrary"),
        name="route",
    )(logits_t)


def _plan(expert_k, rank_k, counts):
    padded = (counts + EXPERT_ROWS - 1) // EXPERT_ROWS * EXPERT_ROWS
    group_end = jnp.cumsum(padded)
    group_start = group_end - padded
    start_k = jnp.sum(jnp.where(expert_k[None] == jnp.arange(N_EXPERTS)[:, None, None],
                                group_start[:, None, None], 0), axis=0)
    slot = (start_k + rank_k).reshape(-1)
    tile_start = jnp.concatenate([group_start, group_end[-1:]]) // EXPERT_ROWS
    return (slot.astype(jnp.int32), tile_start.astype(jnp.int32),
            group_start.astype(jnp.int32), counts.astype(jnp.int32), padded.astype(jnp.int32))


def _dispatch_kernel(slot_ref, start_ref, cnt_ref, pad_ref, nt_ref, t_ref, xs_ref, zero_ref,
                     sem, zsem):
    tokens = t_ref.shape[0] // SUBLANES
    n_tokens = tokens * pl.num_programs(0)
    base = pl.program_id(0) * tokens
    tile_rows = zero_ref.shape[0]
    n_tiles_max = xs_ref.shape[0] // tile_rows

    def row_copy(src, row, sem_):
        dst = xs_ref.at[pl.ds(pl.multiple_of(row * SUBLANES, SUBLANES), SUBLANES), :]
        return pltpu.make_async_copy(src, dst, sem_)

    def tile_copy(tile):
        dst = xs_ref.at[pl.ds(pl.multiple_of(tile * tile_rows, tile_rows), tile_rows), :]
        return pltpu.make_async_copy(zero_ref, dst, zsem)

    @pl.when(pl.program_id(0) == 0)
    def _():
        zero_ref[...] = jnp.zeros_like(zero_ref)

        def padded_tile(e):
            return (start_ref[e] + pad_ref[e]) // (tile_rows // SUBLANES) - 1

        def fill_group(e, carry):
            @pl.when(pad_ref[e] > cnt_ref[e])
            def _():
                tile_copy(padded_tile(e)).start()
            return carry

        def drain_group(e, carry):
            @pl.when(pad_ref[e] > cnt_ref[e])
            def _():
                tile_copy(padded_tile(e)).wait()
            return carry

        lax.fori_loop(0, N_EXPERTS, fill_group, 0)
        lax.fori_loop(0, N_EXPERTS, drain_group, 0)

        def fill_tile(i, carry):
            tile_copy(i).start()
            return carry

        def drain_tile(i, carry):
            tile_copy(i).wait()
            return carry

        lax.fori_loop(nt_ref[0], n_tiles_max, fill_tile, 0)
        lax.fori_loop(nt_ref[0], n_tiles_max, drain_tile, 0)

    def scatter(i, carry):
        src = t_ref.at[pl.ds(pl.multiple_of(i * SUBLANES, SUBLANES), SUBLANES), :]
        for kk in range(TOP_K):
            row_copy(src, slot_ref[kk * n_tokens + base + i], sem).start(priority=kk % 2)
        return carry

    lax.fori_loop(0, tokens, scatter, 0, unroll=8)
    for kk in range(TOP_K):
        pltpu.make_async_copy(t_ref, xs_ref.at[pl.ds(0, tokens * SUBLANES), :], sem).wait()


def _dispatch(slot, group_start, counts, padded, n_tiles, t_tiles, n_rows):
    n = t_tiles.shape[0] // SUBLANES
    tokens = min(DISPATCH_TOKENS, n)
    assert n % tokens == 0 and n_rows % EXPERT_ROWS == 0
    grid_spec = pltpu.PrefetchScalarGridSpec(
        num_scalar_prefetch=5,
        grid=(n // tokens,),
        in_specs=[pl.BlockSpec((tokens * SUBLANES, LANES), lambda j, *_: (j, 0))],
        out_specs=pl.BlockSpec(memory_space=pl.ANY),
        scratch_shapes=[pltpu.VMEM((EXPERT_ROWS * SUBLANES, LANES), F32),
                        pltpu.SemaphoreType.DMA, pltpu.SemaphoreType.DMA],
    )
    return pl.pallas_call(
        _dispatch_kernel,
        grid_spec=grid_spec,
        out_shape=jax.ShapeDtypeStruct((n_rows * SUBLANES, LANES), F32),
        compiler_params=_params("arbitrary"),
        name="dispatch",
    )(slot, group_start, counts, padded, n_tiles, t_tiles)


def _experts_kernel(ts_ref, end_ref, xs_ref, wgu_ref, bgu_ref, wd_ref, bd_ref, ys_ref,
                    wgu_bf, wd_bf, xbuf, ybuf, xsem, ysem):
    e = pl.program_id(0)
    tile_rows = xbuf.shape[1]
    rows = tile_rows // SUBLANES
    d_ff = wd_ref.shape[1]
    n_blocks = wgu_ref.shape[1] // LANES
    n_total = ts_ref[N_EXPERTS]

    def hbm_tile(ref, t):
        return ref.at[pl.ds(pl.multiple_of(t * tile_rows, tile_rows), tile_rows), :]

    def x_copy(t, which):
        return pltpu.make_async_copy(hbm_tile(xs_ref, t), xbuf.at[which], xsem.at[which])

    def y_copy(t, which):
        return pltpu.make_async_copy(ybuf.at[which], hbm_tile(ys_ref, t), ysem.at[which])

    @pl.when(e == 0)
    def _():
        ybuf[...] = jnp.zeros_like(ybuf)
        x_copy(0, 0).start()

    @pl.when(ts_ref[e] < ts_ref[e + 1])
    def _():
        wgu_bf[...] = wgu_ref[0].astype(BF16)
        wd_bf[...] = wd_ref[0].astype(BF16)

        def one_tile(t, carry):
            cur = t % 2

            @pl.when(t + 1 < n_total)
            def _():
                x_copy(t + 1, 1 - cur).start()

            x_copy(t, cur).wait()

            @pl.when(t >= 2)
            def _():
                y_copy(t - 2, cur).wait()

            def ffn(n_rows):
                x = jnp.concatenate(
                    [xbuf[cur, pl.ds(s, n_rows, stride=SUBLANES), :] for s in range(n_blocks)],
                    axis=1).astype(BF16)
                gu = jnp.dot(x, wgu_bf[...], preferred_element_type=F32) + bgu_ref[0]
                gate = jnp.minimum(gu[:, :d_ff], SWIGLU_LIMIT)
                lin = jnp.clip(gu[:, d_ff:], -SWIGLU_LIMIT, SWIGLU_LIMIT)
                act = ((lin + 1.0) * (gate * jax.nn.sigmoid(SWIGLU_ALPHA * gate))).astype(BF16)
                y = jnp.dot(act, wd_bf[...], preferred_element_type=F32) + bd_ref[0]
                for s in range(n_blocks):
                    ybuf[cur, pl.ds(s, n_rows, stride=SUBLANES), :] = y[:, s * LANES:(s + 1) * LANES]

            live = end_ref[e] - t * rows

            step = rows // EXPERT_PATHS
            for q in range(1, EXPERT_PATHS + 1):
                lo = (q - 1) * step if q > 1 else -1
                cond = (live > lo if q == EXPERT_PATHS
                        else jnp.logical_and(live > lo, live <= q * step))
                pl.when(cond)(functools.partial(ffn, q * step))

            y_copy(t, cur).start()
            return carry

        lax.fori_loop(ts_ref[e], ts_ref[e + 1], one_tile, 0)

    @pl.when(e == pl.num_programs(0) - 1)
    def _():
        y_copy(n_total - 1, (n_total - 1) % 2).wait()

        @pl.when(n_total >= 2)
        def _():
            y_copy(n_total - 2, n_total % 2).wait()


def _experts(tile_start, token_end, xs, w_gu, b_gu, w_down, b_down):
    n_exp, d, two_ff = w_gu.shape
    d_ff = two_ff // 2
    by_expert = lambda e, ts, end: (e, 0, 0)
    tile_rows = EXPERT_ROWS * SUBLANES
    grid_spec = pltpu.PrefetchScalarGridSpec(
        num_scalar_prefetch=2,
        grid=(n_exp,),
        in_specs=[pl.BlockSpec(memory_space=pl.ANY),
                  pl.BlockSpec((1, d, two_ff), by_expert),
                  pl.BlockSpec((1, 1, two_ff), by_expert),
                  pl.BlockSpec((1, d_ff, d), by_expert),
                  pl.BlockSpec((1, 1, d), by_expert)],
        out_specs=pl.BlockSpec(memory_space=pl.ANY),
        scratch_shapes=[pltpu.VMEM((d, two_ff), BF16), pltpu.VMEM((d_ff, d), BF16),
                        pltpu.VMEM((2, tile_rows, LANES), F32),
                        pltpu.VMEM((2, tile_rows, LANES), F32),
                        pltpu.SemaphoreType.DMA((2,)), pltpu.SemaphoreType.DMA((2,))],
    )
    return pl.pallas_call(
        _experts_kernel,
        grid_spec=grid_spec,
        out_shape=jax.ShapeDtypeStruct(xs.shape, F32),
        input_output_aliases={2: 0},
        compiler_params=_params("arbitrary"),
        name="experts",
    )(tile_start, token_end, xs, w_gu, b_gu.reshape(n_exp, 1, two_ff), w_down,
      b_down.reshape(n_exp, 1, d))


def _combine_kernel(slot_ref, gate_ref, ys_ref, x1_ref, g2_ref, lng_ref, lnb_ref, o_ref,
                    buf, acc_ref, sem):
    j = pl.program_id(0)
    tokens = o_ref.shape[0]
    n_tokens = tokens * pl.num_programs(0)
    n_blocks = o_ref.shape[1] // LANES
    tile_rows = TOP_K * SUBLANES

    def gather_token(block, which, i):
        for kk in range(TOP_K):
            row = slot_ref[kk * n_tokens + block * tokens + i]
            src = ys_ref.at[pl.ds(pl.multiple_of(row * SUBLANES, SUBLANES), SUBLANES), :]
            dst = buf.at[which, pl.ds(pl.multiple_of((i * TOP_K + kk) * SUBLANES, SUBLANES),
                                      SUBLANES), :]
            pltpu.make_async_copy(src, dst, sem.at[which]).start(priority=kk % 2)

    def reduce_token(which, i):
        tile = None
        for kk in range(TOP_K):
            row0 = pl.multiple_of((i * TOP_K + kk) * SUBLANES, SUBLANES)
            part = gate_ref[kk * n_tokens + j * tokens + i] * buf[which, pl.ds(row0, SUBLANES), :]
            tile = part if tile is None else tile + part
        acc_ref[pl.ds(pl.multiple_of(i * SUBLANES, SUBLANES), SUBLANES), :] = tile

    @pl.when(j == 0)
    def _():
        def first(i, carry):
            gather_token(0, 0, i)
            return carry
        lax.fori_loop(0, tokens, first, 0, unroll=8)

    def block(cur):
        pltpu.make_async_copy(ys_ref.at[pl.ds(0, tokens * tile_rows), :], buf.at[cur],
                              sem.at[cur]).wait()

        @pl.when(j + 1 < pl.num_programs(0))
        def _():
            def step(i, carry):
                gather_token(j + 1, 1 - cur, i)
                reduce_token(cur, i)
                return carry
            lax.fori_loop(0, tokens, step, 0, unroll=16)

        @pl.when(j + 1 == pl.num_programs(0))
        def _():
            def step(i, carry):
                reduce_token(cur, i)
                return carry
            lax.fori_loop(0, tokens, step, 0, unroll=8)

    for parity in range(2):
        pl.when(j % 2 == parity)(functools.partial(block, parity))

    ffn = jnp.concatenate([acc_ref[pl.ds(s, tokens, stride=SUBLANES), :] for s in range(n_blocks)],
                          axis=1)
    z = DEEPNORM_ALPHA * x1_ref[...] + g2_ref[0] * ffn
    o_ref[...] = _normalize(z) * lng_ref[...] + lnb_ref[...]


def _combine(slot, gates_flat, ys, x1, g2, ln_g, ln_b, tokens_per_sample):
    n, d = x1.shape
    tokens = min(COMBINE_TOKENS, tokens_per_sample)
    assert tokens_per_sample % tokens == 0
    blocks_per_sample = tokens_per_sample // tokens
    vec = pl.BlockSpec((1, d), lambda j, *_: (0, 0))
    grid_spec = pltpu.PrefetchScalarGridSpec(
        num_scalar_prefetch=2,
        grid=(n // tokens,),
        in_specs=[pl.BlockSpec(memory_space=pl.ANY),
                  pl.BlockSpec((tokens, d), lambda j, *_: (j, 0)),
                  pl.BlockSpec((1, 1, d), lambda j, *_: (j // blocks_per_sample, 0, 0)),
                  vec, vec],
        out_specs=pl.BlockSpec((tokens, d), lambda j, *_: (j, 0)),
        scratch_shapes=[pltpu.VMEM((2, tokens * TOP_K * SUBLANES, LANES), F32),
                        pltpu.VMEM((tokens * SUBLANES, LANES), F32),
                        pltpu.SemaphoreType.DMA((2,))],
    )
    return pl.pallas_call(
        _combine_kernel,
        grid_spec=grid_spec,
        out_shape=jax.ShapeDtypeStruct((n, d), F32),
        compiler_params=_params("arbitrary"),
        name="combine",
    )(slot, gates_flat, ys, x1, g2, ln_g.reshape(1, d), ln_b.reshape(1, d))


def _layer(x, c, ctx, c_ctx, ada_w, ada_b, w_in, rpb, sgu_ln_g, sgu_ln_b, sgu_w, sgu_b, w_out,
           ln1_g, ln1_b, ln2_g, ln2_b, router_w, router_b, w_gu, b_gu, w_down, b_down):
    b, s, d = x.shape
    n = b * s
    assert s % (GRID_W * ATT_ROWS) == 0 and s // GRID_W >= NA_KH

    cond_rows = jnp.zeros((8, d), F32).at[:b].set(c).at[b].set(c_ctx)
    mod = _ada(cond_rows, ada_w, ada_b)
    sh1, sc1, g1, sh2, sc2, g2 = jnp.split(mod[:b], 6, axis=-1)
    csh1, csc1 = mod[b, :d], mod[b, d:2 * d]

    w_in_bf = w_in.astype(BF16)
    k_c, v_c = _ctx_kv(ctx, csh1, csc1, w_in_bf[:, NA_WIDTH:3 * NA_WIDTH])
    q, k, v, sgu = _proj_in(x, sh1, sc1, w_in_bf, sgu_ln_g, sgu_ln_b, sgu_w, sgu_b)
    att = _natten(q, k, v, k_c, v_c, _bias_table(rpb))
    x1, t_tiles, logits_t = _mix_out(att, sgu, x, w_out.astype(BF16), g1, sh2, sc2, ln1_g, ln1_b,
                                     router_w, router_b)

    expert_k, rank_k, gate_k, counts = _route(logits_t)
    n_tiles_max = n * TOP_K // EXPERT_ROWS + N_EXPERTS
    slot, tile_start, group_start, counts, padded = _plan(expert_k, rank_k, counts[:, 0])
    xs = _dispatch(slot, group_start, counts, padded, tile_start[N_EXPERTS:], t_tiles,
                   n_tiles_max * EXPERT_ROWS)
    ys = _experts(tile_start, group_start + counts, xs, w_gu, b_gu, w_down, b_down)
    out = _combine(slot, gate_k.reshape(-1), ys, x1.reshape(n, d), g2.reshape(b, 1, d),
                   ln2_g, ln2_b, s)
    return out.reshape(b, s, d)


def kernel(x, c, ctx, c_ctx, ada_w, ada_b, w_in, rpb, sgu_ln_g, sgu_ln_b, sgu_w, sgu_b, w_out,
           ln1_g, ln1_b, ln2_g, ln2_b, router_w, router_b, exp_w_gu, exp_b_gu, exp_w_down,
           exp_b_down):
    assert ada_w.shape[0] == DEPTH
    return _layer(x, c, ctx, c_ctx, ada_w[0], ada_b[0], w_in[0], rpb[0], sgu_ln_g[0], sgu_ln_b[0],
                  sgu_w[0], sgu_b[0], w_out[0], ln1_g[0], ln1_b[0], ln2_g[0], ln2_b[0],
                  router_w[0], router_b[0], exp_w_gu[0], exp_b_gu[0], exp_w_down[0], exp_b_down[0])
```

```python
import functools

import jax
import jax.numpy as jnp
from jax import lax
from jax.experimental import pallas as pl
from jax.experimental.pallas import tpu as pltpu

F32 = jnp.float32
BF16 = jnp.bfloat16

GRID_W = 64
NA_HEADS = 8
NA_HEAD_DIM = 64
NA_WIDTH = NA_HEADS * NA_HEAD_DIM
NA_KH = 8
NA_KW = 16
SGU_GROUPS = 4
SGU_GROUP_DIM = 128
SGU_WIDTH = SGU_GROUPS * SGU_GROUP_DIM
CHUNK = 128
N_EXPERTS = 32
TOP_K = 4
SWIGLU_LIMIT = 7.0
SWIGLU_ALPHA = 1.702
LN_EPS = 1e-5
DEPTH = 1
DEEPNORM_ALPHA = (2.0 * DEPTH) ** 0.25
MASKED = -1e30

SUBLANES = 8
LANES = 128
VMEM_LIMIT_BYTES = 52 * 1024 * 1024

ROW_TILE = 1024
ATT_ROWS = 8
ATT_HEAD_GROUP = 4
ROUTE_CHUNK = 2048
SCAN_BLOCK = 256
EXPERT_ROWS = 512
EXPERT_PATHS = 8
DISPATCH_TOKENS = 4096
COMBINE_TOKENS = 512


def _params(*sem):
    return pltpu.CompilerParams(dimension_semantics=sem, vmem_limit_bytes=VMEM_LIMIT_BYTES)


def _dot_split3(a, b, dims):
    a_hi = a.astype(BF16)
    a_lo = (a - a_hi.astype(F32)).astype(BF16)
    b_hi = b.astype(BF16)
    b_lo = (b - b_hi.astype(F32)).astype(BF16)
    dot = functools.partial(lax.dot_general, dimension_numbers=dims, preferred_element_type=F32)
    return dot(a_hi, b_hi) + dot(a_hi, b_lo) + dot(a_lo, b_hi)


def _normalize(x):
    mu = jnp.mean(x, axis=-1, keepdims=True)
    xc = x - mu
    var = jnp.mean(xc * xc, axis=-1, keepdims=True)
    return xc * lax.rsqrt(var + LN_EPS)


def _ada_kernel(c_ref, w_ref, b_ref, o_ref):
    s = c_ref[...]
    s = s * jax.nn.sigmoid(s)
    o_ref[...] = _dot_split3(s, w_ref[...], (((1,), (0,)), ((), ()))) + b_ref[...]


def _ada(cond_rows, ada_w, ada_b):
    d = cond_rows.shape[1]
    n_out = ada_w.shape[1]
    return pl.pallas_call(
        _ada_kernel,
        grid=(n_out // d,),
        in_specs=[pl.BlockSpec((8, d), lambda j: (0, 0)),
                  pl.BlockSpec((d, d), lambda j: (0, j)),
                  pl.BlockSpec((1, d), lambda j: (0, j))],
        out_specs=pl.BlockSpec((8, d), lambda j: (0, j)),
        out_shape=jax.ShapeDtypeStruct((8, n_out), F32),
        compiler_params=_params("arbitrary"),
        name="ada",
    )(cond_rows, ada_w, ada_b.reshape(1, n_out))


def _ctx_kv_kernel(ctx_ref, sh_ref, sc_ref, w_ref, k_ref, v_ref):
    h = _normalize(ctx_ref[0]) * (1.0 + sc_ref[...]) + sh_ref[...]
    kv = jnp.dot(h.astype(BF16), w_ref[...], preferred_element_type=F32)
    k_ref[0] = kv[:, :NA_WIDTH].astype(BF16)
    v_ref[0] = kv[:, NA_WIDTH:].astype(BF16)


def _ctx_kv(ctx, csh1, csc1, w_kv):
    b, l, d = ctx.shape
    out = jax.ShapeDtypeStruct((b, l, NA_WIDTH), BF16)
    return pl.pallas_call(
        _ctx_kv_kernel,
        grid=(b,),
        in_specs=[pl.BlockSpec((1, l, d), lambda i: (i, 0, 0)),
                  pl.BlockSpec((1, d), lambda i: (0, 0)),
                  pl.BlockSpec((1, d), lambda i: (0, 0)),
                  pl.BlockSpec((d, 2 * NA_WIDTH), lambda i: (0, 0))],
        out_specs=[pl.BlockSpec((1, l, NA_WIDTH), lambda i: (i, 0, 0)),
                   pl.BlockSpec((1, l, NA_WIDTH), lambda i: (i, 0, 0))],
        out_shape=[out, out],
        compiler_params=_params("arbitrary"),
        name="ctx_kv",
    )(ctx, csh1.reshape(1, d), csc1.reshape(1, d), w_kv)


def _proj_in_kernel(x_ref, sh_ref, sc_ref, w_ref, lng_ref, lnb_ref, ws_ref, bs_ref,
                    q_ref, k_ref, v_ref, s_ref):
    h = (_normalize(x_ref[0]) * (1.0 + sc_ref[0]) + sh_ref[0]).astype(BF16)

    def proj(lo, width):
        return jnp.dot(h, w_ref[:, lo:lo + width], preferred_element_type=F32)

    q_ref[0] = (proj(0, NA_WIDTH) * (NA_HEAD_DIM ** -0.5)).astype(BF16)
    k_ref[0] = proj(NA_WIDTH, NA_WIDTH).astype(BF16)
    v_ref[0] = proj(2 * NA_WIDTH, NA_WIDTH).astype(BF16)
    u = jax.nn.gelu(proj(3 * NA_WIDTH, SGU_WIDTH))
    g = jax.nn.gelu(proj(3 * NA_WIDTH + SGU_WIDTH, SGU_WIDTH))
    gn = (_normalize(g) * lng_ref[...] + lnb_ref[...]).astype(BF16)
    rows = h.shape[0]
    for n in range(rows // CHUNK):
        r0 = n * CHUNK
        for grp in range(SGU_GROUPS):
            c0 = grp * SGU_GROUP_DIM
            mixed = jnp.dot(ws_ref[grp], gn[r0:r0 + CHUNK, c0:c0 + SGU_GROUP_DIM],
                            preferred_element_type=F32) + bs_ref[grp]
            s_ref[0, r0:r0 + CHUNK, c0:c0 + SGU_GROUP_DIM] = (
                u[r0:r0 + CHUNK, c0:c0 + SGU_GROUP_DIM] * mixed).astype(BF16)


def _proj_in(x, sh1, sc1, w_in, sgu_ln_g, sgu_ln_b, sgu_w, sgu_b):
    b, s, d = x.shape
    d_in = w_in.shape[1]
    tm = min(ROW_TILE, s)
    out = jax.ShapeDtypeStruct((b, s, NA_WIDTH), BF16)
    row_spec = pl.BlockSpec((1, tm, NA_WIDTH), lambda i, j: (i, j, 0))
    mod_spec = pl.BlockSpec((1, 1, d), lambda i, j: (i, 0, 0))
    bs = jnp.broadcast_to(sgu_b[:, :, None], (SGU_GROUPS, CHUNK, SGU_GROUP_DIM))
    return pl.pallas_call(
        _proj_in_kernel,
        grid=(b, s // tm),
        in_specs=[pl.BlockSpec((1, tm, d), lambda i, j: (i, j, 0)),
                  mod_spec, mod_spec,
                  pl.BlockSpec((d, d_in), lambda i, j: (0, 0)),
                  pl.BlockSpec((1, SGU_WIDTH), lambda i, j: (0, 0)),
                  pl.BlockSpec((1, SGU_WIDTH), lambda i, j: (0, 0)),
                  pl.BlockSpec((SGU_GROUPS, CHUNK, CHUNK), lambda i, j: (0, 0, 0)),
                  pl.BlockSpec((SGU_GROUPS, CHUNK, SGU_GROUP_DIM), lambda i, j: (0, 0, 0))],
        out_specs=[row_spec, row_spec, row_spec, row_spec],
        out_shape=[out, out, out, out],
        compiler_params=_params("parallel", "parallel"),
        name="proj_in",
    )(x, sh1.reshape(b, 1, d), sc1.reshape(b, 1, d), w_in,
      sgu_ln_g.reshape(1, SGU_WIDTH), sgu_ln_b.reshape(1, SGU_WIDTH), sgu_w.astype(BF16), bs)


def _bias_kernel(rpb_ref, o_ref, toe_ref):
    n_dr, n_dc = 2 * NA_KH - 1, 2 * NA_KW - 1
    pair = (GRID_W, 2 * GRID_W)
    c = lax.broadcasted_iota(jnp.int32, pair, 0)
    lane = lax.broadcasted_iota(jnp.int32, pair, 1)
    kc = lane % GRID_W
    cs = jnp.clip(c - NA_KW // 2, 0, GRID_W - NA_KW)
    valid = jnp.logical_and(kc >= cs, kc < cs + NA_KW)
    dc = kc - c + NA_KW - 1

    def expand(hd, carry):
        acc = jnp.zeros(pair, F32)
        for j in range(n_dc):
            acc = jnp.where(dc == j, rpb_ref[hd * n_dc + j], acc)
        toe_ref[hd] = jnp.where(valid, acc, MASKED)
        return carry

    lax.fori_loop(0, NA_HEADS * n_dr, expand, 0)

    first_half = lane < GRID_W
    for o in range(NA_KH):
        for h in range(NA_HEADS):
            for p in range(NA_KH // 2):
                dr = NA_KH - 1 - o + 2 * p
                block = jnp.where(first_half, toe_ref[h * n_dr + dr], toe_ref[h * n_dr + dr + 1])
                o_ref[o, h * GRID_W:(h + 1) * GRID_W, p * 2 * GRID_W:(p + 1) * 2 * GRID_W] = block


def _bias_table(rpb):
    heads, n_dr, n_dc = rpb.shape
    assert (heads, n_dr, n_dc) == (NA_HEADS, 2 * NA_KH - 1, 2 * NA_KW - 1)
    shape = (NA_KH, NA_HEADS * GRID_W, NA_KH * GRID_W)
    grid_spec = pltpu.PrefetchScalarGridSpec(
        num_scalar_prefetch=1,
        grid=(1,),
        in_specs=[],
        out_specs=pl.BlockSpec(shape, lambda i, rpb_ref: (0, 0, 0)),
        scratch_shapes=[pltpu.VMEM((heads * n_dr, GRID_W, 2 * GRID_W), F32)],
    )
    return pl.pallas_call(
        _bias_kernel,
        grid_spec=grid_spec,
        out_shape=jax.ShapeDtypeStruct(shape, F32),
        compiler_params=_params("arbitrary"),
        name="bias_table",
    )(rpb.astype(F32).reshape(-1))


def _natten_kernel(q_ref, k_ref, v_ref, kc_ref, vc_ref, bias_ref, o_ref, *, grid_rows):
    group_lanes = ATT_HEAD_GROUP * NA_HEAD_DIM
    group_rows = ATT_HEAD_GROUP * GRID_W
    row_head = lax.broadcasted_iota(jnp.int32, (group_rows, group_lanes), 0) // GRID_W
    lane_head = lax.broadcasted_iota(jnp.int32, (group_rows, group_lanes), 1) // NA_HEAD_DIM
    own_head = row_head == lane_head
    nt = (((1,), (1,)), ((), ()))

    def one_row(i, carry):
        r = pl.program_id(1) * ATT_ROWS + i
        rs = jnp.clip(r - NA_KH // 2, 0, grid_rows - NA_KH)
        k0 = pl.multiple_of(rs * GRID_W, GRID_W)
        q0 = pl.multiple_of(i * GRID_W, GRID_W)
        for grp in range(NA_HEADS // ATT_HEAD_GROUP):
            lanes = slice(grp * group_lanes, (grp + 1) * group_lanes)
            q = q_ref[0, pl.ds(q0, GRID_W), lanes]
            qs = jnp.where(own_head, jnp.concatenate([q] * ATT_HEAD_GROUP, axis=0),
                           jnp.zeros((), BF16))
            kr = k_ref[0, pl.ds(k0, NA_KH * GRID_W), lanes]
            vr = v_ref[0, pl.ds(k0, NA_KH * GRID_W), lanes]
            bias = bias_ref[r - rs, grp * group_rows:(grp + 1) * group_rows, :]
            s_nb = lax.dot_general(qs, kr, nt, preferred_element_type=F32) + bias
            s_cx = lax.dot_general(qs, kc_ref[0, :, lanes], nt, preferred_element_type=F32)
            m = jnp.maximum(jnp.max(s_nb, axis=-1, keepdims=True),
                            jnp.max(s_cx, axis=-1, keepdims=True))
            p_nb = jnp.exp(s_nb - m)
            p_cx = jnp.exp(s_cx - m)
            denom = jnp.sum(p_nb, axis=-1, keepdims=True) + jnp.sum(p_cx, axis=-1, keepdims=True)
            o = (jnp.dot(p_nb.astype(BF16), vr, preferred_element_type=F32)
                 + jnp.dot(p_cx.astype(BF16), vc_ref[0, :, lanes],
                           preferred_element_type=F32)) / denom
            o = jnp.where(own_head, o, 0.0)
            out = o[:GRID_W]
            for h in range(1, ATT_HEAD_GROUP):
                out = out + o[h * GRID_W:(h + 1) * GRID_W]
            o_ref[0, pl.ds(q0, GRID_W), lanes] = out.astype(BF16)
        return carry

    lax.fori_loop(0, ATT_ROWS, one_row, 0, unroll=True)


def _natten(q, k, v, k_c, v_c, bias):
    b, s, w = q.shape
    l = k_c.shape[1]
    grid_rows = s // GRID_W
    tq = ATT_ROWS * GRID_W
    full = pl.BlockSpec((1, s, w), lambda i, j: (i, 0, 0))
    ctx = pl.BlockSpec((1, l, w), lambda i, j: (i, 0, 0))
    return pl.pallas_call(
        functools.partial(_natten_kernel, grid_rows=grid_rows),
        grid=(b, grid_rows // ATT_ROWS),
        in_specs=[pl.BlockSpec((1, tq, w), lambda i, j: (i, j, 0)),
                  full, full, ctx, ctx,
                  pl.BlockSpec(bias.shape, lambda i, j: (0, 0, 0), pipeline_mode=pl.Buffered(1))],
        out_specs=pl.BlockSpec((1, tq, w), lambda i, j: (i, j, 0)),
        out_shape=jax.ShapeDtypeStruct((b, s, w), BF16),
        compiler_params=_params("parallel", "arbitrary"),
        name="natten",
    )(q, k, v, k_c, v_c, bias)


def _mix_out_kernel(att_ref, sgu_ref, x_ref, wo_ref, g1_ref, sh_ref, sc_ref, lng_ref, lnb_ref,
                    wr_ref, br_ref, x1_ref, t_ref, lg_ref):
    mix = (jnp.dot(att_ref[0], wo_ref[:NA_WIDTH, :], preferred_element_type=F32)
           + jnp.dot(sgu_ref[0], wo_ref[NA_WIDTH:, :], preferred_element_type=F32))
    x1 = _normalize(DEEPNORM_ALPHA * x_ref[0] + g1_ref[0] * mix) * lng_ref[...] + lnb_ref[...]
    x1_ref[0] = x1
    t = _normalize(x1) * (1.0 + sc_ref[0]) + sh_ref[0]
    rows = t.shape[0]
    for s in range(t.shape[1] // LANES):
        t_ref[pl.ds(s, rows, stride=SUBLANES), :] = t[:, s * LANES:(s + 1) * LANES]
    lg_ref[...] = _dot_split3(wr_ref[...], t, (((1,), (1,)), ((), ()))) + br_ref[...]


def _mix_out(att, sgu, x, w_out, g1, sh2, sc2, ln_g, ln_b, router_w, router_b):
    b, s, d = x.shape
    assert d == SUBLANES * LANES
    tm = min(ROW_TILE, s)
    nj = s // tm
    row = lambda width: pl.BlockSpec((1, tm, width), lambda i, j: (i, j, 0))
    mod_spec = pl.BlockSpec((1, 1, d), lambda i, j: (i, 0, 0))
    vec_spec = pl.BlockSpec((1, d), lambda i, j: (0, 0))
    return pl.pallas_call(
        _mix_out_kernel,
        grid=(b, nj),
        in_specs=[row(NA_WIDTH), row(SGU_WIDTH), row(d),
                  pl.BlockSpec((d, d), lambda i, j: (0, 0)),
                  mod_spec, mod_spec, mod_spec, vec_spec, vec_spec,
                  pl.BlockSpec((N_EXPERTS, d), lambda i, j: (0, 0)),
                  pl.BlockSpec((N_EXPERTS, 1), lambda i, j: (0, 0))],
        out_specs=[row(d),
                   pl.BlockSpec((tm * SUBLANES, LANES), lambda i, j: (i * nj + j, 0)),
                   pl.BlockSpec((N_EXPERTS, tm), lambda i, j: (0, i * nj + j))],
        out_shape=[jax.ShapeDtypeStruct((b, s, d), F32),
                   jax.ShapeDtypeStruct((b * s * SUBLANES, LANES), F32),
                   jax.ShapeDtypeStruct((N_EXPERTS, b * s), F32)],
        compiler_params=_params("parallel", "parallel"),
        name="mix_out",
    )(att, sgu, x, w_out, g1.reshape(b, 1, d), sh2.reshape(b, 1, d), sc2.reshape(b, 1, d),
      ln_g.reshape(1, d), ln_b.reshape(1, d), router_w.T, router_b.reshape(N_EXPERTS, 1))


def _route_kernel(lg_ref, e_ref, r_ref, g_ref, cnt_ref, run_ref):
    @pl.when(pl.program_id(0) == 0)
    def _():
        run_ref[...] = jnp.zeros_like(run_ref)

    logits = lg_ref[...]
    n_tok = logits.shape[1]
    expert = lax.broadcasted_iota(jnp.int32, logits.shape, 0)
    work = logits
    picks, tops = [], []
    for kk in range(TOP_K):
        m = jnp.max(work, axis=0, keepdims=True)
        first = jnp.min(jnp.where(work == m, expert, N_EXPERTS), axis=0, keepdims=True)
        pick = expert == first
        work = jnp.where(pick, -jnp.inf, work)
        picks.append(pick)
        tops.append(m)
        e_ref[kk:kk + 1, :] = first
    weights = [jnp.exp(m - tops[0]) for m in tops]
    denom = weights[0]
    for w in weights[1:]:
        denom = denom + w
    for kk in range(TOP_K):
        g_ref[kk:kk + 1, :] = weights[kk] / denom

    chosen = picks[0]
    for pick in picks[1:]:
        chosen = jnp.logical_or(chosen, pick)
    tri = (lax.broadcasted_iota(jnp.int32, (SCAN_BLOCK, SCAN_BLOCK), 0)
           <= lax.broadcasted_iota(jnp.int32, (SCAN_BLOCK, SCAN_BLOCK), 1)).astype(BF16)
    sel = jnp.where(chosen, 1.0, 0.0).astype(BF16)
    carry = run_ref[:, 0:1]
    for blk in range(n_tok // SCAN_BLOCK):
        lo = blk * SCAN_BLOCK
        run = jnp.dot(sel[:, lo:lo + SCAN_BLOCK], tri, preferred_element_type=F32) + carry
        for kk in range(TOP_K):
            rank = jnp.sum(jnp.where(picks[kk][:, lo:lo + SCAN_BLOCK], run - 1.0, 0.0),
                           axis=0, keepdims=True)
            r_ref[kk:kk + 1, lo:lo + SCAN_BLOCK] = rank.astype(jnp.int32)
        carry = run[:, SCAN_BLOCK - 1:SCAN_BLOCK]
    run_ref[...] = jnp.broadcast_to(carry, run_ref.shape)
    cnt_ref[...] = jnp.broadcast_to(carry, cnt_ref.shape).astype(jnp.int32)


def _route(logits_t):
    n = logits_t.shape[1]
    chunk = min(ROUTE_CHUNK, n)
    assert n % chunk == 0 and chunk % SCAN_BLOCK == 0
    per_k = pl.BlockSpec((TOP_K, chunk), lambda c: (0, c))
    return pl.pallas_call(
        _route_kernel,
        grid=(n // chunk,),
        in_specs=[pl.BlockSpec((N_EXPERTS, chunk), lambda c: (0, c))],
        out_specs=[per_k, per_k, per_k, pl.BlockSpec((N_EXPERTS, LANES), lambda c: (0, 0))],
        out_shape=[jax.ShapeDtypeStruct((TOP_K, n), jnp.int32),
                   jax.ShapeDtypeStruct((TOP_K, n), jnp.int32),
                   jax.ShapeDtypeStruct((TOP_K, n), F32),
                   jax.ShapeDtypeStruct((N_EXPERTS, LANES), jnp.int32)],
        scratch_shapes=[pltpu.VMEM((N_EXPERTS, LANES), F32)],
        compiler_params=_params("arbitrary"),
        name="route",
    )(logits_t)


def _plan(expert_k, rank_k, counts):
    padded = (counts + EXPERT_ROWS - 1) // EXPERT_ROWS * EXPERT_ROWS
    group_end = jnp.cumsum(padded)
    group_start = group_end - padded
    start_k = jnp.sum(jnp.where(expert_k[None] == jnp.arange(N_EXPERTS)[:, None, None],
                                group_start[:, None, None], 0), axis=0)
    slot = (start_k + rank_k).reshape(-1)
    tile_start = jnp.concatenate([group_start, group_end[-1:]]) // EXPERT_ROWS
    return (slot.astype(jnp.int32), tile_start.astype(jnp.int32),
            group_start.astype(jnp.int32), counts.astype(jnp.int32), padded.astype(jnp.int32))


def _dispatch_kernel(slot_ref, start_ref, cnt_ref, pad_ref, nt_ref, t_ref, xs_ref, zero_ref,
                     sem, zsem):
    tokens = t_ref.shape[0] // SUBLANES
    n_tokens = tokens * pl.num_programs(0)
    base = pl.program_id(0) * tokens
    tile_rows = zero_ref.shape[0]
    n_tiles_max = xs_ref.shape[0] // tile_rows

    def row_copy(src, row, sem_):
        dst = xs_ref.at[pl.ds(pl.multiple_of(row * SUBLANES, SUBLANES), SUBLANES), :]
        return pltpu.make_async_copy(src, dst, sem_)

    def tile_copy(tile):
        dst = xs_ref.at[pl.ds(pl.multiple_of(tile * tile_rows, tile_rows), tile_rows), :]
        return pltpu.make_async_copy(zero_ref, dst, zsem)

    @pl.when(pl.program_id(0) == 0)
    def _():
        zero_ref[...] = jnp.zeros_like(zero_ref)

        def padded_tile(e):
            return (start_ref[e] + pad_ref[e]) // (tile_rows // SUBLANES) - 1

        def fill_group(e, carry):
            @pl.when(pad_ref[e] > cnt_ref[e])
            def _():
                tile_copy(padded_tile(e)).start()
            return carry

        def drain_group(e, carry):
            @pl.when(pad_ref[e] > cnt_ref[e])
            def _():
                tile_copy(padded_tile(e)).wait()
            return carry

        lax.fori_loop(0, N_EXPERTS, fill_group, 0)
        lax.fori_loop(0, N_EXPERTS, drain_group, 0)

        def fill_tile(i, carry):
            tile_copy(i).start()
            return carry

        def drain_tile(i, carry):
            tile_copy(i).wait()
            return carry

        lax.fori_loop(nt_ref[0], n_tiles_max, fill_tile, 0)
        lax.fori_loop(nt_ref[0], n_tiles_max, drain_tile, 0)

    def scatter(i, carry):
        src = t_ref.at[pl.ds(pl.multiple_of(i * SUBLANES, SUBLANES), SUBLANES), :]
        for kk in range(TOP_K):
            row_copy(src, slot_ref[kk * n_tokens + base + i], sem).start(priority=kk % 2)
        return carry

    lax.fori_loop(0, tokens, scatter, 0, unroll=8)
    for kk in range(TOP_K):
        pltpu.make_async_copy(t_ref, xs_ref.at[pl.ds(0, tokens * SUBLANES), :], sem).wait()


def _dispatch(slot, group_start, counts, padded, n_tiles, t_tiles, n_rows):
    n = t_tiles.shape[0] // SUBLANES
    tokens = min(DISPATCH_TOKENS, n)
    assert n % tokens == 0 and n_rows % EXPERT_ROWS == 0
    grid_spec = pltpu.PrefetchScalarGridSpec(
        num_scalar_prefetch=5,
        grid=(n // tokens,),
        in_specs=[pl.BlockSpec((tokens * SUBLANES, LANES), lambda j, *_: (j, 0))],
        out_specs=pl.BlockSpec(memory_space=pl.ANY),
        scratch_shapes=[pltpu.VMEM((EXPERT_ROWS * SUBLANES, LANES), F32),
                        pltpu.SemaphoreType.DMA, pltpu.SemaphoreType.DMA],
    )
    return pl.pallas_call(
        _dispatch_kernel,
        grid_spec=grid_spec,
        out_shape=jax.ShapeDtypeStruct((n_rows * SUBLANES, LANES), F32),
        compiler_params=_params("arbitrary"),
        name="dispatch",
    )(slot, group_start, counts, padded, n_tiles, t_tiles)


def _experts_kernel(ts_ref, end_ref, xs_ref, wgu_ref, bgu_ref, wd_ref, bd_ref, ys_ref,
                    wgu_bf, wd_bf, xbuf, ybuf, xsem, ysem):
    e = pl.program_id(0)
    tile_rows = xbuf.shape[1]
    rows = tile_rows // SUBLANES
    d_ff = wd_ref.shape[1]
    n_blocks = wgu_ref.shape[1] // LANES
    n_total = ts_ref[N_EXPERTS]

    def hbm_tile(ref, t):
        return ref.at[pl.ds(pl.multiple_of(t * tile_rows, tile_rows), tile_rows), :]

    def x_copy(t, which):
        return pltpu.make_async_copy(hbm_tile(xs_ref, t), xbuf.at[which], xsem.at[which])

    def y_copy(t, which):
        return pltpu.make_async_copy(ybuf.at[which], hbm_tile(ys_ref, t), ysem.at[which])

    @pl.when(e == 0)
    def _():
        ybuf[...] = jnp.zeros_like(ybuf)
        x_copy(0, 0).start()

    @pl.when(ts_ref[e] < ts_ref[e + 1])
    def _():
        wgu_bf[...] = wgu_ref[0].astype(BF16)
        wd_bf[...] = wd_ref[0].astype(BF16)

        def one_tile(t, carry):
            for parity in range(2):
                pl.when(t % 2 == parity)(functools.partial(tile_body, t, parity))
            return carry

        def tile_body(t, cur):
            @pl.when(t + 1 < n_total)
            def _():
                x_copy(t + 1, 1 - cur).start()

            x_copy(t, cur).wait()

            @pl.when(t >= 2)
            def _():
                y_copy(t - 2, cur).wait()

            def ffn(n_rows):
                x = jnp.concatenate(
                    [xbuf[cur, pl.ds(s, n_rows, stride=SUBLANES), :] for s in range(n_blocks)],
                    axis=1).astype(BF16)
                gu = jnp.dot(x, wgu_bf[...], preferred_element_type=F32) + bgu_ref[0]
                gate = jnp.minimum(gu[:, :d_ff], SWIGLU_LIMIT)
                lin = jnp.clip(gu[:, d_ff:], -SWIGLU_LIMIT, SWIGLU_LIMIT)
                act = ((lin + 1.0) * (gate * jax.nn.sigmoid(SWIGLU_ALPHA * gate))).astype(BF16)
                y = jnp.dot(act, wd_bf[...], preferred_element_type=F32) + bd_ref[0]
                for s in range(n_blocks):
                    ybuf[cur, pl.ds(s, n_rows, stride=SUBLANES), :] = y[:, s * LANES:(s + 1) * LANES]

            live = end_ref[e] - t * rows

            step = rows // EXPERT_PATHS
            for q in range(1, EXPERT_PATHS + 1):
                lo = (q - 1) * step if q > 1 else -1
                cond = (live > lo if q == EXPERT_PATHS
                        else jnp.logical_and(live > lo, live <= q * step))
                pl.when(cond)(functools.partial(ffn, q * step))

            y_copy(t, cur).start()

        lax.fori_loop(ts_ref[e], ts_ref[e + 1], one_tile, 0)

    @pl.when(e == pl.num_programs(0) - 1)
    def _():
        y_copy(n_total - 1, (n_total - 1) % 2).wait()

        @pl.when(n_total >= 2)
        def _():
            y_copy(n_total - 2, n_total % 2).wait()


def _experts(tile_start, token_end, xs, w_gu, b_gu, w_down, b_down):
    n_exp, d, two_ff = w_gu.shape
    d_ff = two_ff // 2
    by_expert = lambda e, ts, end: (e, 0, 0)
    tile_rows = EXPERT_ROWS * SUBLANES
    grid_spec = pltpu.PrefetchScalarGridSpec(
        num_scalar_prefetch=2,
        grid=(n_exp,),
        in_specs=[pl.BlockSpec(memory_space=pl.ANY),
                  pl.BlockSpec((1, d, two_ff), by_expert),
                  pl.BlockSpec((1, 1, two_ff), by_expert),
                  pl.BlockSpec((1, d_ff, d), by_expert),
                  pl.BlockSpec((1, 1, d), by_expert)],
        out_specs=pl.BlockSpec(memory_space=pl.ANY),
        scratch_shapes=[pltpu.VMEM((d, two_ff), BF16), pltpu.VMEM((d_ff, d), BF16),
                        pltpu.VMEM((2, tile_rows, LANES), F32),
                        pltpu.VMEM((2, tile_rows, LANES), F32),
                        pltpu.SemaphoreType.DMA((2,)), pltpu.SemaphoreType.DMA((2,))],
    )
    return pl.pallas_call(
        _experts_kernel,
        grid_spec=grid_spec,
        out_shape=jax.ShapeDtypeStruct(xs.shape, F32),
        input_output_aliases={2: 0},
        compiler_params=_params("arbitrary"),
        name="experts",
    )(tile_start, token_end, xs, w_gu, b_gu.reshape(n_exp, 1, two_ff), w_down,
      b_down.reshape(n_exp, 1, d))


def _combine_kernel(slot_ref, gate_ref, ys_ref, x1_ref, g2_ref, lng_ref, lnb_ref, o_ref,
                    buf, acc_ref, sem):
    j = pl.program_id(0)
    tokens = o_ref.shape[0]
    n_tokens = tokens * pl.num_programs(0)
    n_blocks = o_ref.shape[1] // LANES
    tile_rows = TOP_K * SUBLANES

    def gather_token(block, which, i):
        for kk in range(TOP_K):
            row = slot_ref[kk * n_tokens + block * tokens + i]
            src = ys_ref.at[pl.ds(pl.multiple_of(row * SUBLANES, SUBLANES), SUBLANES), :]
            dst = buf.at[which, pl.ds(pl.multiple_of((i * TOP_K + kk) * SUBLANES, SUBLANES),
                                      SUBLANES), :]
            pltpu.make_async_copy(src, dst, sem.at[which]).start(priority=kk % 2)

    def reduce_token(which, i):
        tile = None
        for kk in range(TOP_K):
            row0 = pl.multiple_of((i * TOP_K + kk) * SUBLANES, SUBLANES)
            part = gate_ref[kk * n_tokens + j * tokens + i] * buf[which, pl.ds(row0, SUBLANES), :]
            tile = part if tile is None else tile + part
        acc_ref[pl.ds(pl.multiple_of(i * SUBLANES, SUBLANES), SUBLANES), :] = tile

    @pl.when(j == 0)
    def _():
        def first(i, carry):
            gather_token(0, 0, i)
            return carry
        lax.fori_loop(0, tokens, first, 0, unroll=8)

    def block(cur):
        pltpu.make_async_copy(ys_ref.at[pl.ds(0, tokens * tile_rows), :], buf.at[cur],
                              sem.at[cur]).wait()

        @pl.when(j + 1 < pl.num_programs(0))
        def _():
            def step(i, carry):
                gather_token(j + 1, 1 - cur, i)
                reduce_token(cur, i)
                return carry
            lax.fori_loop(0, tokens, step, 0, unroll=16)

        @pl.when(j + 1 == pl.num_programs(0))
        def _():
            def step(i, carry):
                reduce_token(cur, i)
                return carry
            lax.fori_loop(0, tokens, step, 0, unroll=8)

    for parity in range(2):
        pl.when(j % 2 == parity)(functools.partial(block, parity))

    ffn = jnp.concatenate([acc_ref[pl.ds(s, tokens, stride=SUBLANES), :] for s in range(n_blocks)],
                          axis=1)
    z = DEEPNORM_ALPHA * x1_ref[...] + g2_ref[0] * ffn
    o_ref[...] = _normalize(z) * lng_ref[...] + lnb_ref[...]


def _combine(slot, gates_flat, ys, x1, g2, ln_g, ln_b, tokens_per_sample):
    n, d = x1.shape
    tokens = min(COMBINE_TOKENS, tokens_per_sample)
    assert tokens_per_sample % tokens == 0
    blocks_per_sample = tokens_per_sample // tokens
    vec = pl.BlockSpec((1, d), lambda j, *_: (0, 0))
    grid_spec = pltpu.PrefetchScalarGridSpec(
        num_scalar_prefetch=2,
        grid=(n // tokens,),
        in_specs=[pl.BlockSpec(memory_space=pl.ANY),
                  pl.BlockSpec((tokens, d), lambda j, *_: (j, 0)),
                  pl.BlockSpec((1, 1, d), lambda j, *_: (j // blocks_per_sample, 0, 0)),
                  vec, vec],
        out_specs=pl.BlockSpec((tokens, d), lambda j, *_: (j, 0)),
        scratch_shapes=[pltpu.VMEM((2, tokens * TOP_K * SUBLANES, LANES), F32),
                        pltpu.VMEM((tokens * SUBLANES, LANES), F32),
                        pltpu.SemaphoreType.DMA((2,))],
    )
    return pl.pallas_call(
        _combine_kernel,
        grid_spec=grid_spec,
        out_shape=jax.ShapeDtypeStruct((n, d), F32),
        compiler_params=_params("arbitrary"),
        name="combine",
    )(slot, gates_flat, ys, x1, g2, ln_g.reshape(1, d), ln_b.reshape(1, d))


def _layer(x, c, ctx, c_ctx, ada_w, ada_b, w_in, rpb, sgu_ln_g, sgu_ln_b, sgu_w, sgu_b, w_out,
           ln1_g, ln1_b, ln2_g, ln2_b, router_w, router_b, w_gu, b_gu, w_down, b_down):
    b, s, d = x.shape
    n = b * s
    assert s % (GRID_W * ATT_ROWS) == 0 and s // GRID_W >= NA_KH

    cond_rows = jnp.zeros((8, d), F32).at[:b].set(c).at[b].set(c_ctx)
    mod = _ada(cond_rows, ada_w, ada_b)
    sh1, sc1, g1, sh2, sc2, g2 = jnp.split(mod[:b], 6, axis=-1)
    csh1, csc1 = mod[b, :d], mod[b, d:2 * d]

    w_in_bf = w_in.astype(BF16)
    k_c, v_c = _ctx_kv(ctx, csh1, csc1, w_in_bf[:, NA_WIDTH:3 * NA_WIDTH])
    q, k, v, sgu = _proj_in(x, sh1, sc1, w_in_bf, sgu_ln_g, sgu_ln_b, sgu_w, sgu_b)
    att = _natten(q, k, v, k_c, v_c, _bias_table(rpb))
    x1, t_tiles, logits_t = _mix_out(att, sgu, x, w_out.astype(BF16), g1, sh2, sc2, ln1_g, ln1_b,
                                     router_w, router_b)

    expert_k, rank_k, gate_k, counts = _route(logits_t)
    n_tiles_max = n * TOP_K // EXPERT_ROWS + N_EXPERTS
    slot, tile_start, group_start, counts, padded = _plan(expert_k, rank_k, counts[:, 0])
    xs = _dispatch(slot, group_start, counts, padded, tile_start[N_EXPERTS:], t_tiles,
                   n_tiles_max * EXPERT_ROWS)
    ys = _experts(tile_start, group_start + counts, xs, w_gu, b_gu, w_down, b_down)
    out = _combine(slot, gate_k.reshape(-1), ys, x1.reshape(n, d), g2.reshape(b, 1, d),
                   ln2_g, ln2_b, s)
    return out.reshape(b, s, d)


def kernel(x, c, ctx, c_ctx, ada_w, ada_b, w_in, rpb, sgu_ln_g, sgu_ln_b, sgu_w, sgu_b, w_out,
           ln1_g, ln1_b, ln2_g, ln2_b, router_w, router_b, exp_w_gu, exp_b_gu, exp_w_down,
           exp_b_down):
    assert ada_w.shape[0] == DEPTH
    return _layer(x, c, ctx, c_ctx, ada_w[0], ada_b[0], w_in[0], rpb[0], sgu_ln_g[0], sgu_ln_b[0],
                  sgu_w[0], sgu_b[0], w_out[0], ln1_g[0], ln1_b[0], ln2_g[0], ln2_b[0],
                  router_w[0], router_b[0], exp_w_gu[0], exp_b_gu[0], exp_w_down[0], exp_b_down[0])
```

```python
import functools

import jax
import jax.numpy as jnp
from jax import lax
from jax.experimental import pallas as pl
from jax.experimental.pallas import tpu as pltpu

F32 = jnp.float32
BF16 = jnp.bfloat16

GRID_W = 64
NA_HEADS = 8
NA_HEAD_DIM = 64
NA_WIDTH = NA_HEADS * NA_HEAD_DIM
NA_KH = 8
NA_KW = 16
SGU_GROUPS = 4
SGU_GROUP_DIM = 128
SGU_WIDTH = SGU_GROUPS * SGU_GROUP_DIM
CHUNK = 128
N_EXPERTS = 32
TOP_K = 4
SWIGLU_LIMIT = 7.0
SWIGLU_ALPHA = 1.702
LN_EPS = 1e-5
DEPTH = 1
DEEPNORM_ALPHA = (2.0 * DEPTH) ** 0.25
MASKED = -1e30

SUBLANES = 8
LANES = 128
VMEM_LIMIT_BYTES = 52 * 1024 * 1024

ROW_TILE = 1024
ATT_ROWS = 8
ATT_HEAD_GROUP = 4
ROUTE_CHUNK = 2048
SCAN_BLOCK = 256
EXPERT_ROWS = 512
EXPERT_PATHS = 8
DISPATCH_TOKENS = 4096
DISPATCH_GROUP = 8
COMBINE_TOKENS = 512


def _params(*sem):
    return pltpu.CompilerParams(dimension_semantics=sem, vmem_limit_bytes=VMEM_LIMIT_BYTES)


def _dot_split3(a, b, dims):
    a_hi = a.astype(BF16)
    a_lo = (a - a_hi.astype(F32)).astype(BF16)
    b_hi = b.astype(BF16)
    b_lo = (b - b_hi.astype(F32)).astype(BF16)
    dot = functools.partial(lax.dot_general, dimension_numbers=dims, preferred_element_type=F32)
    return dot(a_hi, b_hi) + dot(a_hi, b_lo) + dot(a_lo, b_hi)


def _normalize(x):
    mu = jnp.mean(x, axis=-1, keepdims=True)
    xc = x - mu
    var = jnp.mean(xc * xc, axis=-1, keepdims=True)
    return xc * lax.rsqrt(var + LN_EPS)


def _ada_kernel(c_ref, w_ref, b_ref, o_ref):
    s = c_ref[...]
    s = s * jax.nn.sigmoid(s)
    o_ref[...] = _dot_split3(s, w_ref[...], (((1,), (0,)), ((), ()))) + b_ref[...]


def _ada(cond_rows, ada_w, ada_b):
    d = cond_rows.shape[1]
    n_out = ada_w.shape[1]
    return pl.pallas_call(
        _ada_kernel,
        grid=(n_out // d,),
        in_specs=[pl.BlockSpec((8, d), lambda j: (0, 0)),
                  pl.BlockSpec((d, d), lambda j: (0, j)),
                  pl.BlockSpec((1, d), lambda j: (0, j))],
        out_specs=pl.BlockSpec((8, d), lambda j: (0, j)),
        out_shape=jax.ShapeDtypeStruct((8, n_out), F32),
        compiler_params=_params("arbitrary"),
        name="ada",
    )(cond_rows, ada_w, ada_b.reshape(1, n_out))


def _ctx_kv_kernel(ctx_ref, sh_ref, sc_ref, w_ref, k_ref, v_ref):
    h = _normalize(ctx_ref[0]) * (1.0 + sc_ref[...]) + sh_ref[...]
    kv = jnp.dot(h.astype(BF16), w_ref[...], preferred_element_type=F32)
    k_ref[0] = kv[:, :NA_WIDTH].astype(BF16)
    v_ref[0] = kv[:, NA_WIDTH:].astype(BF16)


def _ctx_kv(ctx, csh1, csc1, w_kv):
    b, l, d = ctx.shape
    out = jax.ShapeDtypeStruct((b, l, NA_WIDTH), BF16)
    return pl.pallas_call(
        _ctx_kv_kernel,
        grid=(b,),
        in_specs=[pl.BlockSpec((1, l, d), lambda i: (i, 0, 0)),
                  pl.BlockSpec((1, d), lambda i: (0, 0)),
                  pl.BlockSpec((1, d), lambda i: (0, 0)),
                  pl.BlockSpec((d, 2 * NA_WIDTH), lambda i: (0, 0))],
        out_specs=[pl.BlockSpec((1, l, NA_WIDTH), lambda i: (i, 0, 0)),
                   pl.BlockSpec((1, l, NA_WIDTH), lambda i: (i, 0, 0))],
        out_shape=[out, out],
        compiler_params=_params("arbitrary"),
        name="ctx_kv",
    )(ctx, csh1.reshape(1, d), csc1.reshape(1, d), w_kv)


def _proj_in_kernel(x_ref, sh_ref, sc_ref, w_ref, lng_ref, lnb_ref, ws_ref, bs_ref,
                    q_ref, k_ref, v_ref, s_ref):
    h = (_normalize(x_ref[0]) * (1.0 + sc_ref[0]) + sh_ref[0]).astype(BF16)

    def proj(lo, width):
        return jnp.dot(h, w_ref[:, lo:lo + width], preferred_element_type=F32)

    q_ref[0] = (proj(0, NA_WIDTH) * (NA_HEAD_DIM ** -0.5)).astype(BF16)
    k_ref[0] = proj(NA_WIDTH, NA_WIDTH).astype(BF16)
    v_ref[0] = proj(2 * NA_WIDTH, NA_WIDTH).astype(BF16)
    u = jax.nn.gelu(proj(3 * NA_WIDTH, SGU_WIDTH))
    g = jax.nn.gelu(proj(3 * NA_WIDTH + SGU_WIDTH, SGU_WIDTH))
    gn = (_normalize(g) * lng_ref[...] + lnb_ref[...]).astype(BF16)
    rows = h.shape[0]
    for n in range(rows // CHUNK):
        r0 = n * CHUNK
        for grp in range(SGU_GROUPS):
            c0 = grp * SGU_GROUP_DIM
            mixed = jnp.dot(ws_ref[grp], gn[r0:r0 + CHUNK, c0:c0 + SGU_GROUP_DIM],
                            preferred_element_type=F32) + bs_ref[grp]
            s_ref[0, r0:r0 + CHUNK, c0:c0 + SGU_GROUP_DIM] = (
                u[r0:r0 + CHUNK, c0:c0 + SGU_GROUP_DIM] * mixed).astype(BF16)


def _proj_in(x, sh1, sc1, w_in, sgu_ln_g, sgu_ln_b, sgu_w, sgu_b):
    b, s, d = x.shape
    d_in = w_in.shape[1]
    tm = min(ROW_TILE, s)
    out = jax.ShapeDtypeStruct((b, s, NA_WIDTH), BF16)
    row_spec = pl.BlockSpec((1, tm, NA_WIDTH), lambda i, j: (i, j, 0))
    mod_spec = pl.BlockSpec((1, 1, d), lambda i, j: (i, 0, 0))
    bs = jnp.broadcast_to(sgu_b[:, :, None], (SGU_GROUPS, CHUNK, SGU_GROUP_DIM))
    return pl.pallas_call(
        _proj_in_kernel,
        grid=(b, s // tm),
        in_specs=[pl.BlockSpec((1, tm, d), lambda i, j: (i, j, 0)),
                  mod_spec, mod_spec,
                  pl.BlockSpec((d, d_in), lambda i, j: (0, 0)),
                  pl.BlockSpec((1, SGU_WIDTH), lambda i, j: (0, 0)),
                  pl.BlockSpec((1, SGU_WIDTH), lambda i, j: (0, 0)),
                  pl.BlockSpec((SGU_GROUPS, CHUNK, CHUNK), lambda i, j: (0, 0, 0)),
                  pl.BlockSpec((SGU_GROUPS, CHUNK, SGU_GROUP_DIM), lambda i, j: (0, 0, 0))],
        out_specs=[row_spec, row_spec, row_spec, row_spec],
        out_shape=[out, out, out, out],
        compiler_params=_params("parallel", "parallel"),
        name="proj_in",
    )(x, sh1.reshape(b, 1, d), sc1.reshape(b, 1, d), w_in,
      sgu_ln_g.reshape(1, SGU_WIDTH), sgu_ln_b.reshape(1, SGU_WIDTH), sgu_w.astype(BF16), bs)


def _bias_kernel(rpb_ref, o_ref, toe_ref):
    n_dr, n_dc = 2 * NA_KH - 1, 2 * NA_KW - 1
    pair = (GRID_W, 2 * GRID_W)
    c = lax.broadcasted_iota(jnp.int32, pair, 0)
    lane = lax.broadcasted_iota(jnp.int32, pair, 1)
    kc = lane % GRID_W
    cs = jnp.clip(c - NA_KW // 2, 0, GRID_W - NA_KW)
    valid = jnp.logical_and(kc >= cs, kc < cs + NA_KW)
    dc = kc - c + NA_KW - 1

    def expand(hd, carry):
        acc = jnp.zeros(pair, F32)
        for j in range(n_dc):
            acc = jnp.where(dc == j, rpb_ref[hd * n_dc + j], acc)
        toe_ref[hd] = jnp.where(valid, acc, MASKED)
        return carry

    lax.fori_loop(0, NA_HEADS * n_dr, expand, 0)

    first_half = lane < GRID_W
    for o in range(NA_KH):
        for h in range(NA_HEADS):
            for p in range(NA_KH // 2):
                dr = NA_KH - 1 - o + 2 * p
                block = jnp.where(first_half, toe_ref[h * n_dr + dr], toe_ref[h * n_dr + dr + 1])
                o_ref[o, h * GRID_W:(h + 1) * GRID_W, p * 2 * GRID_W:(p + 1) * 2 * GRID_W] = block


def _bias_table(rpb):
    heads, n_dr, n_dc = rpb.shape
    assert (heads, n_dr, n_dc) == (NA_HEADS, 2 * NA_KH - 1, 2 * NA_KW - 1)
    shape = (NA_KH, NA_HEADS * GRID_W, NA_KH * GRID_W)
    grid_spec = pltpu.PrefetchScalarGridSpec(
        num_scalar_prefetch=1,
        grid=(1,),
        in_specs=[],
        out_specs=pl.BlockSpec(shape, lambda i, rpb_ref: (0, 0, 0)),
        scratch_shapes=[pltpu.VMEM((heads * n_dr, GRID_W, 2 * GRID_W), F32)],
    )
    return pl.pallas_call(
        _bias_kernel,
        grid_spec=grid_spec,
        out_shape=jax.ShapeDtypeStruct(shape, F32),
        compiler_params=_params("arbitrary"),
        name="bias_table",
    )(rpb.astype(F32).reshape(-1))


def _natten_kernel(q_ref, k_ref, v_ref, kc_ref, vc_ref, bias_ref, o_ref, *, grid_rows):
    group_lanes = ATT_HEAD_GROUP * NA_HEAD_DIM
    group_rows = ATT_HEAD_GROUP * GRID_W
    row_head = lax.broadcasted_iota(jnp.int32, (group_rows, group_lanes), 0) // GRID_W
    lane_head = lax.broadcasted_iota(jnp.int32, (group_rows, group_lanes), 1) // NA_HEAD_DIM
    own_head = row_head == lane_head
    nt = (((1,), (1,)), ((), ()))

    def one_row(i, carry):
        r = pl.program_id(1) * ATT_ROWS + i
        rs = jnp.clip(r - NA_KH // 2, 0, grid_rows - NA_KH)
        k0 = pl.multiple_of(rs * GRID_W, GRID_W)
        q0 = pl.multiple_of(i * GRID_W, GRID_W)
        for grp in range(NA_HEADS // ATT_HEAD_GROUP):
            lanes = slice(grp * group_lanes, (grp + 1) * group_lanes)
            q = q_ref[0, pl.ds(q0, GRID_W), lanes]
            qs = jnp.where(own_head, jnp.concatenate([q] * ATT_HEAD_GROUP, axis=0),
                           jnp.zeros((), BF16))
            kr = k_ref[0, pl.ds(k0, NA_KH * GRID_W), lanes]
            vr = v_ref[0, pl.ds(k0, NA_KH * GRID_W), lanes]
            bias = bias_ref[r - rs, grp * group_rows:(grp + 1) * group_rows, :]
            s_nb = lax.dot_general(qs, kr, nt, preferred_element_type=F32) + bias
            s_cx = lax.dot_general(qs, kc_ref[0, :, lanes], nt, preferred_element_type=F32)
            m = jnp.maximum(jnp.max(s_nb, axis=-1, keepdims=True),
                            jnp.max(s_cx, axis=-1, keepdims=True))
            p_nb = jnp.exp(s_nb - m)
            p_cx = jnp.exp(s_cx - m)
            denom = jnp.sum(p_nb, axis=-1, keepdims=True) + jnp.sum(p_cx, axis=-1, keepdims=True)
            o = (jnp.dot(p_nb.astype(BF16), vr, preferred_element_type=F32)
                 + jnp.dot(p_cx.astype(BF16), vc_ref[0, :, lanes],
                           preferred_element_type=F32)) / denom
            o = jnp.where(own_head, o, 0.0)
            out = o[:GRID_W]
            for h in range(1, ATT_HEAD_GROUP):
                out = out + o[h * GRID_W:(h + 1) * GRID_W]
            o_ref[0, pl.ds(q0, GRID_W), lanes] = out.astype(BF16)
        return carry

    lax.fori_loop(0, ATT_ROWS, one_row, 0, unroll=True)


def _natten(q, k, v, k_c, v_c, bias):
    b, s, w = q.shape
    l = k_c.shape[1]
    grid_rows = s // GRID_W
    tq = ATT_ROWS * GRID_W
    full = pl.BlockSpec((1, s, w), lambda i, j: (i, 0, 0))
    ctx = pl.BlockSpec((1, l, w), lambda i, j: (i, 0, 0))
    return pl.pallas_call(
        functools.partial(_natten_kernel, grid_rows=grid_rows),
        grid=(b, grid_rows // ATT_ROWS),
        in_specs=[pl.BlockSpec((1, tq, w), lambda i, j: (i, j, 0)),
                  full, full, ctx, ctx,
                  pl.BlockSpec(bias.shape, lambda i, j: (0, 0, 0), pipeline_mode=pl.Buffered(1))],
        out_specs=pl.BlockSpec((1, tq, w), lambda i, j: (i, j, 0)),
        out_shape=jax.ShapeDtypeStruct((b, s, w), BF16),
        compiler_params=_params("parallel", "arbitrary"),
        name="natten",
    )(q, k, v, k_c, v_c, bias)


def _mix_out_kernel(att_ref, sgu_ref, x_ref, wo_ref, g1_ref, sh_ref, sc_ref, lng_ref, lnb_ref,
                    wr_ref, br_ref, x1_ref, t_ref, lg_ref):
    mix = (jnp.dot(att_ref[0], wo_ref[:NA_WIDTH, :], preferred_element_type=F32)
           + jnp.dot(sgu_ref[0], wo_ref[NA_WIDTH:, :], preferred_element_type=F32))
    x1 = _normalize(DEEPNORM_ALPHA * x_ref[0] + g1_ref[0] * mix) * lng_ref[...] + lnb_ref[...]
    x1_ref[0] = x1
    t = _normalize(x1) * (1.0 + sc_ref[0]) + sh_ref[0]
    rows = t.shape[0]
    for s in range(t.shape[1] // LANES):
        t_ref[pl.ds(s, rows, stride=SUBLANES), :] = t[:, s * LANES:(s + 1) * LANES]
    lg_ref[...] = _dot_split3(wr_ref[...], t, (((1,), (1,)), ((), ()))) + br_ref[...]


def _mix_out(att, sgu, x, w_out, g1, sh2, sc2, ln_g, ln_b, router_w, router_b):
    b, s, d = x.shape
    assert d == SUBLANES * LANES
    tm = min(ROW_TILE, s)
    nj = s // tm
    row = lambda width: pl.BlockSpec((1, tm, width), lambda i, j: (i, j, 0))
    mod_spec = pl.BlockSpec((1, 1, d), lambda i, j: (i, 0, 0))
    vec_spec = pl.BlockSpec((1, d), lambda i, j: (0, 0))
    return pl.pallas_call(
        _mix_out_kernel,
        grid=(b, nj),
        in_specs=[row(NA_WIDTH), row(SGU_WIDTH), row(d),
                  pl.BlockSpec((d, d), lambda i, j: (0, 0)),
                  mod_spec, mod_spec, mod_spec, vec_spec, vec_spec,
                  pl.BlockSpec((N_EXPERTS, d), lambda i, j: (0, 0)),
                  pl.BlockSpec((N_EXPERTS, 1), lambda i, j: (0, 0))],
        out_specs=[row(d),
                   pl.BlockSpec((tm * SUBLANES, LANES), lambda i, j: (i * nj + j, 0)),
                   pl.BlockSpec((N_EXPERTS, tm), lambda i, j: (0, i * nj + j))],
        out_shape=[jax.ShapeDtypeStruct((b, s, d), F32),
                   jax.ShapeDtypeStruct((b * s * SUBLANES, LANES), F32),
                   jax.ShapeDtypeStruct((N_EXPERTS, b * s), F32)],
        compiler_params=_params("parallel", "parallel"),
        name="mix_out",
    )(att, sgu, x, w_out, g1.reshape(b, 1, d), sh2.reshape(b, 1, d), sc2.reshape(b, 1, d),
      ln_g.reshape(1, d), ln_b.reshape(1, d), router_w.T, router_b.reshape(N_EXPERTS, 1))


def _route_kernel(lg_ref, e_ref, r_ref, g_ref, cnt_ref, run_ref):
    @pl.when(pl.program_id(0) == 0)
    def _():
        run_ref[...] = jnp.zeros_like(run_ref)

    logits = lg_ref[...]
    n_tok = logits.shape[1]
    expert = lax.broadcasted_iota(jnp.int32, logits.shape, 0)
    work = logits
    picks, tops = [], []
    for kk in range(TOP_K):
        m = jnp.max(work, axis=0, keepdims=True)
        first = jnp.min(jnp.where(work == m, expert, N_EXPERTS), axis=0, keepdims=True)
        pick = expert == first
        work = jnp.where(pick, -jnp.inf, work)
        picks.append(pick)
        tops.append(m)
        e_ref[kk:kk + 1, :] = first
    weights = [jnp.exp(m - tops[0]) for m in tops]
    denom = weights[0]
    for w in weights[1:]:
        denom = denom + w
    for kk in range(TOP_K):
        g_ref[kk:kk + 1, :] = weights[kk] / denom

    chosen = picks[0]
    for pick in picks[1:]:
        chosen = jnp.logical_or(chosen, pick)
    tri = (lax.broadcasted_iota(jnp.int32, (SCAN_BLOCK, SCAN_BLOCK), 0)
           <= lax.broadcasted_iota(jnp.int32, (SCAN_BLOCK, SCAN_BLOCK), 1)).astype(BF16)
    sel = jnp.where(chosen, 1.0, 0.0).astype(BF16)
    carry = run_ref[:, 0:1]
    for blk in range(n_tok // SCAN_BLOCK):
        lo = blk * SCAN_BLOCK
        run = jnp.dot(sel[:, lo:lo + SCAN_BLOCK], tri, preferred_element_type=F32) + carry
        for kk in range(TOP_K):
            rank = jnp.sum(jnp.where(picks[kk][:, lo:lo + SCAN_BLOCK], run - 1.0, 0.0),
                           axis=0, keepdims=True)
            r_ref[kk:kk + 1, lo:lo + SCAN_BLOCK] = rank.astype(jnp.int32)
        carry = run[:, SCAN_BLOCK - 1:SCAN_BLOCK]
    run_ref[...] = jnp.broadcast_to(carry, run_ref.shape)
    cnt_ref[...] = jnp.broadcast_to(carry, cnt_ref.shape).astype(jnp.int32)


def _route(logits_t):
    n = logits_t.shape[1]
    chunk = min(ROUTE_CHUNK, n)
    assert n % chunk == 0 and chunk % SCAN_BLOCK == 0
    per_k = pl.BlockSpec((TOP_K, chunk), lambda c: (0, c))
    return pl.pallas_call(
        _route_kernel,
        grid=(n // chunk,),
        in_specs=[pl.BlockSpec((N_EXPERTS, chunk), lambda c: (0, c))],
        out_specs=[per_k, per_k, per_k, pl.BlockSpec((N_EXPERTS, LANES), lambda c: (0, 0))],
        out_shape=[jax.ShapeDtypeStruct((TOP_K, n), jnp.int32),
                   jax.ShapeDtypeStruct((TOP_K, n), jnp.int32),
                   jax.ShapeDtypeStruct((TOP_K, n), F32),
                   jax.ShapeDtypeStruct((N_EXPERTS, LANES), jnp.int32)],
        scratch_shapes=[pltpu.VMEM((N_EXPERTS, LANES), F32)],
        compiler_params=_params("arbitrary"),
        name="route",
    )(logits_t)


def _plan(expert_k, rank_k, counts):
    padded = (counts + EXPERT_ROWS - 1) // EXPERT_ROWS * EXPERT_ROWS
    group_end = jnp.cumsum(padded)
    group_start = group_end - padded
    start_k = jnp.sum(jnp.where(expert_k[None] == jnp.arange(N_EXPERTS)[:, None, None],
                                group_start[:, None, None], 0), axis=0)
    slot = (start_k + rank_k).reshape(-1)
    tile_start = jnp.concatenate([group_start, group_end[-1:]]) // EXPERT_ROWS
    return (slot.astype(jnp.int32), tile_start.astype(jnp.int32),
            group_start.astype(jnp.int32), counts.astype(jnp.int32), padded.astype(jnp.int32))


def _dispatch_kernel(slot_ref, start_ref, cnt_ref, pad_ref, nt_ref, t_ref, xs_ref, zero_ref,
                     sem, zsem):
    tokens = t_ref.shape[0] // SUBLANES
    n_tokens = tokens * pl.num_programs(0)
    base = pl.program_id(0) * tokens
    tile_rows = zero_ref.shape[0]
    n_tiles_max = xs_ref.shape[0] // tile_rows

    def row_copy(src, row, sem_):
        dst = xs_ref.at[pl.ds(pl.multiple_of(row * SUBLANES, SUBLANES), SUBLANES), :]
        return pltpu.make_async_copy(src, dst, sem_)

    def tile_copy(tile):
        dst = xs_ref.at[pl.ds(pl.multiple_of(tile * tile_rows, tile_rows), tile_rows), :]
        return pltpu.make_async_copy(zero_ref, dst, zsem)

    @pl.when(pl.program_id(0) == 0)
    def _():
        zero_ref[...] = jnp.zeros_like(zero_ref)

        def padded_tile(e):
            return (start_ref[e] + pad_ref[e]) // (tile_rows // SUBLANES) - 1

        def fill_group(e, carry):
            @pl.when(pad_ref[e] > cnt_ref[e])
            def _():
                tile_copy(padded_tile(e)).start()
            return carry

        def drain_group(e, carry):
            @pl.when(pad_ref[e] > cnt_ref[e])
            def _():
                tile_copy(padded_tile(e)).wait()
            return carry

        lax.fori_loop(0, N_EXPERTS, fill_group, 0)
        lax.fori_loop(0, N_EXPERTS, drain_group, 0)

        def fill_tile(i, carry):
            tile_copy(i).start()
            return carry

        def drain_tile(i, carry):
            tile_copy(i).wait()
            return carry

        lax.fori_loop(nt_ref[0], n_tiles_max, fill_tile, 0)
        lax.fori_loop(nt_ref[0], n_tiles_max, drain_tile, 0)

    def scatter(g, carry):
        for kk in range(TOP_K):
            for u in range(DISPATCH_GROUP):
                i = g * DISPATCH_GROUP + u
                src = t_ref.at[pl.ds(pl.multiple_of(i * SUBLANES, SUBLANES), SUBLANES), :]
                row_copy(src, slot_ref[kk * n_tokens + base + i], sem).start(priority=u % 2)
        return carry

    lax.fori_loop(0, tokens // DISPATCH_GROUP, scatter, 0)
    for kk in range(TOP_K):
        pltpu.make_async_copy(t_ref, xs_ref.at[pl.ds(0, tokens * SUBLANES), :], sem).wait()


def _dispatch(slot, group_start, counts, padded, n_tiles, t_tiles, n_rows):
    n = t_tiles.shape[0] // SUBLANES
    tokens = min(DISPATCH_TOKENS, n)
    assert n % tokens == 0 and n_rows % EXPERT_ROWS == 0
    grid_spec = pltpu.PrefetchScalarGridSpec(
        num_scalar_prefetch=5,
        grid=(n // tokens,),
        in_specs=[pl.BlockSpec((tokens * SUBLANES, LANES), lambda j, *_: (j, 0))],
        out_specs=pl.BlockSpec(memory_space=pl.ANY),
        scratch_shapes=[pltpu.VMEM((EXPERT_ROWS * SUBLANES, LANES), F32),
                        pltpu.SemaphoreType.DMA, pltpu.SemaphoreType.DMA],
    )
    return pl.pallas_call(
        _dispatch_kernel,
        grid_spec=grid_spec,
        out_shape=jax.ShapeDtypeStruct((n_rows * SUBLANES, LANES), F32),
        compiler_params=_params("arbitrary"),
        name="dispatch",
    )(slot, group_start, counts, padded, n_tiles, t_tiles)


def _experts_kernel(ts_ref, end_ref, xs_ref, wgu_ref, bgu_ref, wd_ref, bd_ref, ys_ref,
                    wgu_bf, wd_bf, xbuf, ybuf, xsem, ysem):
    e = pl.program_id(0)
    tile_rows = xbuf.shape[1]
    rows = tile_rows // SUBLANES
    d_ff = wd_ref.shape[1]
    n_blocks = wgu_ref.shape[1] // LANES
    n_total = ts_ref[N_EXPERTS]

    def hbm_tile(ref, t):
        return ref.at[pl.ds(pl.multiple_of(t * tile_rows, tile_rows), tile_rows), :]

    def x_copy(t, which):
        return pltpu.make_async_copy(hbm_tile(xs_ref, t), xbuf.at[which], xsem.at[which])

    def y_copy(t, which):
        return pltpu.make_async_copy(ybuf.at[which], hbm_tile(ys_ref, t), ysem.at[which])

    @pl.when(e == 0)
    def _():
        ybuf[...] = jnp.zeros_like(ybuf)
        x_copy(0, 0).start()

    @pl.when(ts_ref[e] < ts_ref[e + 1])
    def _():
        wgu_bf[...] = wgu_ref[0].astype(BF16)
        wd_bf[...] = wd_ref[0].astype(BF16)

        def one_tile(t, carry):
            cur = t % 2

            @pl.when(t + 1 < n_total)
            def _():
                x_copy(t + 1, 1 - cur).start()

            x_copy(t, cur).wait()

            @pl.when(t >= 2)
            def _():
                y_copy(t - 2, cur).wait()

            def ffn(n_rows):
                x = jnp.concatenate(
                    [xbuf[cur, pl.ds(s, n_rows, stride=SUBLANES), :] for s in range(n_blocks)],
                    axis=1).astype(BF16)
                gu = jnp.dot(x, wgu_bf[...], preferred_element_type=F32) + bgu_ref[0]
                gate = jnp.minimum(gu[:, :d_ff], SWIGLU_LIMIT)
                lin = jnp.clip(gu[:, d_ff:], -SWIGLU_LIMIT, SWIGLU_LIMIT)
                act = ((lin + 1.0) * (gate * jax.nn.sigmoid(SWIGLU_ALPHA * gate))).astype(BF16)
                y = jnp.dot(act, wd_bf[...], preferred_element_type=F32) + bd_ref[0]
                for s in range(n_blocks):
                    ybuf[cur, pl.ds(s, n_rows, stride=SUBLANES), :] = y[:, s * LANES:(s + 1) * LANES]

            live = end_ref[e] - t * rows

            step = rows // EXPERT_PATHS
            for q in range(1, EXPERT_PATHS + 1):
                lo = (q - 1) * step if q > 1 else -1
                cond = (live > lo if q == EXPERT_PATHS
                        else jnp.logical_and(live > lo, live <= q * step))
                pl.when(cond)(functools.partial(ffn, q * step))

            y_copy(t, cur).start()
            return carry

        lax.fori_loop(ts_ref[e], ts_ref[e + 1], one_tile, 0)

    @pl.when(e == pl.num_programs(0) - 1)
    def _():
        y_copy(n_total - 1, (n_total - 1) % 2).wait()

        @pl.when(n_total >= 2)
        def _():
            y_copy(n_total - 2, n_total % 2).wait()


def _experts(tile_start, token_end, xs, w_gu, b_gu, w_down, b_down):
    n_exp, d, two_ff = w_gu.shape
    d_ff = two_ff // 2
    by_expert = lambda e, ts, end: (e, 0, 0)
    tile_rows = EXPERT_ROWS * SUBLANES
    grid_spec = pltpu.PrefetchScalarGridSpec(
        num_scalar_prefetch=2,
        grid=(n_exp,),
        in_specs=[pl.BlockSpec(memory_space=pl.ANY),
                  pl.BlockSpec((1, d, two_ff), by_expert),
                  pl.BlockSpec((1, 1, two_ff), by_expert),
                  pl.BlockSpec((1, d_ff, d), by_expert),
                  pl.BlockSpec((1, 1, d), by_expert)],
        out_specs=pl.BlockSpec(memory_space=pl.ANY),
        scratch_shapes=[pltpu.VMEM((d, two_ff), BF16), pltpu.VMEM((d_ff, d), BF16),
                        pltpu.VMEM((2, tile_rows, LANES), F32),
                        pltpu.VMEM((2, tile_rows, LANES), F32),
                        pltpu.SemaphoreType.DMA((2,)), pltpu.SemaphoreType.DMA((2,))],
    )
    return pl.pallas_call(
        _experts_kernel,
        grid_spec=grid_spec,
        out_shape=jax.ShapeDtypeStruct(xs.shape, F32),
        input_output_aliases={2: 0},
        compiler_params=_params("arbitrary"),
        name="experts",
    )(tile_start, token_end, xs, w_gu, b_gu.reshape(n_exp, 1, two_ff), w_down,
      b_down.reshape(n_exp, 1, d))


def _combine_kernel(slot_ref, gate_ref, ys_ref, x1_ref, g2_ref, lng_ref, lnb_ref, o_ref,
                    buf, acc_ref, sem):
    j = pl.program_id(0)
    tokens = o_ref.shape[0]
    n_tokens = tokens * pl.num_programs(0)
    n_blocks = o_ref.shape[1] // LANES
    tile_rows = TOP_K * SUBLANES

    def gather_token(block, which, i):
        for kk in range(TOP_K):
            row = slot_ref[kk * n_tokens + block * tokens + i]
            src = ys_ref.at[pl.ds(pl.multiple_of(row * SUBLANES, SUBLANES), SUBLANES), :]
            dst = buf.at[which, pl.ds(pl.multiple_of((i * TOP_K + kk) * SUBLANES, SUBLANES),
                                      SUBLANES), :]
            pltpu.make_async_copy(src, dst, sem.at[which]).start(priority=kk % 2)

    def reduce_token(which, i):
        tile = None
        for kk in range(TOP_K):
            row0 = pl.multiple_of((i * TOP_K + kk) * SUBLANES, SUBLANES)
            part = gate_ref[kk * n_tokens + j * tokens + i] * buf[which, pl.ds(row0, SUBLANES), :]
            tile = part if tile is None else tile + part
        acc_ref[pl.ds(pl.multiple_of(i * SUBLANES, SUBLANES), SUBLANES), :] = tile

    @pl.when(j == 0)
    def _():
        def first(i, carry):
            gather_token(0, 0, i)
            return carry
        lax.fori_loop(0, tokens, first, 0, unroll=8)

    def block(cur):
        pltpu.make_async_copy(ys_ref.at[pl.ds(0, tokens * tile_rows), :], buf.at[cur],
                              sem.at[cur]).wait()

        @pl.when(j + 1 < pl.num_programs(0))
        def _():
            def step(i, carry):
                gather_token(j + 1, 1 - cur, i)
                reduce_token(cur, i)
                return carry
            lax.fori_loop(0, tokens, step, 0, unroll=16)

        @pl.when(j + 1 == pl.num_programs(0))
        def _():
            def step(i, carry):
                reduce_token(cur, i)
                return carry
            lax.fori_loop(0, tokens, step, 0, unroll=8)

    for parity in range(2):
        pl.when(j % 2 == parity)(functools.partial(block, parity))

    ffn = jnp.concatenate([acc_ref[pl.ds(s, tokens, stride=SUBLANES), :] for s in range(n_blocks)],
                          axis=1)
    z = DEEPNORM_ALPHA * x1_ref[...] + g2_ref[0] * ffn
    o_ref[...] = _normalize(z) * lng_ref[...] + lnb_ref[...]


def _combine(slot, gates_flat, ys, x1, g2, ln_g, ln_b, tokens_per_sample):
    n, d = x1.shape
    tokens = min(COMBINE_TOKENS, tokens_per_sample)
    assert tokens_per_sample % tokens == 0
    blocks_per_sample = tokens_per_sample // tokens
    vec = pl.BlockSpec((1, d), lambda j, *_: (0, 0))
    grid_spec = pltpu.PrefetchScalarGridSpec(
        num_scalar_prefetch=2,
        grid=(n // tokens,),
        in_specs=[pl.BlockSpec(memory_space=pl.ANY),
                  pl.BlockSpec((tokens, d), lambda j, *_: (j, 0)),
                  pl.BlockSpec((1, 1, d), lambda j, *_: (j // blocks_per_sample, 0, 0)),
                  vec, vec],
        out_specs=pl.BlockSpec((tokens, d), lambda j, *_: (j, 0)),
        scratch_shapes=[pltpu.VMEM((2, tokens * TOP_K * SUBLANES, LANES), F32),
                        pltpu.VMEM((tokens * SUBLANES, LANES), F32),
                        pltpu.SemaphoreType.DMA((2,))],
    )
    return pl.pallas_call(
        _combine_kernel,
        grid_spec=grid_spec,
        out_shape=jax.ShapeDtypeStruct((n, d), F32),
        compiler_params=_params("arbitrary"),
        name="combine",
    )(slot, gates_flat, ys, x1, g2, ln_g.reshape(1, d), ln_b.reshape(1, d))


def _layer(x, c, ctx, c_ctx, ada_w, ada_b, w_in, rpb, sgu_ln_g, sgu_ln_b, sgu_w, sgu_b, w_out,
           ln1_g, ln1_b, ln2_g, ln2_b, router_w, router_b, w_gu, b_gu, w_down, b_down):
    b, s, d = x.shape
    n = b * s
    assert s % (GRID_W * ATT_ROWS) == 0 and s // GRID_W >= NA_KH

    cond_rows = jnp.zeros((8, d), F32).at[:b].set(c).at[b].set(c_ctx)
    mod = _ada(cond_rows, ada_w, ada_b)
    sh1, sc1, g1, sh2, sc2, g2 = jnp.split(mod[:b], 6, axis=-1)
    csh1, csc1 = mod[b, :d], mod[b, d:2 * d]

    w_in_bf = w_in.astype(BF16)
    k_c, v_c = _ctx_kv(ctx, csh1, csc1, w_in_bf[:, NA_WIDTH:3 * NA_WIDTH])
    q, k, v, sgu = _proj_in(x, sh1, sc1, w_in_bf, sgu_ln_g, sgu_ln_b, sgu_w, sgu_b)
    att = _natten(q, k, v, k_c, v_c, _bias_table(rpb))
    x1, t_tiles, logits_t = _mix_out(att, sgu, x, w_out.astype(BF16), g1, sh2, sc2, ln1_g, ln1_b,
                                     router_w, router_b)

    expert_k, rank_k, gate_k, counts = _route(logits_t)
    n_tiles_max = n * TOP_K // EXPERT_ROWS + N_EXPERTS
    slot, tile_start, group_start, counts, padded = _plan(expert_k, rank_k, counts[:, 0])
    xs = _dispatch(slot, group_start, counts, padded, tile_start[N_EXPERTS:], t_tiles,
                   n_tiles_max * EXPERT_ROWS)
    ys = _experts(tile_start, group_start + counts, xs, w_gu, b_gu, w_down, b_down)
    out = _combine(slot, gate_k.reshape(-1), ys, x1.reshape(n, d), g2.reshape(b, 1, d),
                   ln2_g, ln2_b, s)
    return out.reshape(b, s, d)


def kernel(x, c, ctx, c_ctx, ada_w, ada_b, w_in, rpb, sgu_ln_g, sgu_ln_b, sgu_w, sgu_b, w_out,
           ln1_g, ln1_b, ln2_g, ln2_b, router_w, router_b, exp_w_gu, exp_b_gu, exp_w_down,
           exp_b_down):
    assert ada_w.shape[0] == DEPTH
    return _layer(x, c, ctx, c_ctx, ada_w[0], ada_b[0], w_in[0], rpb[0], sgu_ln_g[0], sgu_ln_b[0],
                  sgu_w[0], sgu_b[0], w_out[0], ln1_g[0], ln1_b[0], ln2_g[0], ln2_b[0],
                  router_w[0], router_b[0], exp_w_gu[0], exp_b_gu[0], exp_w_down[0], exp_b_down[0])
```

```python
import functools

import jax
import jax.numpy as jnp
from jax import lax
from jax.experimental import pallas as pl
from jax.experimental.pallas import tpu as pltpu

F32 = jnp.float32
BF16 = jnp.bfloat16

GRID_W = 64
NA_HEADS = 8
NA_HEAD_DIM = 64
NA_WIDTH = NA_HEADS * NA_HEAD_DIM
NA_KH = 8
NA_KW = 16
SGU_GROUPS = 4
SGU_GROUP_DIM = 128
SGU_WIDTH = SGU_GROUPS * SGU_GROUP_DIM
CHUNK = 128
N_EXPERTS = 32
TOP_K = 4
SWIGLU_LIMIT = 7.0
SWIGLU_ALPHA = 1.702
LN_EPS = 1e-5
DEPTH = 1
DEEPNORM_ALPHA = (2.0 * DEPTH) ** 0.25
MASKED = -1e30

SUBLANES = 8
LANES = 128
VMEM_LIMIT_BYTES = 52 * 1024 * 1024

ROW_TILE = 1024
ATT_ROWS = 8
ATT_HEAD_GROUP = 4
ROUTE_CHUNK = 2048
SCAN_BLOCK = 256
EXPERT_ROWS = 512
EXPERT_PATHS = 8
DISPATCH_TOKENS = 4096
COMBINE_TOKENS = 512


def _params(*sem):
    return pltpu.CompilerParams(dimension_semantics=sem, vmem_limit_bytes=VMEM_LIMIT_BYTES)


def _dot_split3(a, b, dims):
    a_hi = a.astype(BF16)
    a_lo = (a - a_hi.astype(F32)).astype(BF16)
    b_hi = b.astype(BF16)
    b_lo = (b - b_hi.astype(F32)).astype(BF16)
    dot = functools.partial(lax.dot_general, dimension_numbers=dims, preferred_element_type=F32)
    return dot(a_hi, b_hi) + dot(a_hi, b_lo) + dot(a_lo, b_hi)


def _normalize(x):
    mu = jnp.mean(x, axis=-1, keepdims=True)
    xc = x - mu
    var = jnp.mean(xc * xc, axis=-1, keepdims=True)
    return xc * lax.rsqrt(var + LN_EPS)


def _ada_kernel(c_ref, w_ref, b_ref, o_ref):
    s = c_ref[...]
    s = s * jax.nn.sigmoid(s)
    o_ref[...] = _dot_split3(s, w_ref[...], (((1,), (0,)), ((), ()))) + b_ref[...]


def _ada(cond_rows, ada_w, ada_b):
    d = cond_rows.shape[1]
    n_out = ada_w.shape[1]
    return pl.pallas_call(
        _ada_kernel,
        grid=(n_out // d,),
        in_specs=[pl.BlockSpec((8, d), lambda j: (0, 0)),
                  pl.BlockSpec((d, d), lambda j: (0, j)),
                  pl.BlockSpec((1, d), lambda j: (0, j))],
        out_specs=pl.BlockSpec((8, d), lambda j: (0, j)),
        out_shape=jax.ShapeDtypeStruct((8, n_out), F32),
        compiler_params=_params("arbitrary"),
        name="ada",
    )(cond_rows, ada_w, ada_b.reshape(1, n_out))


def _ctx_kv_kernel(ctx_ref, sh_ref, sc_ref, w_ref, k_ref, v_ref):
    h = _normalize(ctx_ref[0]) * (1.0 + sc_ref[...]) + sh_ref[...]
    kv = jnp.dot(h.astype(BF16), w_ref[...], preferred_element_type=F32)
    k_ref[0] = kv[:, :NA_WIDTH].astype(BF16)
    v_ref[0] = kv[:, NA_WIDTH:].astype(BF16)


def _ctx_kv(ctx, csh1, csc1, w_kv):
    b, l, d = ctx.shape
    out = jax.ShapeDtypeStruct((b, l, NA_WIDTH), BF16)
    return pl.pallas_call(
        _ctx_kv_kernel,
        grid=(b,),
        in_specs=[pl.BlockSpec((1, l, d), lambda i: (i, 0, 0)),
                  pl.BlockSpec((1, d), lambda i: (0, 0)),
                  pl.BlockSpec((1, d), lambda i: (0, 0)),
                  pl.BlockSpec((d, 2 * NA_WIDTH), lambda i: (0, 0))],
        out_specs=[pl.BlockSpec((1, l, NA_WIDTH), lambda i: (i, 0, 0)),
                   pl.BlockSpec((1, l, NA_WIDTH), lambda i: (i, 0, 0))],
        out_shape=[out, out],
        compiler_params=_params("arbitrary"),
        name="ctx_kv",
    )(ctx, csh1.reshape(1, d), csc1.reshape(1, d), w_kv)


def _proj_in_kernel(x_ref, sh_ref, sc_ref, w_ref, lng_ref, lnb_ref, ws_ref, bs_ref,
                    q_ref, k_ref, v_ref, s_ref):
    h = (_normalize(x_ref[0]) * (1.0 + sc_ref[0]) + sh_ref[0]).astype(BF16)

    def proj(lo, width):
        return jnp.dot(h, w_ref[:, lo:lo + width], preferred_element_type=F32)

    q_ref[0] = (proj(0, NA_WIDTH) * (NA_HEAD_DIM ** -0.5)).astype(BF16)
    k_ref[0] = proj(NA_WIDTH, NA_WIDTH).astype(BF16)
    v_ref[0] = proj(2 * NA_WIDTH, NA_WIDTH).astype(BF16)
    u = jax.nn.gelu(proj(3 * NA_WIDTH, SGU_WIDTH))
    g = jax.nn.gelu(proj(3 * NA_WIDTH + SGU_WIDTH, SGU_WIDTH))
    gn = (_normalize(g) * lng_ref[...] + lnb_ref[...]).astype(BF16)
    rows = h.shape[0]
    for n in range(rows // CHUNK):
        r0 = n * CHUNK
        for grp in range(SGU_GROUPS):
            c0 = grp * SGU_GROUP_DIM
            mixed = jnp.dot(ws_ref[grp], gn[r0:r0 + CHUNK, c0:c0 + SGU_GROUP_DIM],
                            preferred_element_type=F32) + bs_ref[grp]
            s_ref[0, r0:r0 + CHUNK, c0:c0 + SGU_GROUP_DIM] = (
                u[r0:r0 + CHUNK, c0:c0 + SGU_GROUP_DIM] * mixed).astype(BF16)


def _proj_in(x, sh1, sc1, w_in, sgu_ln_g, sgu_ln_b, sgu_w, sgu_b):
    b, s, d = x.shape
    d_in = w_in.shape[1]
    tm = min(ROW_TILE, s)
    out = jax.ShapeDtypeStruct((b, s, NA_WIDTH), BF16)
    row_spec = pl.BlockSpec((1, tm, NA_WIDTH), lambda i, j: (i, j, 0))
    mod_spec = pl.BlockSpec((1, 1, d), lambda i, j: (i, 0, 0))
    bs = jnp.broadcast_to(sgu_b[:, :, None], (SGU_GROUPS, CHUNK, SGU_GROUP_DIM))
    return pl.pallas_call(
        _proj_in_kernel,
        grid=(b, s // tm),
        in_specs=[pl.BlockSpec((1, tm, d), lambda i, j: (i, j, 0)),
                  mod_spec, mod_spec,
                  pl.BlockSpec((d, d_in), lambda i, j: (0, 0)),
                  pl.BlockSpec((1, SGU_WIDTH), lambda i, j: (0, 0)),
                  pl.BlockSpec((1, SGU_WIDTH), lambda i, j: (0, 0)),
                  pl.BlockSpec((SGU_GROUPS, CHUNK, CHUNK), lambda i, j: (0, 0, 0)),
                  pl.BlockSpec((SGU_GROUPS, CHUNK, SGU_GROUP_DIM), lambda i, j: (0, 0, 0))],
        out_specs=[row_spec, row_spec, row_spec, row_spec],
        out_shape=[out, out, out, out],
        compiler_params=_params("parallel", "parallel"),
        name="proj_in",
    )(x, sh1.reshape(b, 1, d), sc1.reshape(b, 1, d), w_in,
      sgu_ln_g.reshape(1, SGU_WIDTH), sgu_ln_b.reshape(1, SGU_WIDTH), sgu_w.astype(BF16), bs)


def _bias_kernel(rpb_ref, o_ref, toe_ref):
    n_dr, n_dc = 2 * NA_KH - 1, 2 * NA_KW - 1
    pair = (GRID_W, 2 * GRID_W)
    c = lax.broadcasted_iota(jnp.int32, pair, 0)
    lane = lax.broadcasted_iota(jnp.int32, pair, 1)
    kc = lane % GRID_W
    cs = jnp.clip(c - NA_KW // 2, 0, GRID_W - NA_KW)
    valid = jnp.logical_and(kc >= cs, kc < cs + NA_KW)
    dc = kc - c + NA_KW - 1

    def expand(hd, carry):
        acc = jnp.zeros(pair, F32)
        for j in range(n_dc):
            acc = jnp.where(dc == j, rpb_ref[hd * n_dc + j], acc)
        toe_ref[hd] = jnp.where(valid, acc, MASKED)
        return carry

    lax.fori_loop(0, NA_HEADS * n_dr, expand, 0)

    first_half = lane < GRID_W
    for o in range(NA_KH):
        for h in range(NA_HEADS):
            for p in range(NA_KH // 2):
                dr = NA_KH - 1 - o + 2 * p
                block = jnp.where(first_half, toe_ref[h * n_dr + dr], toe_ref[h * n_dr + dr + 1])
                o_ref[o, h * GRID_W:(h + 1) * GRID_W, p * 2 * GRID_W:(p + 1) * 2 * GRID_W] = block


def _bias_table(rpb):
    heads, n_dr, n_dc = rpb.shape
    assert (heads, n_dr, n_dc) == (NA_HEADS, 2 * NA_KH - 1, 2 * NA_KW - 1)
    shape = (NA_KH, NA_HEADS * GRID_W, NA_KH * GRID_W)
    grid_spec = pltpu.PrefetchScalarGridSpec(
        num_scalar_prefetch=1,
        grid=(1,),
        in_specs=[],
        out_specs=pl.BlockSpec(shape, lambda i, rpb_ref: (0, 0, 0)),
        scratch_shapes=[pltpu.VMEM((heads * n_dr, GRID_W, 2 * GRID_W), F32)],
    )
    return pl.pallas_call(
        _bias_kernel,
        grid_spec=grid_spec,
        out_shape=jax.ShapeDtypeStruct(shape, F32),
        compiler_params=_params("arbitrary"),
        name="bias_table",
    )(rpb.astype(F32).reshape(-1))


def _natten_kernel(q_ref, k_ref, v_ref, kc_ref, vc_ref, bias_ref, o_ref, *, grid_rows):
    group_lanes = ATT_HEAD_GROUP * NA_HEAD_DIM
    group_rows = ATT_HEAD_GROUP * GRID_W
    row_head = lax.broadcasted_iota(jnp.int32, (group_rows, group_lanes), 0) // GRID_W
    lane_head = lax.broadcasted_iota(jnp.int32, (group_rows, group_lanes), 1) // NA_HEAD_DIM
    own_head = row_head == lane_head
    nt = (((1,), (1,)), ((), ()))

    def one_row(i, carry):
        r = pl.program_id(1) * ATT_ROWS + i
        rs = jnp.clip(r - NA_KH // 2, 0, grid_rows - NA_KH)
        k0 = pl.multiple_of(rs * GRID_W, GRID_W)
        q0 = pl.multiple_of(i * GRID_W, GRID_W)
        for grp in range(NA_HEADS // ATT_HEAD_GROUP):
            lanes = slice(grp * group_lanes, (grp + 1) * group_lanes)
            q = q_ref[0, pl.ds(q0, GRID_W), lanes]
            qs = jnp.where(own_head, jnp.concatenate([q] * ATT_HEAD_GROUP, axis=0),
                           jnp.zeros((), BF16))
            kr = k_ref[0, pl.ds(k0, NA_KH * GRID_W), lanes]
            vr = v_ref[0, pl.ds(k0, NA_KH * GRID_W), lanes]
            bias = bias_ref[r - rs, grp * group_rows:(grp + 1) * group_rows, :]
            s_nb = lax.dot_general(qs, kr, nt, preferred_element_type=F32) + bias
            s_cx = lax.dot_general(qs, kc_ref[0, :, lanes], nt, preferred_element_type=F32)
            m = jnp.maximum(jnp.max(s_nb, axis=-1, keepdims=True),
                            jnp.max(s_cx, axis=-1, keepdims=True))
            p_nb = jnp.exp(s_nb - m)
            p_cx = jnp.exp(s_cx - m)
            denom = jnp.sum(p_nb, axis=-1, keepdims=True) + jnp.sum(p_cx, axis=-1, keepdims=True)
            o = (jnp.dot(p_nb.astype(BF16), vr, preferred_element_type=F32)
                 + jnp.dot(p_cx.astype(BF16), vc_ref[0, :, lanes],
                           preferred_element_type=F32)) / denom
            o = jnp.where(own_head, o, 0.0)
            out = o[:GRID_W]
            for h in range(1, ATT_HEAD_GROUP):
                out = out + o[h * GRID_W:(h + 1) * GRID_W]
            o_ref[0, pl.ds(q0, GRID_W), lanes] = out.astype(BF16)
        return carry

    lax.fori_loop(0, ATT_ROWS, one_row, 0, unroll=True)


def _natten(q, k, v, k_c, v_c, bias):
    b, s, w = q.shape
    l = k_c.shape[1]
    grid_rows = s // GRID_W
    tq = ATT_ROWS * GRID_W
    full = pl.BlockSpec((1, s, w), lambda i, j: (i, 0, 0))
    ctx = pl.BlockSpec((1, l, w), lambda i, j: (i, 0, 0))
    return pl.pallas_call(
        functools.partial(_natten_kernel, grid_rows=grid_rows),
        grid=(b, grid_rows // ATT_ROWS),
        in_specs=[pl.BlockSpec((1, tq, w), lambda i, j: (i, j, 0)),
                  full, full, ctx, ctx,
                  pl.BlockSpec(bias.shape, lambda i, j: (0, 0, 0), pipeline_mode=pl.Buffered(1))],
        out_specs=pl.BlockSpec((1, tq, w), lambda i, j: (i, j, 0)),
        out_shape=jax.ShapeDtypeStruct((b, s, w), BF16),
        compiler_params=_params("parallel", "arbitrary"),
        name="natten",
    )(q, k, v, k_c, v_c, bias)


def _mix_out_kernel(att_ref, sgu_ref, x_ref, wo_ref, g1_ref, sh_ref, sc_ref, lng_ref, lnb_ref,
                    wr_ref, br_ref, x1_ref, t_ref, lg_ref):
    mix = (jnp.dot(att_ref[0], wo_ref[:NA_WIDTH, :], preferred_element_type=F32)
           + jnp.dot(sgu_ref[0], wo_ref[NA_WIDTH:, :], preferred_element_type=F32))
    x1 = _normalize(DEEPNORM_ALPHA * x_ref[0] + g1_ref[0] * mix) * lng_ref[...] + lnb_ref[...]
    x1_ref[0] = x1
    t = _normalize(x1) * (1.0 + sc_ref[0]) + sh_ref[0]
    rows = t.shape[0]
    for s in range(t.shape[1] // LANES):
        t_ref[pl.ds(s, rows, stride=SUBLANES), :] = t[:, s * LANES:(s + 1) * LANES]
    lg_ref[...] = _dot_split3(wr_ref[...], t, (((1,), (1,)), ((), ()))) + br_ref[...]


def _mix_out(att, sgu, x, w_out, g1, sh2, sc2, ln_g, ln_b, router_w, router_b):
    b, s, d = x.shape
    assert d == SUBLANES * LANES
    tm = min(ROW_TILE, s)
    nj = s // tm
    row = lambda width: pl.BlockSpec((1, tm, width), lambda i, j: (i, j, 0))
    mod_spec = pl.BlockSpec((1, 1, d), lambda i, j: (i, 0, 0))
    vec_spec = pl.BlockSpec((1, d), lambda i, j: (0, 0))
    return pl.pallas_call(
        _mix_out_kernel,
        grid=(b, nj),
        in_specs=[row(NA_WIDTH), row(SGU_WIDTH), row(d),
                  pl.BlockSpec((d, d), lambda i, j: (0, 0)),
                  mod_spec, mod_spec, mod_spec, vec_spec, vec_spec,
                  pl.BlockSpec((N_EXPERTS, d), lambda i, j: (0, 0)),
                  pl.BlockSpec((N_EXPERTS, 1), lambda i, j: (0, 0))],
        out_specs=[row(d),
                   pl.BlockSpec((tm * SUBLANES, LANES), lambda i, j: (i * nj + j, 0)),
                   pl.BlockSpec((N_EXPERTS, tm), lambda i, j: (0, i * nj + j))],
        out_shape=[jax.ShapeDtypeStruct((b, s, d), F32),
                   jax.ShapeDtypeStruct((b * s * SUBLANES, LANES), F32),
                   jax.ShapeDtypeStruct((N_EXPERTS, b * s), F32)],
        compiler_params=_params("parallel", "parallel"),
        name="mix_out",
    )(att, sgu, x, w_out, g1.reshape(b, 1, d), sh2.reshape(b, 1, d), sc2.reshape(b, 1, d),
      ln_g.reshape(1, d), ln_b.reshape(1, d), router_w.T, router_b.reshape(N_EXPERTS, 1))


def _route_kernel(lg_ref, e_ref, r_ref, g_ref, cnt_ref, run_ref):
    @pl.when(pl.program_id(0) == 0)
    def _():
        run_ref[...] = jnp.zeros_like(run_ref)

    logits = lg_ref[...]
    n_tok = logits.shape[1]
    expert = lax.broadcasted_iota(jnp.int32, logits.shape, 0)
    work = logits
    picks, tops = [], []
    for kk in range(TOP_K):
        m = jnp.max(work, axis=0, keepdims=True)
        first = jnp.min(jnp.where(work == m, expert, N_EXPERTS), axis=0, keepdims=True)
        pick = expert == first
        work = jnp.where(pick, -jnp.inf, work)
        picks.append(pick)
        tops.append(m)
        e_ref[kk:kk + 1, :] = first
    weights = [jnp.exp(m - tops[0]) for m in tops]
    denom = weights[0]
    for w in weights[1:]:
        denom = denom + w
    for kk in range(TOP_K):
        g_ref[kk:kk + 1, :] = weights[kk] / denom

    chosen = picks[0]
    for pick in picks[1:]:
        chosen = jnp.logical_or(chosen, pick)
    tri = (lax.broadcasted_iota(jnp.int32, (SCAN_BLOCK, SCAN_BLOCK), 0)
           <= lax.broadcasted_iota(jnp.int32, (SCAN_BLOCK, SCAN_BLOCK), 1)).astype(BF16)
    sel = jnp.where(chosen, 1.0, 0.0).astype(BF16)
    carry = run_ref[:, 0:1]
    for blk in range(n_tok // SCAN_BLOCK):
        lo = blk * SCAN_BLOCK
        run = jnp.dot(sel[:, lo:lo + SCAN_BLOCK], tri, preferred_element_type=F32) + carry
        for kk in range(TOP_K):
            rank = jnp.sum(jnp.where(picks[kk][:, lo:lo + SCAN_BLOCK], run - 1.0, 0.0),
                           axis=0, keepdims=True)
            r_ref[kk:kk + 1, lo:lo + SCAN_BLOCK] = rank.astype(jnp.int32)
        carry = run[:, SCAN_BLOCK - 1:SCAN_BLOCK]
    run_ref[...] = jnp.broadcast_to(carry, run_ref.shape)
    cnt_ref[...] = jnp.broadcast_to(carry, cnt_ref.shape).astype(jnp.int32)


def _route(logits_t):
    n = logits_t.shape[1]
    chunk = min(ROUTE_CHUNK, n)
    assert n % chunk == 0 and chunk % SCAN_BLOCK == 0
    per_k = pl.BlockSpec((TOP_K, chunk), lambda c: (0, c))
    return pl.pallas_call(
        _route_kernel,
        grid=(n // chunk,),
        in_specs=[pl.BlockSpec((N_EXPERTS, chunk), lambda c: (0, c))],
        out_specs=[per_k, per_k, per_k, pl.BlockSpec((N_EXPERTS, LANES), lambda c: (0, 0))],
        out_shape=[jax.ShapeDtypeStruct((TOP_K, n), jnp.int32),
                   jax.ShapeDtypeStruct((TOP_K, n), jnp.int32),
                   jax.ShapeDtypeStruct((TOP_K, n), F32),
                   jax.ShapeDtypeStruct((N_EXPERTS, LANES), jnp.int32)],
        scratch_shapes=[pltpu.VMEM((N_EXPERTS, LANES), F32)],
        compiler_params=_params("arbitrary"),
        name="route",
    )(logits_t)


def _plan(expert_k, rank_k, counts):
    padded = (counts + EXPERT_ROWS - 1) // EXPERT_ROWS * EXPERT_ROWS
    group_end = jnp.cumsum(padded)
    group_start = group_end - padded
    start_k = jnp.sum(jnp.where(expert_k[None] == jnp.arange(N_EXPERTS)[:, None, None],
                                group_start[:, None, None], 0), axis=0)
    slot = (start_k + rank_k).reshape(-1)
    tile_start = jnp.concatenate([group_start, group_end[-1:]]) // EXPERT_ROWS
    return (slot.astype(jnp.int32), tile_start.astype(jnp.int32),
            group_start.astype(jnp.int32), counts.astype(jnp.int32), padded.astype(jnp.int32))


def _dispatch_kernel(slot_ref, start_ref, cnt_ref, pad_ref, nt_ref, t_ref, xs_ref, zero_ref,
                     sem, zsem):
    tokens = t_ref.shape[0] // SUBLANES
    n_tokens = tokens * pl.num_programs(0)
    base = pl.program_id(0) * tokens
    tile_rows = zero_ref.shape[0]
    n_tiles_max = xs_ref.shape[0] // tile_rows

    def row_copy(src, row, sem_):
        dst = xs_ref.at[pl.ds(pl.multiple_of(row * SUBLANES, SUBLANES), SUBLANES), :]
        return pltpu.make_async_copy(src, dst, sem_)

    def tile_copy(tile):
        dst = xs_ref.at[pl.ds(pl.multiple_of(tile * tile_rows, tile_rows), tile_rows), :]
        return pltpu.make_async_copy(zero_ref, dst, zsem)

    @pl.when(pl.program_id(0) == 0)
    def _():
        zero_ref[...] = jnp.zeros_like(zero_ref)

        def padded_tile(e):
            return (start_ref[e] + pad_ref[e]) // (tile_rows // SUBLANES) - 1

        def fill_group(e, carry):
            @pl.when(pad_ref[e] > cnt_ref[e])
            def _():
                tile_copy(padded_tile(e)).start()
            return carry

        def drain_group(e, carry):
            @pl.when(pad_ref[e] > cnt_ref[e])
            def _():
                tile_copy(padded_tile(e)).wait()
            return carry

        lax.fori_loop(0, N_EXPERTS, fill_group, 0)
        lax.fori_loop(0, N_EXPERTS, drain_group, 0)

        def fill_tile(i, carry):
            tile_copy(i).start()
            return carry

        def drain_tile(i, carry):
            tile_copy(i).wait()
            return carry

        lax.fori_loop(nt_ref[0], n_tiles_max, fill_tile, 0)
        lax.fori_loop(nt_ref[0], n_tiles_max, drain_tile, 0)

    def scatter(i, carry):
        src = t_ref.at[pl.ds(pl.multiple_of(i * SUBLANES, SUBLANES), SUBLANES), :]
        for kk in range(TOP_K):
            row_copy(src, slot_ref[kk * n_tokens + base + i], sem).start(priority=kk % 2)
        return carry

    lax.fori_loop(0, tokens, scatter, 0, unroll=8)
    for kk in range(TOP_K):
        pltpu.make_async_copy(t_ref, xs_ref.at[pl.ds(0, tokens * SUBLANES), :], sem).wait()


def _dispatch(slot, group_start, counts, padded, n_tiles, t_tiles, n_rows):
    n = t_tiles.shape[0] // SUBLANES
    tokens = min(DISPATCH_TOKENS, n)
    assert n % tokens == 0 and n_rows % EXPERT_ROWS == 0
    grid_spec = pltpu.PrefetchScalarGridSpec(
        num_scalar_prefetch=5,
        grid=(1,),
        in_specs=[pl.BlockSpec(memory_space=pl.ANY)],
        out_specs=pl.BlockSpec(memory_space=pl.ANY),
        scratch_shapes=[pltpu.VMEM((EXPERT_ROWS * SUBLANES, LANES), F32),
                        pltpu.SemaphoreType.DMA, pltpu.SemaphoreType.DMA],
    )
    return pl.pallas_call(
        _dispatch_kernel,
        grid_spec=grid_spec,
        out_shape=jax.ShapeDtypeStruct((n_rows * SUBLANES, LANES), F32),
        compiler_params=_params("arbitrary"),
        name="dispatch",
    )(slot, group_start, counts, padded, n_tiles, t_tiles)


def _experts_kernel(ts_ref, end_ref, xs_ref, wgu_ref, bgu_ref, wd_ref, bd_ref, ys_ref,
                    wgu_bf, wd_bf, xbuf, ybuf, xsem, ysem):
    e = pl.program_id(0)
    tile_rows = xbuf.shape[1]
    rows = tile_rows // SUBLANES
    d_ff = wd_ref.shape[1]
    n_blocks = wgu_ref.shape[1] // LANES
    n_total = ts_ref[N_EXPERTS]

    def hbm_tile(ref, t):
        return ref.at[pl.ds(pl.multiple_of(t * tile_rows, tile_rows), tile_rows), :]

    def x_copy(t, which):
        return pltpu.make_async_copy(hbm_tile(xs_ref, t), xbuf.at[which], xsem.at[which])

    def y_copy(t, which):
        return pltpu.make_async_copy(ybuf.at[which], hbm_tile(ys_ref, t), ysem.at[which])

    @pl.when(e == 0)
    def _():
        ybuf[...] = jnp.zeros_like(ybuf)
        x_copy(0, 0).start()

    @pl.when(ts_ref[e] < ts_ref[e + 1])
    def _():
        wgu_bf[...] = wgu_ref[0].astype(BF16)
        wd_bf[...] = wd_ref[0].astype(BF16)

        def one_tile(t, carry):
            cur = t % 2

            @pl.when(t + 1 < n_total)
            def _():
                x_copy(t + 1, 1 - cur).start()

            x_copy(t, cur).wait()

            @pl.when(t >= 2)
            def _():
                y_copy(t - 2, cur).wait()

            def ffn(n_rows):
                x = jnp.concatenate(
                    [xbuf[cur, pl.ds(s, n_rows, stride=SUBLANES), :] for s in range(n_blocks)],
                    axis=1).astype(BF16)
                gu = jnp.dot(x, wgu_bf[...], preferred_element_type=F32) + bgu_ref[0]
                gate = jnp.minimum(gu[:, :d_ff], SWIGLU_LIMIT)
                lin = jnp.clip(gu[:, d_ff:], -SWIGLU_LIMIT, SWIGLU_LIMIT)
                act = ((lin + 1.0) * (gate * jax.nn.sigmoid(SWIGLU_ALPHA * gate))).astype(BF16)
                y = jnp.dot(act, wd_bf[...], preferred_element_type=F32) + bd_ref[0]
                for s in range(n_blocks):
                    ybuf[cur, pl.ds(s, n_rows, stride=SUBLANES), :] = y[:, s * LANES:(s + 1) * LANES]

            live = end_ref[e] - t * rows

            step = rows // EXPERT_PATHS
            for q in range(1, EXPERT_PATHS + 1):
                lo = (q - 1) * step if q > 1 else -1
                cond = (live > lo if q == EXPERT_PATHS
                        else jnp.logical_and(live > lo, live <= q * step))
                pl.when(cond)(functools.partial(ffn, q * step))

            y_copy(t, cur).start()
            return carry

        lax.fori_loop(ts_ref[e], ts_ref[e + 1], one_tile, 0)

    @pl.when(e == pl.num_programs(0) - 1)
    def _():
        y_copy(n_total - 1, (n_total - 1) % 2).wait()

        @pl.when(n_total >= 2)
        def _():
            y_copy(n_total - 2, n_total % 2).wait()


def _experts(tile_start, token_end, xs, w_gu, b_gu, w_down, b_down):
    n_exp, d, two_ff = w_gu.shape
    d_ff = two_ff // 2
    by_expert = lambda e, ts, end: (e, 0, 0)
    tile_rows = EXPERT_ROWS * SUBLANES
    grid_spec = pltpu.PrefetchScalarGridSpec(
        num_scalar_prefetch=2,
        grid=(n_exp,),
        in_specs=[pl.BlockSpec(memory_space=pl.ANY),
                  pl.BlockSpec((1, d, two_ff), by_expert),
                  pl.BlockSpec((1, 1, two_ff), by_expert),
                  pl.BlockSpec((1, d_ff, d), by_expert),
                  pl.BlockSpec((1, 1, d), by_expert)],
        out_specs=pl.BlockSpec(memory_space=pl.ANY),
        scratch_shapes=[pltpu.VMEM((d, two_ff), BF16), pltpu.VMEM((d_ff, d), BF16),
                        pltpu.VMEM((2, tile_rows, LANES), F32),
                        pltpu.VMEM((2, tile_rows, LANES), F32),
                        pltpu.SemaphoreType.DMA((2,)), pltpu.SemaphoreType.DMA((2,))],
    )
    return pl.pallas_call(
        _experts_kernel,
        grid_spec=grid_spec,
        out_shape=jax.ShapeDtypeStruct(xs.shape, F32),
        input_output_aliases={2: 0},
        compiler_params=_params("arbitrary"),
        name="experts",
    )(tile_start, token_end, xs, w_gu, b_gu.reshape(n_exp, 1, two_ff), w_down,
      b_down.reshape(n_exp, 1, d))


def _combine_kernel(slot_ref, gate_ref, ys_ref, x1_ref, g2_ref, lng_ref, lnb_ref, o_ref,
                    buf, acc_ref, sem):
    j = pl.program_id(0)
    tokens = o_ref.shape[0]
    n_tokens = tokens * pl.num_programs(0)
    n_blocks = o_ref.shape[1] // LANES
    tile_rows = TOP_K * SUBLANES

    def gather_token(block, which, i):
        for kk in range(TOP_K):
            row = slot_ref[kk * n_tokens + block * tokens + i]
            src = ys_ref.at[pl.ds(pl.multiple_of(row * SUBLANES, SUBLANES), SUBLANES), :]
            dst = buf.at[which, pl.ds(pl.multiple_of((i * TOP_K + kk) * SUBLANES, SUBLANES),
                                      SUBLANES), :]
            pltpu.make_async_copy(src, dst, sem.at[which]).start(priority=kk % 2)

    def reduce_token(which, i):
        tile = None
        for kk in range(TOP_K):
            row0 = pl.multiple_of((i * TOP_K + kk) * SUBLANES, SUBLANES)
            part = gate_ref[kk * n_tokens + j * tokens + i] * buf[which, pl.ds(row0, SUBLANES), :]
            tile = part if tile is None else tile + part
        acc_ref[pl.ds(pl.multiple_of(i * SUBLANES, SUBLANES), SUBLANES), :] = tile

    @pl.when(j == 0)
    def _():
        def first(i, carry):
            gather_token(0, 0, i)
            return carry
        lax.fori_loop(0, tokens, first, 0, unroll=8)

    def block(cur):
        pltpu.make_async_copy(ys_ref.at[pl.ds(0, tokens * tile_rows), :], buf.at[cur],
                              sem.at[cur]).wait()

        @pl.when(j + 1 < pl.num_programs(0))
        def _():
            def step(i, carry):
                gather_token(j + 1, 1 - cur, i)
                reduce_token(cur, i)
                return carry
            lax.fori_loop(0, tokens, step, 0, unroll=16)

        @pl.when(j + 1 == pl.num_programs(0))
        def _():
            def step(i, carry):
                reduce_token(cur, i)
                return carry
            lax.fori_loop(0, tokens, step, 0, unroll=8)

    for parity in range(2):
        pl.when(j % 2 == parity)(functools.partial(block, parity))

    ffn = jnp.concatenate([acc_ref[pl.ds(s, tokens, stride=SUBLANES), :] for s in range(n_blocks)],
                          axis=1)
    z = DEEPNORM_ALPHA * x1_ref[...] + g2_ref[0] * ffn
    o_ref[...] = _normalize(z) * lng_ref[...] + lnb_ref[...]


def _combine(slot, gates_flat, ys, x1, g2, ln_g, ln_b, tokens_per_sample):
    n, d = x1.shape
    tokens = min(COMBINE_TOKENS, tokens_per_sample)
    assert tokens_per_sample % tokens == 0
    blocks_per_sample = tokens_per_sample // tokens
    vec = pl.BlockSpec((1, d), lambda j, *_: (0, 0))
    grid_spec = pltpu.PrefetchScalarGridSpec(
        num_scalar_prefetch=2,
        grid=(n // tokens,),
        in_specs=[pl.BlockSpec(memory_space=pl.ANY),
                  pl.BlockSpec((tokens, d), lambda j, *_: (j, 0)),
                  pl.BlockSpec((1, 1, d), lambda j, *_: (j // blocks_per_sample, 0, 0)),
                  vec, vec],
        out_specs=pl.BlockSpec((tokens, d), lambda j, *_: (j, 0)),
        scratch_shapes=[pltpu.VMEM((2, tokens * TOP_K * SUBLANES, LANES), F32),
                        pltpu.VMEM((tokens * SUBLANES, LANES), F32),
                        pltpu.SemaphoreType.DMA((2,))],
    )
    return pl.pallas_call(
        _combine_kernel,
        grid_spec=grid_spec,
        out_shape=jax.ShapeDtypeStruct((n, d), F32),
        compiler_params=_params("arbitrary"),
        name="combine",
    )(slot, gates_flat, ys, x1, g2, ln_g.reshape(1, d), ln_b.reshape(1, d))


def _layer(x, c, ctx, c_ctx, ada_w, ada_b, w_in, rpb, sgu_ln_g, sgu_ln_b, sgu_w, sgu_b, w_out,
           ln1_g, ln1_b, ln2_g, ln2_b, router_w, router_b, w_gu, b_gu, w_down, b_down):
    b, s, d = x.shape
    n = b * s
    assert s % (GRID_W * ATT_ROWS) == 0 and s // GRID_W >= NA_KH

    cond_rows = jnp.zeros((8, d), F32).at[:b].set(c).at[b].set(c_ctx)
    mod = _ada(cond_rows, ada_w, ada_b)
    sh1, sc1, g1, sh2, sc2, g2 = jnp.split(mod[:b], 6, axis=-1)
    csh1, csc1 = mod[b, :d], mod[b, d:2 * d]

    w_in_bf = w_in.astype(BF16)
    k_c, v_c = _ctx_kv(ctx, csh1, csc1, w_in_bf[:, NA_WIDTH:3 * NA_WIDTH])
    q, k, v, sgu = _proj_in(x, sh1, sc1, w_in_bf, sgu_ln_g, sgu_ln_b, sgu_w, sgu_b)
    att = _natten(q, k, v, k_c, v_c, _bias_table(rpb))
    x1, t_tiles, logits_t = _mix_out(att, sgu, x, w_out.astype(BF16), g1, sh2, sc2, ln1_g, ln1_b,
                                     router_w, router_b)

    expert_k, rank_k, gate_k, counts = _route(logits_t)
    n_tiles_max = n * TOP_K // EXPERT_ROWS + N_EXPERTS
    slot, tile_start, group_start, counts, padded = _plan(expert_k, rank_k, counts[:, 0])
    xs = _dispatch(slot, group_start, counts, padded, tile_start[N_EXPERTS:], t_tiles,
                   n_tiles_max * EXPERT_ROWS)
    ys = _experts(tile_start, group_start + counts, xs, w_gu, b_gu, w_down, b_down)
    out = _combine(slot, gate_k.reshape(-1), ys, x1.reshape(n, d), g2.reshape(b, 1, d),
                   ln2_g, ln2_b, s)
    return out.reshape(b, s, d)


def kernel(x, c, ctx, c_ctx, ada_w, ada_b, w_in, rpb, sgu_ln_g, sgu_ln_b, sgu_w, sgu_b, w_out,
           ln1_g, ln1_b, ln2_g, ln2_b, router_w, router_b, exp_w_gu, exp_b_gu, exp_w_down,
           exp_b_down):
    assert ada_w.shape[0] == DEPTH
    return _layer(x, c, ctx, c_ctx, ada_w[0], ada_b[0], w_in[0], rpb[0], sgu_ln_g[0], sgu_ln_b[0],
                  sgu_w[0], sgu_b[0], w_out[0], ln1_g[0], ln1_b[0], ln2_g[0], ln2_b[0],
                  router_w[0], router_b[0], exp_w_gu[0], exp_b_gu[0], exp_w_down[0], exp_b_down[0])
```
